```python
import math
import jax, jax.numpy as jnp
from jax import lax
import numpy as np

D_MODEL = 1024
BATCH = 2
SEQ = 16384
DEPTH = 4

ATTN_HEADS = 8
NOPE_DIM = 64
ROPE_DIM = 32
QK_DIM = NOPE_DIM + ROPE_DIM
V_DIM = 64
Q_RANK = 384
KV_RANK = 256
ATTN_WIDTH = ATTN_HEADS * V_DIM
Q_BLOCK = 128
ROPE_THETA = 10000.0
SG_GROUPS = 4
SG_CH = 64
SG_WIDTH = SG_GROUPS * SG_CH
SG_CHUNK = 128
SSM_GROUPS = 16
SSM_CH = 16
SSM_WIDTH = SSM_GROUPS * SSM_CH
SSM_STATE = 64
MIX_WIDTH = ATTN_WIDTH + SG_WIDTH + SSM_WIDTH
IN_COLS = Q_RANK + KV_RANK + ROPE_DIM + 2 * SG_WIDTH + SSM_WIDTH
OFF_KV = Q_RANK
OFF_KR = OFF_KV + KV_RANK
OFF_SG = OFF_KR + ROPE_DIM
OFF_SSM = OFF_SG + 2 * SG_WIDTH
MOE_GROUPS = 4
EXPERTS_PER_GROUP = 8
N_EXPERTS = MOE_GROUPS * EXPERTS_PER_GROUP
TOP_K = 2
D_EXPERT = 256
MOE_BLOCK = 128
PLE_DIM = 256
EPS = 1e-6

kernel_name = 'hybrid_mla_gmlp_s5_hmoe_trunk'


def rms_norm(x, g):
    xf = x.astype(jnp.float32)
    y = xf * lax.rsqrt(jnp.mean(xf * xf, axis=-1, keepdims=True) + EPS)
    return (y * g.astype(jnp.float32)).astype(x.dtype)


def layer_norm(x, g, b):
    xf = x.astype(jnp.float32)
    mu = jnp.mean(xf, axis=-1, keepdims=True)
    xc = xf - mu
    y = xc * lax.rsqrt(jnp.mean(xc * xc, axis=-1, keepdims=True) + EPS)
    return (y * g.astype(jnp.float32) + b.astype(jnp.float32)).astype(x.dtype)


def rope_tables(seq):
    inv = 1.0 / (ROPE_THETA ** (jnp.arange(0, ROPE_DIM, 2, dtype=jnp.float32) / ROPE_DIM))
    ang = jnp.arange(seq, dtype=jnp.float32)[:, None] * inv[None, :]
    return jnp.cos(ang), jnp.sin(ang)


def apply_rope(x, cos, sin):
    x1, x2 = jnp.split(x.astype(jnp.float32), 2, axis=-1)
    c = cos[None, :, None, :]
    s = sin[None, :, None, :]
    return jnp.concatenate([x1 * c - x2 * s, x1 * s + x2 * c], axis=-1).astype(x.dtype)


def mla_attention(z_q, z_kv, k_rope, g_cq, g_ckv, w_uq, w_ukv, g_qh, g_kh, cos, sin):
    B, S, _ = z_q.shape
    q = (rms_norm(z_q, g_cq) @ w_uq).reshape(B, S, ATTN_HEADS, QK_DIM)
    kv = (rms_norm(z_kv, g_ckv) @ w_ukv).reshape(B, S, ATTN_HEADS, NOPE_DIM + V_DIM)
    k_nope, v = kv[..., :NOPE_DIM], kv[..., NOPE_DIM:]
    k_r = jnp.broadcast_to(k_rope[:, :, None, :], (B, S, ATTN_HEADS, ROPE_DIM))
    k = jnp.concatenate([k_nope, k_r], axis=-1)
    q = rms_norm(q, g_qh)
    k = rms_norm(k, g_kh)
    q = jnp.concatenate([q[..., :NOPE_DIM], apply_rope(q[..., NOPE_DIM:], cos, sin)], axis=-1)
    k = jnp.concatenate([k[..., :NOPE_DIM], apply_rope(k[..., NOPE_DIM:], cos, sin)], axis=-1)
    scale = QK_DIM ** -0.5
    n_blk = S // Q_BLOCK
    qb = q.reshape(B, n_blk, Q_BLOCK, ATTN_HEADS, QK_DIM).transpose(1, 0, 2, 3, 4)
    kpos = jnp.arange(S)

    def block(args):
        qi, bi = args
        qpos = bi * Q_BLOCK + jnp.arange(Q_BLOCK)
        s = jnp.einsum('bqhd,bkhd->bhqk', qi, k, preferred_element_type=jnp.float32) * scale
        s = jnp.where(kpos[None, :] <= qpos[:, None], s, -jnp.inf)
        pr = jax.nn.softmax(s, axis=-1)
        return jnp.einsum('bhqk,bkhd->bqhd', pr.astype(v.dtype), v)

    o = lax.map(block, (qb, jnp.arange(n_blk)))
    return o.transpose(1, 0, 2, 3, 4).reshape(B, S, ATTN_WIDTH)


def spatial_gating(z, ln_g, ln_b, w_s, b_s):
    B, S, _ = z.shape
    z = jax.nn.gelu(z)
    u, v = z[..., :SG_WIDTH], z[..., SG_WIDTH:]
    v = layer_norm(v.reshape(B, S, SG_GROUPS, SG_CH),
                   ln_g.reshape(SG_GROUPS, SG_CH), ln_b.reshape(SG_GROUPS, SG_CH))
    vc = v.reshape(B, S // SG_CHUNK, SG_CHUNK, SG_GROUPS, SG_CH)
    causal = jnp.tril(jnp.ones((SG_CHUNK, SG_CHUNK), dtype=w_s.dtype))
    w = w_s * causal[None]
    mixed = jnp.einsum('gts,bcsgd->bctgd', w, vc) + b_s.T[None, None, :, :, None]
    return u * mixed.reshape(B, S, SG_WIDTH)


def _ssm_combine(e1, e2):
    a1r, a1i, b1r, b1i = e1
    a2r, a2i, b2r, b2i = e2
    return (a2r * a1r - a2i * a1i,
            a2r * a1i + a2i * a1r,
            a2r * b1r - a2i * b1i + b2r,
            a2r * b1i + a2i * b1r + b2i)


def s5_mixer(u, lam_re, lam_im, log_dt, b_re, b_im, c_re, c_im, d_skip, w_glu, b_glu):
    B, S, _ = u.shape
    f32 = jnp.float32
    x = u.reshape(B, S, SSM_GROUPS, SSM_CH).astype(f32)
    lr = jnp.minimum(lam_re.astype(f32), -1e-4)
    li = lam_im.astype(f32)
    dt = jnp.exp(log_dt.astype(f32))
    mag = jnp.exp(lr * dt)
    ab_re, ab_im = mag * jnp.cos(li * dt), mag * jnp.sin(li * dt)
    den = lr * lr + li * li
    nr, ni = ab_re - 1.0, ab_im
    fr = (nr * lr + ni * li) / den
    fi = (ni * lr - nr * li) / den
    br, bi = b_re.astype(f32), b_im.astype(f32)
    bb_re = fr[..., None] * br - fi[..., None] * bi
    bb_im = fr[..., None] * bi + fi[..., None] * br
    bu_re = jnp.einsum('gnc,bsgc->bsgn', bb_re, x)
    bu_im = jnp.einsum('gnc,bsgc->bsgn', bb_im, x)
    a_re = jnp.broadcast_to(ab_re, bu_re.shape)
    a_im = jnp.broadcast_to(ab_im, bu_im.shape)
    _, _, h_re, h_im = lax.associative_scan(_ssm_combine, (a_re, a_im, bu_re, bu_im), axis=1)
    y = (jnp.einsum('gcn,bsgn->bsgc', c_re.astype(f32), h_re)
         - jnp.einsum('gcn,bsgn->bsgc', c_im.astype(f32), h_im)
         + d_skip.astype(f32) * x)
    g = jax.nn.gelu(y.reshape(B, S, SSM_WIDTH)).astype(u.dtype)
    return g * jax.nn.sigmoid(g @ w_glu + b_glu)


def hierarchical_moe(h, w_rg, b_rg, w_rexp, b_rexp, w1, w3, w2):
    B, S, D = h.shape
    T = B * S
    xf = h.reshape(T, D)
    g_logits = (xf @ w_rg + b_rg).astype(jnp.float32)
    g_sel = jnp.argmax(g_logits, axis=-1)
    g_w = jnp.take_along_axis(jax.nn.softmax(g_logits, axis=-1), g_sel[:, None], axis=-1)[:, 0]
    e_logits = (xf @ w_rexp + b_rexp).astype(jnp.float32).reshape(T, MOE_GROUPS, EXPERTS_PER_GROUP)
    e_in = jnp.take_along_axis(e_logits, g_sel[:, None, None], axis=1)[:, 0]
    top_v, top_i = lax.top_k(e_in, TOP_K)
    e_w = jax.nn.softmax(top_v, axis=-1) * g_w[:, None]
    e_idx = g_sel[:, None].astype(jnp.int32) * EXPERTS_PER_GROUP + top_i.astype(jnp.int32)

    TK = T * TOP_K
    flat_e = e_idx.reshape(TK)
    flat_w = e_w.reshape(TK)
    flat_tok = jnp.repeat(jnp.arange(T, dtype=jnp.int32), TOP_K)
    order = jnp.argsort(flat_e)
    se = flat_e[order]
    counts = jnp.bincount(flat_e, length=N_EXPERTS)
    start = jnp.cumsum(counts) - counts
    pcounts = (counts + MOE_BLOCK - 1) // MOE_BLOCK * MOE_BLOCK
    pend = jnp.cumsum(pcounts)
    pstart = pend - pcounts
    dest = pstart[se] + (jnp.arange(TK, dtype=jnp.int32) - start[se])
    n_blocks = -(-(TK + N_EXPERTS * (MOE_BLOCK - 1)) // MOE_BLOCK)
    NP = n_blocks * MOE_BLOCK
    row_tok = jnp.full((NP,), T, dtype=jnp.int32).at[dest].set(flat_tok[order])
    row_w = jnp.zeros((NP,), jnp.float32).at[dest].set(flat_w[order])
    blk_e = jnp.minimum(jnp.searchsorted(pend, jnp.arange(n_blocks) * MOE_BLOCK, side='right'),
                        N_EXPERTS - 1)
    xpad = jnp.concatenate([xf, jnp.zeros((1, D), xf.dtype)], axis=0)

    def expert_block(args):
        tok, e = args
        xb = xpad[tok]
        hb = jax.nn.silu(xb @ w1[e]) * (xb @ w3[e])
        return hb @ w2[e]

    yb = lax.map(expert_block, (row_tok.reshape(n_blocks, MOE_BLOCK), blk_e))
    y = jnp.zeros((T + 1, D), jnp.float32).at[row_tok].add(
        yb.reshape(NP, D).astype(jnp.float32) * row_w[:, None])
    return y[:T].reshape(B, S, D).astype(h.dtype)


def setup_inputs(seed: int = 0) -> dict:
    key = jax.random.key(seed)
    ks = iter(jax.random.split(key, 48))
    f32 = jnp.float32

    def nrm(shape, scale):
        return jax.random.normal(next(ks), shape, f32) * scale

    def gain(shape):
        return 1.0 + nrm(shape, 0.02)

    L = DEPTH
    inp = {}
    inp['x'] = nrm((BATCH, SEQ, D_MODEL), 1.0)
    inp['p'] = nrm((DEPTH, BATCH, SEQ, PLE_DIM), 1.0)
    inp['ln1'] = gain((L, D_MODEL))
    inp['w_in'] = nrm((L, D_MODEL, IN_COLS), D_MODEL ** -0.5)
    inp['g_cq'] = gain((L, Q_RANK))
    inp['g_ckv'] = gain((L, KV_RANK))
    inp['w_uq'] = nrm((L, Q_RANK, ATTN_HEADS * QK_DIM), Q_RANK ** -0.5)
    inp['w_ukv'] = nrm((L, KV_RANK, ATTN_HEADS * (NOPE_DIM + V_DIM)), KV_RANK ** -0.5)
    inp['g_qh'] = gain((L, QK_DIM))
    inp['g_kh'] = gain((L, QK_DIM))
    inp['sg_ln_g'] = gain((L, SG_WIDTH))
    inp['sg_ln_b'] = nrm((L, SG_WIDTH), 0.02)
    inp['sg_w'] = nrm((L, SG_GROUPS, SG_CHUNK, SG_CHUNK), 0.5 * SG_CHUNK ** -0.5)
    inp['sg_b'] = 1.0 + nrm((L, SG_GROUPS, SG_CHUNK), 0.1)
    inp['lam_re'] = -0.5 + nrm((L, SSM_GROUPS, SSM_STATE), 0.01)
    inp['lam_im'] = (math.pi * jnp.arange(SSM_STATE, dtype=f32))[None, None, :] + nrm((L, SSM_GROUPS, SSM_STATE), 0.01)
    inp['log_dt'] = jax.random.uniform(next(ks), (L, SSM_GROUPS, SSM_STATE), f32, math.log(1e-3), math.log(1e-1))
    inp['ssm_b_re'] = nrm((L, SSM_GROUPS, SSM_STATE, SSM_CH), (2.0 * SSM_CH) ** -0.5)
    inp['ssm_b_im'] = nrm((L, SSM_GROUPS, SSM_STATE, SSM_CH), (2.0 * SSM_CH) ** -0.5)
    inp['ssm_c_re'] = nrm((L, SSM_GROUPS, SSM_CH, SSM_STATE), (2.0 * SSM_STATE) ** -0.5)
    inp['ssm_c_im'] = nrm((L, SSM_GROUPS, SSM_CH, SSM_STATE), (2.0 * SSM_STATE) ** -0.5)
    inp['ssm_d'] = nrm((L, SSM_GROUPS, SSM_CH), 1.0)
    inp['w_glu'] = nrm((L, SSM_WIDTH, SSM_WIDTH), SSM_WIDTH ** -0.5)
    inp['b_glu'] = nrm((L, SSM_WIDTH), 0.02)
    inp['g_mix'] = gain((L, MIX_WIDTH))
    inp['w_out'] = nrm((L, MIX_WIDTH, D_MODEL), MIX_WIDTH ** -0.5)
    inp['ln2'] = gain((L, D_MODEL))
    inp['w_rg'] = nrm((L, D_MODEL, MOE_GROUPS), D_MODEL ** -0.5)
    inp['b_rg'] = nrm((L, MOE_GROUPS), 0.01)
    inp['w_rexp'] = nrm((L, D_MODEL, N_EXPERTS), D_MODEL ** -0.5)
    inp['b_rexp'] = nrm((L, N_EXPERTS), 0.01)
    inp['w1'] = nrm((L, N_EXPERTS, D_MODEL, D_EXPERT), D_MODEL ** -0.5)
    inp['w3'] = nrm((L, N_EXPERTS, D_MODEL, D_EXPERT), D_MODEL ** -0.5)
    inp['w2'] = nrm((L, N_EXPERTS, D_EXPERT, D_MODEL), D_EXPERT ** -0.5)
    inp['w_pg'] = nrm((L, D_MODEL, D_MODEL), D_MODEL ** -0.5)
    inp['b_pg'] = nrm((L, D_MODEL), 0.02)
    inp['w_ple'] = nrm((L, PLE_DIM, D_MODEL), PLE_DIM ** -0.5)
    return inp


def reference(x, p, ln1, w_in, g_cq, g_ckv, w_uq, w_ukv, g_qh, g_kh, sg_ln_g, sg_ln_b, sg_w, sg_b,
              lam_re, lam_im, log_dt, ssm_b_re, ssm_b_im, ssm_c_re, ssm_c_im, ssm_d, w_glu, b_glu,
              g_mix, w_out, ln2, w_rg, b_rg, w_rexp, b_rexp, w1, w3, w2, w_pg, b_pg, w_ple):
    cos, sin = rope_tables(x.shape[1])
    h = x
    for i in range(DEPTH):
        z = rms_norm(h, ln1[i]) @ w_in[i]
        a = mla_attention(z[..., :OFF_KV], z[..., OFF_KV:OFF_KR], z[..., OFF_KR:OFF_SG],
                          g_cq[i], g_ckv[i], w_uq[i], w_ukv[i], g_qh[i], g_kh[i], cos, sin)
        g = spatial_gating(z[..., OFF_SG:OFF_SSM], sg_ln_g[i], sg_ln_b[i], sg_w[i], sg_b[i])
        s = s5_mixer(z[..., OFF_SSM:], lam_re[i], lam_im[i], log_dt[i], ssm_b_re[i], ssm_b_im[i],
                     ssm_c_re[i], ssm_c_im[i], ssm_d[i], w_glu[i], b_glu[i])
        gm = g_mix[i]
        mix = jnp.concatenate([
            rms_norm(a, gm[:ATTN_WIDTH]),
            rms_norm(g, gm[ATTN_WIDTH:ATTN_WIDTH + SG_WIDTH]),
            rms_norm(s, gm[ATTN_WIDTH + SG_WIDTH:]),
        ], axis=-1)
        h = h + mix @ w_out[i]
        h = h + hierarchical_moe(rms_norm(h, ln2[i]), w_rg[i], b_rg[i], w_rexp[i], b_rexp[i],
                                 w1[i], w3[i], w2[i])
        h = h + jax.nn.sigmoid(h @ w_pg[i] + b_pg[i]) * (p[i] @ w_ple[i])
    return h
```

```python
import functools
import math

import jax
import jax.numpy as jnp
from jax import lax
from jax.experimental import pallas as pl
from jax.experimental.pallas import tpu as pltpu

F32 = jnp.float32
BF16 = jnp.bfloat16

D_MODEL = 1024
ATTN_HEADS = 8
NOPE_DIM = 64
ROPE_DIM = 32
QK_DIM = NOPE_DIM + ROPE_DIM
V_DIM = 64
Q_RANK = 384
KV_RANK = 256
SG_GROUPS = 4
SG_CH = 64
SG_WIDTH = SG_GROUPS * SG_CH
SG_CHUNK = 128
SSM_GROUPS = 16
SSM_CH = 16
SSM_WIDTH = SSM_GROUPS * SSM_CH
SSM_STATE = 64
MOE_GROUPS = 4
EXPERTS_PER_GROUP = 8
N_EXPERTS = MOE_GROUPS * EXPERTS_PER_GROUP
D_EXPERT = 256
PLE_DIM = 256
ROPE_THETA = 10000.0
EPS = 1e-6

LANES = 128
HEAD_PAD = LANES
QKV_PAD = ATTN_HEADS * HEAD_PAD
SSM_CHUNK = 16
SSM_PAIR = 2 * SSM_CHUNK * SSM_CH
EXPERT_BLOCK = 256
VMEM_LIMIT = 56 * 1024 * 1024

C_Q = 0
C_KV = C_Q + Q_RANK
C_KRA = C_KV + KV_RANK
C_KRB = C_KRA + LANES
C_SG = C_KRB + LANES
C_SSM = C_SG + 2 * SG_WIDTH
C_END = C_SSM + SSM_WIDTH


def _dot(a, b):
    return jnp.dot(a, b, preferred_element_type=F32)


def _split_bf16(x):
    hi = x.astype(BF16)
    lo = (x - hi.astype(F32)).astype(BF16)
    return hi, lo


def _rms(x, g, width=None):
    width = x.shape[-1] if width is None else width
    ms = jnp.sum(x * x, axis=-1, keepdims=True) * (1.0 / width)
    return x * lax.rsqrt(ms + EPS) * g


def _mixer_in_kernel(h_ref, ln1_ref, win_ref, gcq_ref, gckv_ref, wqa_ref, wqb_ref, wk_ref, wv_ref,
                     cos_ref, sin_ref, gqk_ref, vones_ref, lng_ref, lnb_ref, sgw_ref, sgb_ref,
                     mmean_ref, q_out, k_out, v_out, sg_out, ssm_out, *, tm):
    xn = _rms(h_ref[...], ln1_ref[...])
    z = _dot(xn.astype(BF16), win_ref[...])

    cos = cos_ref[...]
    sin = sin_ref[...]
    gqk = gqk_ref[...]
    gqa = cos * gqk[0:1, :]
    gqb = sin * gqk[1:2, :]
    gka = cos * gqk[2:3, :]
    gkb = sin * gqk[3:4, :]
    scale = QK_DIM ** -0.5

    qn = _rms(z[:, C_Q:C_KV], gcq_ref[...]).astype(BF16)
    qa = _dot(qn, wqa_ref[...])
    qb = _dot(qn, wqb_ref[...])
    for hd in range(ATTN_HEADS):
        sl = slice(hd * HEAD_PAD, (hd + 1) * HEAD_PAD)
        a = qa[:, sl]
        r = lax.rsqrt(jnp.sum(a * a, axis=-1, keepdims=True) * (1.0 / QK_DIM) + EPS)
        q_out[:, sl] = ((a * gqa + qb[:, sl] * gqb) * (r * scale)).astype(BF16)

    kvn = _rms(z[:, C_KV:C_KRA], gckv_ref[...]).astype(BF16)
    ka = _dot(kvn, wk_ref[...])
    v_out[...] = (_dot(kvn, wv_ref[...]) + vones_ref[...]).astype(BF16)
    kra = z[:, C_KRA:C_KRB]
    krb = z[:, C_KRB:C_SG]
    ss_rope = jnp.sum(kra * kra, axis=-1, keepdims=True)
    krot = kra * gka + krb * gkb
    for hd in range(ATTN_HEADS):
        sl = slice(hd * HEAD_PAD, (hd + 1) * HEAD_PAD)
        a = ka[:, sl]
        r = lax.rsqrt((jnp.sum(a * a, axis=-1, keepdims=True) + ss_rope) * (1.0 / QK_DIM) + EPS)
        k_out[:, sl] = ((a * gka + krot) * r).astype(BF16)

    zg = jax.nn.gelu(z[:, C_SG:C_SSM])
    u = zg[:, :SG_WIDTH]
    vv = zg[:, SG_WIDTH:]
    mmean = mmean_ref[...]
    vh, vl = _split_bf16(vv)
    mu = _dot(vh, mmean) + _dot(vl, mmean)
    xc = vv - mu
    sh, sl_ = _split_bf16(xc * xc)
    var = _dot(sh, mmean) + _dot(sl_, mmean)
    vn = (xc * lax.rsqrt(var + EPS) * lng_ref[...] + lnb_ref[...]).astype(BF16)
    tri = (lax.broadcasted_iota(jnp.int32, (SG_CHUNK, SG_CHUNK), 1)
           <= lax.broadcasted_iota(jnp.int32, (SG_CHUNK, SG_CHUNK), 0))
    lane_grp = lax.broadcasted_iota(jnp.int32, (SG_CHUNK, SG_WIDTH), 1) // SG_CH
    ws = [jnp.where(tri, sgw_ref[g], 0.0).astype(BF16) for g in range(SG_GROUPS)]
    for c in range(tm // SG_CHUNK):
        rows = slice(c * SG_CHUNK, (c + 1) * SG_CHUNK)
        vc = vn[rows, :]
        mixed = sgb_ref[...]
        for g in range(SG_GROUPS):
            mixed = mixed + jnp.where(lane_grp == g, _dot(ws[g], vc), 0.0)
        sg_out[rows, :] = u[rows, :] * mixed

    ssm_out[...] = z[:, C_SSM:C_END].astype(BF16)


def _mixer_in(h, lp, cos128, sin128, tm, n_pos_tiles):
    t = h.shape[0]
    const = lambda *shape: pl.BlockSpec(shape, lambda i: (0,) * len(shape))
    rows = lambda w: pl.BlockSpec((tm, w), lambda i: (i, 0))
    pos = pl.BlockSpec((tm, LANES), lambda i: (i % n_pos_tiles, 0))
    return pl.pallas_call(
        functools.partial(_mixer_in_kernel, tm=tm),
        grid=(t // tm,),
        in_specs=[rows(D_MODEL), const(1, D_MODEL), const(D_MODEL, C_END), const(1, Q_RANK),
                  const(1, KV_RANK), const(Q_RANK, QKV_PAD), const(Q_RANK, QKV_PAD),
                  const(KV_RANK, QKV_PAD), const(KV_RANK, QKV_PAD), pos, pos, const(8, LANES),
                  const(1, QKV_PAD), const(1, SG_WIDTH), const(1, SG_WIDTH),
                  const(SG_GROUPS, SG_CHUNK, SG_CHUNK), const(SG_CHUNK, SG_WIDTH),
                  const(SG_WIDTH, SG_WIDTH)],
        out_specs=[rows(QKV_PAD), rows(QKV_PAD), rows(QKV_PAD), rows(SG_WIDTH), rows(SSM_WIDTH)],
        out_shape=[jax.ShapeDtypeStruct((t, QKV_PAD), BF16), jax.ShapeDtypeStruct((t, QKV_PAD), BF16),
                   jax.ShapeDtypeStruct((t, QKV_PAD), BF16), jax.ShapeDtypeStruct((t, SG_WIDTH), F32),
                   jax.ShapeDtypeStruct((t, SSM_WIDTH), BF16)],
        compiler_params=pltpu.CompilerParams(dimension_semantics=("parallel",),
                                             vmem_limit_bytes=VMEM_LIMIT),
        name="mixer_in",
    )(h, lp["ln1"], lp["win"], lp["gcq"], lp["gckv"], lp["wqa"], lp["wqb"], lp["wk"], lp["wv"],
      cos128, sin128, lp["gqk"], lp["vones"], lp["sg_lng"], lp["sg_lnb"], lp["sg_w"], lp["sg_bias"],
      lp["mmean"])


def _attn_kernel(q_ref, k_ref, v_ref, o_ref, *, tq):
    i = pl.program_id(2)
    q = q_ref[...]

    def step(j, carry, masked):
        m, acc = carry
        kb = k_ref[pl.ds(pl.multiple_of(j * tq, tq), tq), :]
        vb = v_ref[pl.ds(pl.multiple_of(j * tq, tq), tq), :]
        s = lax.dot_general(q, kb, (((1,), (1,)), ((), ())), preferred_element_type=F32)
        if masked:
            keep = (lax.broadcasted_iota(jnp.int32, (tq, tq), 1)
                    <= lax.broadcasted_iota(jnp.int32, (tq, tq), 0))
            s = jnp.where(keep, s, -jnp.inf)
        m_new = jnp.maximum(m, jnp.max(s, axis=-1, keepdims=True))
        alpha = jnp.exp(m - m_new)
        p = jnp.exp(s - m_new).astype(BF16)
        return m_new, alpha * acc + _dot(p, vb)

    init = (jnp.full((tq, 1), -jnp.inf, F32), jnp.zeros((tq, HEAD_PAD), F32))
    carry = lax.fori_loop(0, i, lambda j, c: step(j, c, False), init)
    _, acc = step(i, carry, True)
    out = acc / acc[:, V_DIM:V_DIM + 1]
    lane = lax.broadcasted_iota(jnp.int32, (tq, HEAD_PAD), 1)
    o_ref[...] = jnp.where(lane < V_DIM, out, 0.0).astype(BF16)


def _attention(q, k, v, batch, seq, tq):
    t = q.shape[0]
    nq = seq // tq
    return pl.pallas_call(
        functools.partial(_attn_kernel, tq=tq),
        grid=(batch, ATTN_HEADS, nq),
        in_specs=[pl.BlockSpec((tq, HEAD_PAD), lambda b, h, i: (b * nq + i, h)),
                  pl.BlockSpec((seq, HEAD_PAD), lambda b, h, i: (b, h)),
                  pl.BlockSpec((seq, HEAD_PAD), lambda b, h, i: (b, h))],
        out_specs=pl.BlockSpec((tq, HEAD_PAD), lambda b, h, i: (b * nq + i, h)),
        out_shape=jax.ShapeDtypeStruct((t, QKV_PAD), BF16),
        compiler_params=pltpu.CompilerParams(
            dimension_semantics=("parallel", "parallel", "arbitrary"), vmem_limit_bytes=VMEM_LIMIT),
        name="mla_attention",
    )(q, k, v)


def _ssm_kernel(x_ref, m_ref, pre_ref, pim_ref, qre_ref, qim_ref, are_ref, aim_ref, y_ref, *,
                rows_per_batch, n_steps):
    x = x_ref[0]
    cr = x.shape[0]
    y_local = _dot(x, m_ref[0])
    sre = _dot(x, pre_ref[0])
    sim = _dot(x, pim_ref[0])
    row = lax.broadcasted_iota(jnp.int32, (cr, LANES), 0) % rows_per_batch
    are = are_ref[0]
    aim = aim_ref[0]
    for k in range(n_steps):
        sh = 1 << k
        ok = row >= sh
        pr = jnp.where(ok, pltpu.roll(sre, sh, 0), 0.0)
        pi = jnp.where(ok, pltpu.roll(sim, sh, 0), 0.0)
        ar = are[k:k + 1, :]
        ai = aim[k:k + 1, :]
        sre, sim = sre + ar * pr - ai * pi, sim + ar * pi + ai * pr
    ok = row >= 1
    hre = jnp.where(ok, pltpu.roll(sre, 1, 0), 0.0).astype(BF16)
    him = jnp.where(ok, pltpu.roll(sim, 1, 0), 0.0).astype(BF16)
    y_ref[0] = y_local + _dot(hre, qre_ref[0]) + _dot(him, qim_ref[0])


def _ssm(xp, lp, rows_per_batch):
    n_pairs, cr, _ = xp.shape
    n_steps = lp["ssm_are"].shape[1]
    blk = lambda *shape: pl.BlockSpec((1,) + shape, lambda p: (p,) + (0,) * len(shape))
    return pl.pallas_call(
        functools.partial(_ssm_kernel, rows_per_batch=rows_per_batch, n_steps=n_steps),
        grid=(n_pairs,),
        in_specs=[blk(cr, SSM_PAIR), blk(SSM_PAIR, SSM_PAIR), blk(SSM_PAIR, LANES), blk(SSM_PAIR, LANES),
                  blk(LANES, SSM_PAIR), blk(LANES, SSM_PAIR), blk(n_steps, LANES), blk(n_steps, LANES)],
        out_specs=blk(cr, SSM_PAIR),
        out_shape=jax.ShapeDtypeStruct((n_pairs, cr, SSM_PAIR), F32),
        compiler_params=pltpu.CompilerParams(dimension_semantics=("parallel",),
                                             vmem_limit_bytes=VMEM_LIMIT),
        name="s5_mixer",
    )(xp, lp["ssm_m"], lp["ssm_pre"], lp["ssm_pim"], lp["ssm_qre"], lp["ssm_qim"], lp["ssm_are"],
      lp["ssm_aim"])


def _mixer_out_kernel(a_ref, sg_ref, y_ref, h_ref, gma_ref, gmg_ref, gms_ref, woa_ref, wog_ref,
                      wos_ref, wglu_ref, bglu_ref, ln2_ref, wrh_ref, wrl_ref, br_ref, ltri_ref,
                      h1_out, xn_out, ri_out, rw_out, cnt_out, run_ref):
    i = pl.program_id(0)

    @pl.when(i == 0)
    def _():
        run_ref[...] = jnp.zeros_like(run_ref)

    g = jax.nn.gelu(y_ref[...])
    glu = g * jax.nn.sigmoid(_dot(g.astype(BF16), wglu_ref[...]) + bglu_ref[...])
    a_n = _rms(a_ref[...].astype(F32), gma_ref[...], ATTN_HEADS * V_DIM)
    g_n = _rms(sg_ref[...], gmg_ref[...])
    s_n = _rms(glu, gms_ref[...])
    h1 = (h_ref[...] + _dot(a_n.astype(BF16), woa_ref[...]) + _dot(g_n.astype(BF16), wog_ref[...])
          + _dot(s_n.astype(BF16), wos_ref[...]))
    h1_out[...] = h1
    xn = _rms(h1, ln2_ref[...])
    xn_out[...] = xn

    xh, xl = _split_bf16(xn)
    logit = _dot(xh, wrh_ref[...]) + _dot(xl, wrh_ref[...]) + _dot(xh, wrl_ref[...]) + br_ref[...]
    shape = logit.shape
    lane = lax.broadcasted_iota(jnp.int32, shape, 1)
    neg = -jnp.inf
    is_grp = lane < MOE_GROUPS
    gl = jnp.where(is_grp, logit, neg)
    gmax = jnp.max(gl, axis=-1, keepdims=True)
    gsel = jnp.min(jnp.where(gl == gmax, lane, LANES), axis=-1, keepdims=True)
    gsum = jnp.sum(jnp.where(is_grp, jnp.exp(logit - gmax), 0.0), axis=-1, keepdims=True)
    g_w = 1.0 / gsum
    lo = MOE_GROUPS + gsel * EXPERTS_PER_GROUP
    in_grp = (lane >= lo) & (lane < lo + EXPERTS_PER_GROUP)
    el = jnp.where(in_grp, logit, neg)
    m1 = jnp.max(el, axis=-1, keepdims=True)
    i1 = jnp.min(jnp.where(el == m1, lane, LANES), axis=-1, keepdims=True)
    el2 = jnp.where(lane == i1, neg, el)
    m2 = jnp.max(el2, axis=-1, keepdims=True)
    i2 = jnp.min(jnp.where(el2 == m2, lane, LANES), axis=-1, keepdims=True)
    e2 = jnp.exp(m2 - m1)
    w1 = g_w / (1.0 + e2)
    w2 = g_w * e2 / (1.0 + e2)
    e_a = i1 - MOE_GROUPS
    e_b = i2 - MOE_GROUPS

    hit_a = lane == e_a
    hit_b = lane == e_b
    onehot = jnp.where(hit_a | hit_b, 1.0, 0.0).astype(BF16)
    before = _dot(ltri_ref[...], onehot) + run_ref[...]
    c_a = jnp.sum(jnp.where(hit_a, before, 0.0), axis=-1, keepdims=True).astype(jnp.int32)
    c_b = jnp.sum(jnp.where(hit_b, before, 0.0), axis=-1, keepdims=True).astype(jnp.int32)
    run_ref[...] = run_ref[...] + jnp.sum(onehot.astype(F32), axis=0, keepdims=True)
    cnt_out[...] = jnp.broadcast_to(run_ref[...], cnt_out.shape)

    ri_out[...] = jnp.where(lane == 0, e_a, jnp.where(lane == 1, e_b, jnp.where(
        lane == 2, c_a, jnp.where(lane == 3, c_b, 0))))
    rw_out[...] = jnp.where(lane == 0, w1, jnp.where(lane == 1, w2, 0.0))


def _mixer_out(a, sg, y, h, lp, tm):
    t = h.shape[0]
    const = lambda *shape: pl.BlockSpec(shape, lambda i: (0,) * len(shape))
    rows = lambda w: pl.BlockSpec((tm, w), lambda i: (i, 0))
    return pl.pallas_call(
        _mixer_out_kernel,
        grid=(t // tm,),
        in_specs=[rows(QKV_PAD), rows(SG_WIDTH), rows(SSM_WIDTH), rows(D_MODEL), const(1, QKV_PAD),
                  const(1, SG_WIDTH), const(1, SSM_WIDTH), const(QKV_PAD, D_MODEL),
                  const(SG_WIDTH, D_MODEL), const(SSM_WIDTH, D_MODEL), const(SSM_WIDTH, SSM_WIDTH),
                  const(1, SSM_WIDTH), const(1, D_MODEL), const(D_MODEL, LANES), const(D_MODEL, LANES),
                  const(1, LANES), const(tm, tm)],
        out_specs=[rows(D_MODEL), rows(D_MODEL), rows(LANES), rows(LANES), const(8, LANES)],
        out_shape=[jax.ShapeDtypeStruct((t, D_MODEL), F32), jax.ShapeDtypeStruct((t, D_MODEL), F32),
                   jax.ShapeDtypeStruct((t, LANES), jnp.int32), jax.ShapeDtypeStruct((t, LANES), F32),
                   jax.ShapeDtypeStruct((8, LANES), F32)],
        scratch_shapes=[pltpu.VMEM((1, LANES), F32)],
        compiler_params=pltpu.CompilerParams(dimension_semantics=("arbitrary",),
                                             vmem_limit_bytes=VMEM_LIMIT),
        name="mixer_out_router",
    )(a, sg, y, h, lp["gm_a"], lp["gm_g"], lp["gm_s"], lp["wo_a"], lp["wo_g"], lp["wo_s"], lp["wglu"],
      lp["bglu"], lp["ln2"], lp["wr_hi"], lp["wr_lo"], lp["br"], lp["ltri"])


def _row_copy(src_ref, src_row, dst_ref, dst_row, sem):
    return pltpu.make_async_copy(src_ref.at[pl.ds(src_row, 1)], dst_ref.at[pl.ds(dst_row, 1)], sem)


def _dispatch_kernel(pstart_ref, idx_ref, x_ref, xs_in_ref, xs_ref, sem, *, td):
    del xs_in_ref

    def issue(r, _):
        d0 = pstart_ref[idx_ref[0, 0, 4 * r]] + idx_ref[0, 0, 4 * r + 2]
        d1 = pstart_ref[idx_ref[0, 0, 4 * r + 1]] + idx_ref[0, 0, 4 * r + 3]
        _row_copy(x_ref, r, xs_ref, d0, sem).start()
        _row_copy(x_ref, r, xs_ref, d1, sem).start()
        return 0

    lax.fori_loop(0, td, issue, 0)

    def drain(r, _):
        _row_copy(x_ref, 0, xs_ref, 0, sem).wait()
        _row_copy(x_ref, 0, xs_ref, 0, sem).wait()
        return 0

    lax.fori_loop(0, td, drain, 0)


def _dispatch(pstart, idx, xn, xs_zero, td):
    t = xn.shape[0]
    grid_spec = pltpu.PrefetchScalarGridSpec(
        num_scalar_prefetch=1,
        grid=(t // td,),
        in_specs=[pl.BlockSpec((1, 1, 4 * td), lambda i, ps: (i, 0, 0), memory_space=pltpu.SMEM),
                  pl.BlockSpec((td, D_MODEL), lambda i, ps: (i, 0)),
                  pl.BlockSpec(memory_space=pl.ANY)],
        out_specs=pl.BlockSpec(memory_space=pl.ANY),
        scratch_shapes=[pltpu.SemaphoreType.DMA(())],
    )
    return pl.pallas_call(
        functools.partial(_dispatch_kernel, td=td),
        grid_spec=grid_spec,
        out_shape=jax.ShapeDtypeStruct(xs_zero.shape, xs_zero.dtype),
        input_output_aliases={3: 0},
        compiler_params=pltpu.CompilerParams(dimension_semantics=("arbitrary",)),
        name="moe_dispatch",
    )(pstart, idx, xn, xs_zero)


def _expert_kernel(blk_e_ref, n_valid_ref, xs_ref, w13_ref, w2_ref, yb_ref):
    del blk_e_ref
    i = pl.program_id(0)

    @pl.when(i < n_valid_ref[0])
    def _():
        x = xs_ref[...].astype(BF16)
        hh = _dot(x, w13_ref[...])
        hb = jax.nn.silu(hh[:, :D_EXPERT]) * hh[:, D_EXPERT:]
        yb_ref[...] = _dot(hb.astype(BF16), w2_ref[...])

    @pl.when(i >= n_valid_ref[0])
    def _():
        yb_ref[...] = jnp.zeros_like(yb_ref)


def _experts(blk_e, n_valid, xs, w13, w2):
    n_rows = xs.shape[0]
    grid_spec = pltpu.PrefetchScalarGridSpec(
        num_scalar_prefetch=2,
        grid=(n_rows // EXPERT_BLOCK,),
        in_specs=[pl.BlockSpec((EXPERT_BLOCK, D_MODEL), lambda i, be, nv: (i, 0)),
                  pl.BlockSpec((None, D_MODEL, 2 * D_EXPERT), lambda i, be, nv: (be[i], 0, 0)),
                  pl.BlockSpec((None, D_EXPERT, D_MODEL), lambda i, be, nv: (be[i], 0, 0))],
        out_specs=pl.BlockSpec((EXPERT_BLOCK, D_MODEL), lambda i, be, nv: (i, 0)),
    )
    return pl.pallas_call(
        _expert_kernel,
        grid_spec=grid_spec,
        out_shape=jax.ShapeDtypeStruct((n_rows, D_MODEL), F32),
        compiler_params=pltpu.CompilerParams(dimension_semantics=("arbitrary",),
                                             vmem_limit_bytes=VMEM_LIMIT),
        name="moe_experts",
    )(blk_e, n_valid, xs, w13, w2)


def _combine_kernel(pstart_ref, idx_ref, h_ref, rw_ref, p_ref, wpg_ref, bpg_ref, wple_ref, yb_ref,
                    out_ref, y0_ref, y1_ref, sem, *, tc):
    def issue(r, _):
        d0 = pstart_ref[idx_ref[0, 0, 4 * r]] + idx_ref[0, 0, 4 * r + 2]
        d1 = pstart_ref[idx_ref[0, 0, 4 * r + 1]] + idx_ref[0, 0, 4 * r + 3]
        _row_copy(yb_ref, d0, y0_ref, r, sem).start()
        _row_copy(yb_ref, d1, y1_ref, r, sem).start()
        return 0

    lax.fori_loop(0, tc, issue, 0)

    def drain(r, _):
        _row_copy(yb_ref, 0, y0_ref, 0, sem).wait()
        _row_copy(yb_ref, 0, y1_ref, 0, sem).wait()
        return 0

    lax.fori_loop(0, tc, drain, 0)

    rw = rw_ref[...]
    h2 = h_ref[...] + rw[:, 0:1] * y0_ref[...] + rw[:, 1:2] * y1_ref[...]
    gate = jax.nn.sigmoid(_dot(h2.astype(BF16), wpg_ref[...]) + bpg_ref[...])
    out_ref[...] = h2 + gate * _dot(p_ref[...].astype(BF16), wple_ref[...])


def _combine(pstart, idx, h1, rw, p, lp, yb, tc):
    t = h1.shape[0]
    const = lambda *shape: pl.BlockSpec(shape, lambda i, ps: (0,) * len(shape))
    rows = lambda w: pl.BlockSpec((tc, w), lambda i, ps: (i, 0))
    grid_spec = pltpu.PrefetchScalarGridSpec(
        num_scalar_prefetch=1,
        grid=(t // tc,),
        in_specs=[pl.BlockSpec((1, 1, 4 * tc), lambda i, ps: (i, 0, 0), memory_space=pltpu.SMEM),
                  rows(D_MODEL), rows(LANES), rows(PLE_DIM), const(D_MODEL, D_MODEL), const(1, D_MODEL),
                  const(PLE_DIM, D_MODEL), pl.BlockSpec(memory_space=pl.ANY)],
        out_specs=rows(D_MODEL),
        scratch_shapes=[pltpu.VMEM((tc, D_MODEL), F32), pltpu.VMEM((tc, D_MODEL), F32),
                        pltpu.SemaphoreType.DMA(())],
    )
    return pl.pallas_call(
        functools.partial(_combine_kernel, tc=tc),
        grid_spec=grid_spec,
        out_shape=jax.ShapeDtypeStruct((t, D_MODEL), F32),
        compiler_params=pltpu.CompilerParams(dimension_semantics=("arbitrary",),
                                             vmem_limit_bytes=VMEM_LIMIT),
        name="moe_combine_ple",
    )(pstart, idx, h1, rw, p, lp["wpg"], lp["bpg"], lp["wple"], yb)


def _rope_lane_tables(seq):
    inv = 1.0 / (ROPE_THETA ** (jnp.arange(0, ROPE_DIM, 2, dtype=F32) / ROPE_DIM))
    ang = jnp.arange(seq, dtype=F32)[:, None] * inv[None, :]
    cos, sin = jnp.cos(ang), jnp.sin(ang)
    half = ROPE_DIM // 2
    cos128 = jnp.concatenate([jnp.ones((seq, NOPE_DIM), F32), cos, cos,
                              jnp.zeros((seq, LANES - QK_DIM), F32)], axis=1)
    sin128 = jnp.concatenate([jnp.zeros((seq, NOPE_DIM), F32), -sin, sin,
                              jnp.zeros((seq, LANES - QK_DIM), F32)], axis=1)
    del half
    return cos128, sin128


def _swap_rope_halves(w):
    half = ROPE_DIM // 2
    return jnp.concatenate([w[..., half:], w[..., :half]], axis=-1)


def _pad_heads(w, width):
    return jnp.pad(w, ((0, 0), (0, 0), (0, HEAD_PAD - width))).reshape(w.shape[0], QKV_PAD)


def _rope_gain_rows(g):
    pad = jnp.zeros((LANES - QK_DIM,), F32)
    ga = jnp.concatenate([g, pad])
    gb = jnp.concatenate([jnp.zeros((NOPE_DIM,), F32), _swap_rope_halves(g[NOPE_DIM:]), pad])
    return ga, gb


def _ssm_tables(lam_re, lam_im, log_dt, b_re, b_im, c_re, c_im, d_skip, n_steps):
    hp = lax.Precision.HIGHEST
    lr = jnp.minimum(lam_re, -1e-4)
    li = lam_im
    dt = jnp.exp(log_dt)
    mag = jnp.exp(lr * dt)
    ab_re, ab_im = mag * jnp.cos(li * dt), mag * jnp.sin(li * dt)
    den = lr * lr + li * li
    nr, ni = ab_re - 1.0, ab_im
    fr = (nr * lr + ni * li) / den
    fi = (ni * lr - nr * li) / den
    bb_re = fr[..., None] * b_re - fi[..., None] * b_im
    bb_im = fr[..., None] * b_im + fi[..., None] * b_re

    steps = jnp.arange(SSM_CHUNK + 1, dtype=F32)[:, None, None]
    pmag = jnp.exp(lr * dt * steps)
    pw_re, pw_im = pmag * jnp.cos(li * dt * steps), pmag * jnp.sin(li * dt * steps)

    w_re = pw_re[:SSM_CHUNK, :, :, None] * bb_re - pw_im[:SSM_CHUNK, :, :, None] * bb_im
    w_im = pw_re[:SSM_CHUNK, :, :, None] * bb_im + pw_im[:SSM_CHUNK, :, :, None] * bb_re
    taps = (jnp.einsum("gcn,tgnd->tgcd", c_re, w_re, precision=hp)
            - jnp.einsum("gcn,tgnd->tgcd", c_im, w_im, precision=hp))
    taps = taps.at[0].add(d_skip[:, :, None] * jnp.eye(SSM_CH, dtype=F32)[None])
    t_in = jnp.arange(SSM_CHUNK)[:, None]
    t_out = jnp.arange(SSM_CHUNK)[None, :]
    lag = t_out - t_in
    m = jnp.where((lag >= 0)[:, :, None, None, None], taps[jnp.clip(lag, 0)], 0.0)
    m = m.transpose(2, 0, 4, 1, 3).reshape(SSM_GROUPS, SSM_CHUNK * SSM_CH, SSM_CHUNK * SSM_CH)

    rev = w_re[::-1], w_im[::-1]
    p_re = rev[0].transpose(1, 0, 3, 2).reshape(SSM_GROUPS, SSM_CHUNK * SSM_CH, SSM_STATE)
    p_im = rev[1].transpose(1, 0, 3, 2).reshape(SSM_GROUPS, SSM_CHUNK * SSM_CH, SSM_STATE)

    up_re, up_im = pw_re[1:], pw_im[1:]
    q_re = (c_re[None] * up_re[:, :, None, :] - c_im[None] * up_im[:, :, None, :])
    q_im = -(c_re[None] * up_im[:, :, None, :] + c_im[None] * up_re[:, :, None, :])
    q_re = q_re.transpose(1, 3, 0, 2).reshape(SSM_GROUPS, SSM_STATE, SSM_CHUNK * SSM_CH)
    q_im = q_im.transpose(1, 3, 0, 2).reshape(SSM_GROUPS, SSM_STATE, SSM_CHUNK * SSM_CH)

    are, aim = [pw_re[SSM_CHUNK]], [pw_im[SSM_CHUNK]]
    for _ in range(n_steps - 1):
        r, im = are[-1], aim[-1]
        are.append(r * r - im * im)
        aim.append(2.0 * r * im)
    are, aim = jnp.stack(are, axis=1), jnp.stack(aim, axis=1)

    np_ = SSM_GROUPS // 2

    def blockdiag(x):
        g, r, c = x.shape
        x = x.reshape(np_, 2, r, c)
        z = jnp.zeros((np_, r, c), x.dtype)
        top = jnp.concatenate([x[:, 0], z], axis=2)
        bot = jnp.concatenate([z, x[:, 1]], axis=2)
        return jnp.concatenate([top, bot], axis=1)

    def lanes(x):
        g, k, n = x.shape
        return x.reshape(np_, 2, k, n).transpose(0, 2, 1, 3).reshape(np_, k, 2 * n)

    return dict(ssm_m=blockdiag(m).astype(BF16), ssm_pre=blockdiag(p_re).astype(BF16),
                ssm_pim=blockdiag(p_im).astype(BF16), ssm_qre=blockdiag(q_re).astype(BF16),
                ssm_qim=blockdiag(q_im).astype(BF16), ssm_are=lanes(are), ssm_aim=lanes(aim))


def _layer_params(i, tm, n_scan_steps, ln1, w_in, g_cq, g_ckv, w_uq, w_ukv, g_qh, g_kh, sg_ln_g,
                  sg_ln_b, sg_w, sg_b, lam_re, lam_im, log_dt, ssm_b_re, ssm_b_im, ssm_c_re, ssm_c_im,
                  ssm_d, w_glu, b_glu, g_mix, w_out, ln2, w_rg, b_rg, w_rexp, b_rexp, w1, w3, w2, w_pg,
                  b_pg, w_ple):
    lp = {}
    wi = w_in[i]
    off_kr = Q_RANK + KV_RANK
    off_sg = off_kr + ROPE_DIM
    kr = wi[:, off_kr:off_sg]
    zpad = lambda n: jnp.zeros((D_MODEL, n), F32)
    kra = jnp.concatenate([zpad(NOPE_DIM), kr, zpad(LANES - QK_DIM)], axis=1)
    krb = jnp.concatenate([zpad(NOPE_DIM), _swap_rope_halves(kr), zpad(LANES - QK_DIM)], axis=1)
    lp["win"] = jnp.concatenate([wi[:, :off_kr], kra, krb, wi[:, off_sg:]], axis=1).astype(BF16)
    lp["ln1"] = ln1[i][None]
    lp["gcq"] = g_cq[i][None]
    lp["gckv"] = g_ckv[i][None]

    wq = w_uq[i].reshape(Q_RANK, ATTN_HEADS, QK_DIM)
    lp["wqa"] = _pad_heads(wq, QK_DIM).astype(BF16)
    wq_sw = jnp.concatenate([jnp.zeros((Q_RANK, ATTN_HEADS, NOPE_DIM), F32),
                             _swap_rope_halves(wq[..., NOPE_DIM:])], axis=-1)
    lp["wqb"] = _pad_heads(wq_sw, QK_DIM).astype(BF16)
    wkv = w_ukv[i].reshape(KV_RANK, ATTN_HEADS, NOPE_DIM + V_DIM)
    lp["wk"] = _pad_heads(wkv[..., :NOPE_DIM], NOPE_DIM).astype(BF16)
    lp["wv"] = _pad_heads(wkv[..., NOPE_DIM:], V_DIM).astype(BF16)
    gq_a, gq_b = _rope_gain_rows(g_qh[i])
    gk_a, gk_b = _rope_gain_rows(g_kh[i])
    lp["gqk"] = jnp.concatenate([jnp.stack([gq_a, gq_b, gk_a, gk_b]), jnp.zeros((4, LANES), F32)])
    lp["vones"] = jnp.tile((jnp.arange(HEAD_PAD) == V_DIM).astype(F32), ATTN_HEADS)[None]

    lp["sg_lng"] = sg_ln_g[i][None]
    lp["sg_lnb"] = sg_ln_b[i][None]
    lp["sg_w"] = sg_w[i]
    lp["sg_bias"] = jnp.repeat(sg_b[i].T, SG_CH, axis=1)
    grp = jnp.arange(SG_WIDTH) // SG_CH
    lp["mmean"] = ((grp[:, None] == grp[None, :]).astype(F32) / SG_CH).astype(BF16)

    lp.update(_ssm_tables(lam_re[i], lam_im[i], log_dt[i], ssm_b_re[i], ssm_b_im[i], ssm_c_re[i],
                          ssm_c_im[i], ssm_d[i], n_scan_steps))

    gm = g_mix[i]
    aw = ATTN_HEADS * V_DIM
    lp["gm_a"] = _pad_heads(gm[:aw].reshape(1, ATTN_HEADS, V_DIM), V_DIM)
    lp["gm_g"] = gm[aw:aw + SG_WIDTH][None]
    lp["gm_s"] = gm[aw + SG_WIDTH:][None]
    wo = w_out[i]
    woa = wo[:aw].reshape(ATTN_HEADS, V_DIM, D_MODEL)
    lp["wo_a"] = jnp.pad(woa, ((0, 0), (0, HEAD_PAD - V_DIM), (0, 0))).reshape(QKV_PAD, D_MODEL).astype(BF16)
    lp["wo_g"] = wo[aw:aw + SG_WIDTH].astype(BF16)
    lp["wo_s"] = wo[aw + SG_WIDTH:].astype(BF16)
    lp["wglu"] = w_glu[i].astype(BF16)
    lp["bglu"] = b_glu[i][None]
    lp["ln2"] = ln2[i][None]
    wr = jnp.concatenate([w_rg[i], w_rexp[i], jnp.zeros((D_MODEL, LANES - MOE_GROUPS - N_EXPERTS), F32)],
                         axis=1)
    lp["wr_hi"], lp["wr_lo"] = _split_bf16(wr)
    lp["br"] = jnp.concatenate([b_rg[i], b_rexp[i], jnp.zeros((LANES - MOE_GROUPS - N_EXPERTS,), F32)])[None]
    lp["ltri"] = (jnp.arange(tm)[None, :] < jnp.arange(tm)[:, None]).astype(BF16)

    lp["w13"] = jnp.concatenate([w1[i], w3[i]], axis=-1).astype(BF16)
    lp["w2"] = w2[i].astype(BF16)
    lp["wpg"] = w_pg[i].astype(BF16)
    lp["bpg"] = b_pg[i][None]
    lp["wple"] = w_ple[i].astype(BF16)
    return lp


def kernel(x, p, ln1, w_in, g_cq, g_ckv, w_uq, w_ukv, g_qh, g_kh, sg_ln_g, sg_ln_b, sg_w, sg_b, lam_re, lam_im, log_dt, ssm_b_re, ssm_b_im, ssm_c_re, ssm_c_im, ssm_d, w_glu, b_glu, g_mix, w_out, ln2, w_rg, b_rg, w_rexp, b_rexp, w1, w3, w2, w_pg, b_pg, w_ple):
    batch, seq, _ = x.shape
    depth = p.shape[0]
    t = batch * seq
    tm = min(512, seq)
    tq = min(512, seq)
    td = min(512, seq)
    tc = min(256, seq)
    assert seq % tm == 0 and seq % SSM_CHUNK == 0 and tm % SG_CHUNK == 0
    rows_per_batch = seq // SSM_CHUNK
    n_scan_steps = max(1, (rows_per_batch - 1).bit_length())
    cr = t // SSM_CHUNK
    n_pairs = SSM_GROUPS // 2
    tk2 = 2 * t
    n_blocks = -(-(tk2 + N_EXPERTS * (EXPERT_BLOCK - 1)) // EXPERT_BLOCK)
    n_rows = n_blocks * EXPERT_BLOCK

    cos128, sin128 = _rope_lane_tables(seq)
    h = x.reshape(t, D_MODEL)
    weights = (ln1, w_in, g_cq, g_ckv, w_uq, w_ukv, g_qh, g_kh, sg_ln_g, sg_ln_b, sg_w, sg_b, lam_re,
               lam_im, log_dt, ssm_b_re, ssm_b_im, ssm_c_re, ssm_c_im, ssm_d, w_glu, b_glu, g_mix, w_out,
               ln2, w_rg, b_rg, w_rexp, b_rexp, w1, w3, w2, w_pg, b_pg, w_ple)
    for i in range(depth):
        lp = _layer_params(i, tm, n_scan_steps, *weights)
        q, k, v, sg, xs_ssm = _mixer_in(h, lp, cos128, sin128, tm, seq // tm)
        a = _attention(q, k, v, batch, seq, tq)

        xp = xs_ssm.reshape(cr, SSM_CHUNK, n_pairs, 2, SSM_CH).transpose(2, 0, 3, 1, 4)
        yp = _ssm(xp.reshape(n_pairs, cr, SSM_PAIR), lp, rows_per_batch)
        y = yp.reshape(n_pairs, cr, 2, SSM_CHUNK, SSM_CH).transpose(1, 3, 0, 2, 4).reshape(t, SSM_WIDTH)

        h1, xn, ri, rw, cnt = _mixer_out(a, sg, y, h, lp, tm)

        counts = cnt[0, :N_EXPERTS].astype(jnp.int32)
        pcounts = (counts + EXPERT_BLOCK - 1) // EXPERT_BLOCK * EXPERT_BLOCK
        pend = jnp.cumsum(pcounts)
        pstart = (pend - pcounts).astype(jnp.int32)
        n_valid = (pend[-1:] // EXPERT_BLOCK).astype(jnp.int32)
        blk_e = jnp.minimum(jnp.searchsorted(pend, jnp.arange(n_blocks) * EXPERT_BLOCK, side="right"),
                            N_EXPERTS - 1).astype(jnp.int32)
        idx = ri[:, :4]
        xs = _dispatch(pstart, idx.reshape(t // td, 1, 4 * td), xn, jnp.zeros((n_rows, D_MODEL), F32), td)
        yb = _experts(blk_e, n_valid, xs, lp["w13"], lp["w2"])
        h = _combine(pstart, idx.reshape(t // tc, 1, 4 * tc), h1, rw, p[i].reshape(t, PLE_DIM), lp, yb, tc)
    return h.reshape(batch, seq, D_MODEL)
```

```python
import functools
import math

import jax
import jax.numpy as jnp
from jax import lax
from jax.experimental import pallas as pl
from jax.experimental.pallas import tpu as pltpu

F32 = jnp.float32
BF16 = jnp.bfloat16

D_MODEL = 1024
ATTN_HEADS = 8
NOPE_DIM = 64
ROPE_DIM = 32
QK_DIM = NOPE_DIM + ROPE_DIM
V_DIM = 64
Q_RANK = 384
KV_RANK = 256
SG_GROUPS = 4
SG_CH = 64
SG_WIDTH = SG_GROUPS * SG_CH
SG_CHUNK = 128
SSM_GROUPS = 16
SSM_CH = 16
SSM_WIDTH = SSM_GROUPS * SSM_CH
SSM_STATE = 64
MOE_GROUPS = 4
EXPERTS_PER_GROUP = 8
N_EXPERTS = MOE_GROUPS * EXPERTS_PER_GROUP
D_EXPERT = 256
PLE_DIM = 256
ROPE_THETA = 10000.0
EPS = 1e-6

LANES = 128
HEAD_PAD = LANES
QKV_PAD = ATTN_HEADS * HEAD_PAD
ATTN_HEADS_PER_STEP = 2
SSM_CHUNK = 16
SSM_BUNDLE = 4
SSM_COLS = SSM_BUNDLE * SSM_CHUNK * SSM_CH
SSM_SLANES = SSM_BUNDLE * SSM_STATE
EXPERT_BLOCK = 256
DMA_ISSUE_UNROLL = 8
VMEM_LIMIT = 56 * 1024 * 1024

C_Q = 0
C_KV = C_Q + Q_RANK
C_KRA = C_KV + KV_RANK
C_KRB = C_KRA + LANES
C_SG = C_KRB + LANES
C_SSM = C_SG + 2 * SG_WIDTH
C_END = C_SSM + SSM_WIDTH


def _dot(a, b):
    return jnp.dot(a, b, preferred_element_type=F32)


def _split_bf16(x):
    hi = x.astype(BF16)
    lo = (x - hi.astype(F32)).astype(BF16)
    return hi, lo


def _rms(x, g, width=None):
    width = x.shape[-1] if width is None else width
    ms = jnp.sum(x * x, axis=-1, keepdims=True) * (1.0 / width)
    return x * lax.rsqrt(ms + EPS) * g


def _mixer_in_kernel(h_ref, ln1_ref, win_ref, gcq_ref, gckv_ref, wqa_ref, wqb_ref, wk_ref, wv_ref,
                     cos_ref, sin_ref, gqk_ref, vones_ref, lng_ref, lnb_ref, sgw_ref, sgb_ref,
                     mmean_ref, q_out, k_out, v_out, sg_out, ssm_out, *, tm):
    xn = _rms(h_ref[...], ln1_ref[...])
    z = _dot(xn.astype(BF16), win_ref[...])

    cos = cos_ref[...]
    sin = sin_ref[...]
    gqk = gqk_ref[...]
    gqa = cos * gqk[0:1, :]
    gqb = sin * gqk[1:2, :]
    gka = cos * gqk[2:3, :]
    gkb = sin * gqk[3:4, :]
    scale = QK_DIM ** -0.5 * math.log2(math.e)

    qn = _rms(z[:, C_Q:C_KV], gcq_ref[...]).astype(BF16)
    qa = _dot(qn, wqa_ref[...])
    qb = _dot(qn, wqb_ref[...])
    for hd in range(ATTN_HEADS):
        sl = slice(hd * HEAD_PAD, (hd + 1) * HEAD_PAD)
        a = qa[:, sl]
        r = lax.rsqrt(jnp.sum(a * a, axis=-1, keepdims=True) * (1.0 / QK_DIM) + EPS)
        q_out[:, sl] = ((a * gqa + qb[:, sl] * gqb) * (r * scale)).astype(BF16)

    kvn = _rms(z[:, C_KV:C_KRA], gckv_ref[...]).astype(BF16)
    ka = _dot(kvn, wk_ref[...])
    v_out[...] = (_dot(kvn, wv_ref[...]) + vones_ref[...]).astype(BF16)
    kra = z[:, C_KRA:C_KRB]
    krb = z[:, C_KRB:C_SG]
    ss_rope = jnp.sum(kra * kra, axis=-1, keepdims=True)
    krot = kra * gka + krb * gkb
    for hd in range(ATTN_HEADS):
        sl = slice(hd * HEAD_PAD, (hd + 1) * HEAD_PAD)
        a = ka[:, sl]
        r = lax.rsqrt((jnp.sum(a * a, axis=-1, keepdims=True) + ss_rope) * (1.0 / QK_DIM) + EPS)
        k_out[:, sl] = ((a * gka + krot) * r).astype(BF16)

    zg = jax.nn.gelu(z[:, C_SG:C_SSM])
    u = zg[:, :SG_WIDTH]
    vv = zg[:, SG_WIDTH:]
    mmean = mmean_ref[...]
    vh, vl = _split_bf16(vv)
    mu = _dot(vh, mmean) + _dot(vl, mmean)
    xc = vv - mu
    sh, sl_ = _split_bf16(xc * xc)
    var = _dot(sh, mmean) + _dot(sl_, mmean)
    vn = (xc * lax.rsqrt(var + EPS) * lng_ref[...] + lnb_ref[...]).astype(BF16)
    tri = (lax.broadcasted_iota(jnp.int32, (SG_CHUNK, SG_CHUNK), 1)
           <= lax.broadcasted_iota(jnp.int32, (SG_CHUNK, SG_CHUNK), 0))
    lane_grp = lax.broadcasted_iota(jnp.int32, (SG_CHUNK, SG_WIDTH), 1) // SG_CH
    ws = [jnp.where(tri, sgw_ref[g], 0.0).astype(BF16) for g in range(SG_GROUPS)]
    for c in range(tm // SG_CHUNK):
        rows = slice(c * SG_CHUNK, (c + 1) * SG_CHUNK)
        vc = vn[rows, :]
        mixed = sgb_ref[...]
        for g in range(SG_GROUPS):
            mixed = mixed + jnp.where(lane_grp == g, _dot(ws[g], vc), 0.0)
        sg_out[rows, :] = u[rows, :] * mixed

    ssm_out[...] = z[:, C_SSM:C_END].astype(BF16)


def _mixer_in(h, lp, cos128, sin128, tm, n_pos_tiles):
    t = h.shape[0]
    const = lambda *shape: pl.BlockSpec(shape, lambda i: (0,) * len(shape))
    rows = lambda w: pl.BlockSpec((tm, w), lambda i: (i, 0))
    pos = pl.BlockSpec((tm, LANES), lambda i: (i % n_pos_tiles, 0))
    return pl.pallas_call(
        functools.partial(_mixer_in_kernel, tm=tm),
        grid=(t // tm,),
        in_specs=[rows(D_MODEL), const(1, D_MODEL), const(D_MODEL, C_END), const(1, Q_RANK),
                  const(1, KV_RANK), const(Q_RANK, QKV_PAD), const(Q_RANK, QKV_PAD),
                  const(KV_RANK, QKV_PAD), const(KV_RANK, QKV_PAD), pos, pos, const(8, LANES),
                  const(1, QKV_PAD), const(1, SG_WIDTH), const(1, SG_WIDTH),
                  const(SG_GROUPS, SG_CHUNK, SG_CHUNK), const(SG_CHUNK, SG_WIDTH),
                  const(SG_WIDTH, SG_WIDTH)],
        out_specs=[rows(QKV_PAD), rows(QKV_PAD), rows(QKV_PAD), rows(SG_WIDTH), rows(SSM_WIDTH)],
        out_shape=[jax.ShapeDtypeStruct((t, QKV_PAD), BF16), jax.ShapeDtypeStruct((t, QKV_PAD), BF16),
                   jax.ShapeDtypeStruct((t, QKV_PAD), BF16), jax.ShapeDtypeStruct((t, SG_WIDTH), F32),
                   jax.ShapeDtypeStruct((t, SSM_WIDTH), BF16)],
        compiler_params=pltpu.CompilerParams(dimension_semantics=("parallel",),
                                             vmem_limit_bytes=VMEM_LIMIT),
        name="mixer_in",
    )(h, lp["ln1"], lp["win"], lp["gcq"], lp["gckv"], lp["wqa"], lp["wqb"], lp["wk"], lp["wv"],
      cos128, sin128, lp["gqk"], lp["vones"], lp["sg_lng"], lp["sg_lnb"], lp["sg_w"], lp["sg_bias"],
      lp["mmean"])


def _attn_kernel(q_ref, k_ref, v_ref, o_ref, s_even, s_odd, m_ref, acc_ref, *, tq):
    i = pl.program_id(2)
    heads = [slice(hh * HEAD_PAD, (hh + 1) * HEAD_PAD) for hh in range(ATTN_HEADS_PER_STEP)]

    def block_rows(j):
        return pl.ds(pl.multiple_of(j * tq, tq), tq)

    def scores(j, s_ref):
        for hh, sl in enumerate(heads):
            s_ref[hh] = lax.dot_general(q_ref[:, sl], k_ref[block_rows(j), sl],
                                        (((1,), (1,)), ((), ())), preferred_element_type=F32)

    def accumulate(j, s_ref, masked):
        for hh, sl in enumerate(heads):
            s = s_ref[hh]
            if masked:
                keep = (lax.broadcasted_iota(jnp.int32, (tq, tq), 1)
                        <= lax.broadcasted_iota(jnp.int32, (tq, tq), 0))
                s = jnp.where(keep, s, -jnp.inf)
            m = m_ref[hh]
            m_new = jnp.maximum(m, jnp.max(s, axis=-1, keepdims=True))
            alpha = jnp.exp2(m - m_new)
            p = jnp.exp2(s - m_new).astype(BF16)
            m_ref[hh] = m_new
            acc_ref[hh] = alpha * acc_ref[hh] + _dot(p, v_ref[block_rows(j), sl])

    m_ref[...] = jnp.full(m_ref.shape, -jnp.inf, F32)
    acc_ref[...] = jnp.zeros(acc_ref.shape, F32)
    scores(0, s_even)

    def pair(t, _):
        j = 2 * t
        scores(j + 1, s_odd)
        accumulate(j, s_even, False)
        scores(j + 2, s_even)
        accumulate(j + 1, s_odd, False)
        return 0

    lax.fori_loop(0, i // 2, pair, 0)

    @pl.when(i % 2 == 1)
    def _():
        scores(i, s_odd)
        accumulate(i - 1, s_even, False)
        accumulate(i, s_odd, True)

    @pl.when(i % 2 == 0)
    def _():
        accumulate(i, s_even, True)

    lane = lax.broadcasted_iota(jnp.int32, (tq, HEAD_PAD), 1)
    for hh, sl in enumerate(heads):
        acc = acc_ref[hh]
        out = acc / acc[:, V_DIM:V_DIM + 1]
        o_ref[:, sl] = jnp.where(lane < V_DIM, out, 0.0).astype(BF16)


def _attention(q, k, v, batch, seq, tq):
    t = q.shape[0]
    nq = seq // tq
    width = ATTN_HEADS_PER_STEP * HEAD_PAD
    return pl.pallas_call(
        functools.partial(_attn_kernel, tq=tq),
        grid=(batch, ATTN_HEADS // ATTN_HEADS_PER_STEP, nq),
        in_specs=[pl.BlockSpec((tq, width), lambda b, h, i: (b * nq + i, h)),
                  pl.BlockSpec((seq, width), lambda b, h, i: (b, h)),
                  pl.BlockSpec((seq, width), lambda b, h, i: (b, h))],
        out_specs=pl.BlockSpec((tq, width), lambda b, h, i: (b * nq + i, h)),
        out_shape=jax.ShapeDtypeStruct((t, QKV_PAD), BF16),
        scratch_shapes=[pltpu.VMEM((ATTN_HEADS_PER_STEP, tq, tq), F32),
                        pltpu.VMEM((ATTN_HEADS_PER_STEP, tq, tq), F32),
                        pltpu.VMEM((ATTN_HEADS_PER_STEP, tq, 1), F32),
                        pltpu.VMEM((ATTN_HEADS_PER_STEP, tq, HEAD_PAD), F32)],
        compiler_params=pltpu.CompilerParams(
            dimension_semantics=("parallel", "parallel", "arbitrary"), vmem_limit_bytes=VMEM_LIMIT),
        name="mla_attention",
    )(q, k, v)


def _ssm_kernel(x_ref, m_ref, pre_ref, pim_ref, qre_ref, qim_ref, are_ref, aim_ref, y_ref, *,
                rows_per_batch, n_steps):
    x = x_ref[0]
    cr = x.shape[0]
    y_local = _dot(x, m_ref[0])
    sre = _dot(x, pre_ref[0])
    sim = _dot(x, pim_ref[0])
    row = lax.broadcasted_iota(jnp.int32, (cr, SSM_SLANES), 0) % rows_per_batch
    are = are_ref[0]
    aim = aim_ref[0]
    for k in range(n_steps):
        sh = 1 << k
        ok = row >= sh
        pr = jnp.where(ok, pltpu.roll(sre, sh, 0), 0.0)
        pi = jnp.where(ok, pltpu.roll(sim, sh, 0), 0.0)
        ar = are[k:k + 1, :]
        ai = aim[k:k + 1, :]
        sre, sim = sre + ar * pr - ai * pi, sim + ar * pi + ai * pr
    ok = row >= 1
    hre = jnp.where(ok, pltpu.roll(sre, 1, 0), 0.0).astype(BF16)
    him = jnp.where(ok, pltpu.roll(sim, 1, 0), 0.0).astype(BF16)
    y_ref[0] = y_local + _dot(hre, qre_ref[0]) + _dot(him, qim_ref[0])


def _ssm(xp, lp, rows_per_batch):
    n_bundles, cr, _ = xp.shape
    n_steps = lp["ssm_are"].shape[1]
    blk = lambda *shape: pl.BlockSpec((1,) + shape, lambda p: (p,) + (0,) * len(shape))
    return pl.pallas_call(
        functools.partial(_ssm_kernel, rows_per_batch=rows_per_batch, n_steps=n_steps),
        grid=(n_bundles,),
        in_specs=[blk(cr, SSM_COLS), blk(SSM_COLS, SSM_COLS), blk(SSM_COLS, SSM_SLANES),
                  blk(SSM_COLS, SSM_SLANES), blk(SSM_SLANES, SSM_COLS), blk(SSM_SLANES, SSM_COLS),
                  blk(n_steps, SSM_SLANES), blk(n_steps, SSM_SLANES)],
        out_specs=blk(cr, SSM_COLS),
        out_shape=jax.ShapeDtypeStruct((n_bundles, cr, SSM_COLS), F32),
        compiler_params=pltpu.CompilerParams(dimension_semantics=("parallel",),
                                             vmem_limit_bytes=VMEM_LIMIT),
        name="s5_mixer",
    )(xp, lp["ssm_m"], lp["ssm_pre"], lp["ssm_pim"], lp["ssm_qre"], lp["ssm_qim"], lp["ssm_are"],
      lp["ssm_aim"])


def _mixer_out_kernel(a_ref, sg_ref, y_ref, h_ref, gma_ref, gmg_ref, gms_ref, woa_ref, wog_ref,
                      wos_ref, wglu_ref, bglu_ref, ln2_ref, wrh_ref, wrl_ref, br_ref, ltri_ref,
                      h1_out, xn_out, ri_out, rw_out, cnt_out, run_ref):
    i = pl.program_id(0)

    @pl.when(i == 0)
    def _():
        run_ref[...] = jnp.zeros_like(run_ref)

    g = jax.nn.gelu(y_ref[...])
    glu = g * jax.nn.sigmoid(_dot(g.astype(BF16), wglu_ref[...]) + bglu_ref[...])
    a_n = _rms(a_ref[...].astype(F32), gma_ref[...], ATTN_HEADS * V_DIM)
    g_n = _rms(sg_ref[...], gmg_ref[...])
    s_n = _rms(glu, gms_ref[...])
    h1 = (h_ref[...] + _dot(a_n.astype(BF16), woa_ref[...]) + _dot(g_n.astype(BF16), wog_ref[...])
          + _dot(s_n.astype(BF16), wos_ref[...]))
    h1_out[...] = h1
    xn = _rms(h1, ln2_ref[...])
    xn_out[...] = xn

    xh, xl = _split_bf16(xn)
    logit = _dot(xh, wrh_ref[...]) + _dot(xl, wrh_ref[...]) + _dot(xh, wrl_ref[...]) + br_ref[...]
    shape = logit.shape
    lane = lax.broadcasted_iota(jnp.int32, shape, 1)
    neg = -jnp.inf
    is_grp = lane < MOE_GROUPS
    gl = jnp.where(is_grp, logit, neg)
    gmax = jnp.max(gl, axis=-1, keepdims=True)
    gsel = jnp.min(jnp.where(gl == gmax, lane, LANES), axis=-1, keepdims=True)
    gsum = jnp.sum(jnp.where(is_grp, jnp.exp(logit - gmax), 0.0), axis=-1, keepdims=True)
    g_w = 1.0 / gsum
    lo = MOE_GROUPS + gsel * EXPERTS_PER_GROUP
    in_grp = (lane >= lo) & (lane < lo + EXPERTS_PER_GROUP)
    el = jnp.where(in_grp, logit, neg)
    m1 = jnp.max(el, axis=-1, keepdims=True)
    i1 = jnp.min(jnp.where(el == m1, lane, LANES), axis=-1, keepdims=True)
    el2 = jnp.where(lane == i1, neg, el)
    m2 = jnp.max(el2, axis=-1, keepdims=True)
    i2 = jnp.min(jnp.where(el2 == m2, lane, LANES), axis=-1, keepdims=True)
    e2 = jnp.exp(m2 - m1)
    w1 = g_w / (1.0 + e2)
    w2 = g_w * e2 / (1.0 + e2)
    e_a = i1 - MOE_GROUPS
    e_b = i2 - MOE_GROUPS

    hit_a = lane == e_a
    hit_b = lane == e_b
    onehot = jnp.where(hit_a | hit_b, 1.0, 0.0).astype(BF16)
    before = _dot(ltri_ref[...], onehot) + run_ref[...]
    c_a = jnp.sum(jnp.where(hit_a, before, 0.0), axis=-1, keepdims=True).astype(jnp.int32)
    c_b = jnp.sum(jnp.where(hit_b, before, 0.0), axis=-1, keepdims=True).astype(jnp.int32)
    run_ref[...] = run_ref[...] + jnp.sum(onehot.astype(F32), axis=0, keepdims=True)
    cnt_out[...] = jnp.broadcast_to(run_ref[...], cnt_out.shape)

    ri_out[...] = jnp.where(lane == 0, e_a, jnp.where(lane == 1, e_b, jnp.where(
        lane == 2, c_a, jnp.where(lane == 3, c_b, 0))))
    rw_out[...] = jnp.where(lane == 0, w1, jnp.where(lane == 1, w2, 0.0))


def _mixer_out(a, sg, y, h, lp, tm):
    t = h.shape[0]
    const = lambda *shape: pl.BlockSpec(shape, lambda i: (0,) * len(shape))
    rows = lambda w: pl.BlockSpec((tm, w), lambda i: (i, 0))
    return pl.pallas_call(
        _mixer_out_kernel,
        grid=(t // tm,),
        in_specs=[rows(QKV_PAD), rows(SG_WIDTH), rows(SSM_WIDTH), rows(D_MODEL), const(1, QKV_PAD),
                  const(1, SG_WIDTH), const(1, SSM_WIDTH), const(QKV_PAD, D_MODEL),
                  const(SG_WIDTH, D_MODEL), const(SSM_WIDTH, D_MODEL), const(SSM_WIDTH, SSM_WIDTH),
                  const(1, SSM_WIDTH), const(1, D_MODEL), const(D_MODEL, LANES), const(D_MODEL, LANES),
                  const(1, LANES), const(tm, tm)],
        out_specs=[rows(D_MODEL), rows(D_MODEL), rows(LANES), rows(LANES), const(8, LANES)],
        out_shape=[jax.ShapeDtypeStruct((t, D_MODEL), F32), jax.ShapeDtypeStruct((t, D_MODEL), F32),
                   jax.ShapeDtypeStruct((t, LANES), jnp.int32), jax.ShapeDtypeStruct((t, LANES), F32),
                   jax.ShapeDtypeStruct((8, LANES), F32)],
        scratch_shapes=[pltpu.VMEM((1, LANES), F32)],
        compiler_params=pltpu.CompilerParams(dimension_semantics=("arbitrary",),
                                             vmem_limit_bytes=VMEM_LIMIT),
        name="mixer_out_router",
    )(a, sg, y, h, lp["gm_a"], lp["gm_g"], lp["gm_s"], lp["wo_a"], lp["wo_g"], lp["wo_s"], lp["wglu"],
      lp["bglu"], lp["ln2"], lp["wr_hi"], lp["wr_lo"], lp["br"], lp["ltri"])


def _row_copy(src_ref, src_row, dst_ref, dst_row, sem):
    return pltpu.make_async_copy(src_ref.at[pl.ds(src_row, 1)], dst_ref.at[pl.ds(dst_row, 1)], sem)


def _rows_wait(src_ref, dst_ref, n_rows, sem):
    pltpu.make_async_copy(src_ref.at[pl.ds(0, n_rows)], dst_ref.at[pl.ds(0, n_rows)], sem).wait()


def _dispatch_kernel(pstart_ref, idx_ref, x_ref, xs_in_ref, xs_ref, sem, *, td):
    del xs_in_ref
    i = pl.program_id(0)
    base = i * td

    def issue(r, _):
        d0 = pstart_ref[idx_ref[0, 0, 4 * r]] + idx_ref[0, 0, 4 * r + 2]
        d1 = pstart_ref[idx_ref[0, 0, 4 * r + 1]] + idx_ref[0, 0, 4 * r + 3]
        _row_copy(x_ref, base + r, xs_ref, d0, sem).start()
        _row_copy(x_ref, base + r, xs_ref, d1, sem).start()
        return 0

    lax.fori_loop(0, td, issue, 0, unroll=DMA_ISSUE_UNROLL)

    @pl.when(i > 0)
    def _():
        _rows_wait(x_ref, xs_ref, td, sem)
        _rows_wait(x_ref, xs_ref, td, sem)

    @pl.when(i == pl.num_programs(0) - 1)
    def _():
        _rows_wait(x_ref, xs_ref, td, sem)
        _rows_wait(x_ref, xs_ref, td, sem)


def _dispatch(pstart, idx, xn, xs_zero, td):
    t = xn.shape[0]
    grid_spec = pltpu.PrefetchScalarGridSpec(
        num_scalar_prefetch=1,
        grid=(t // td,),
        in_specs=[pl.BlockSpec((1, 1, 4 * td), lambda i, ps: (i, 0, 0), memory_space=pltpu.SMEM),
                  pl.BlockSpec(memory_space=pl.ANY),
                  pl.BlockSpec(memory_space=pl.ANY)],
        out_specs=pl.BlockSpec(memory_space=pl.ANY),
        scratch_shapes=[pltpu.SemaphoreType.DMA(())],
    )
    return pl.pallas_call(
        functools.partial(_dispatch_kernel, td=td),
        grid_spec=grid_spec,
        out_shape=jax.ShapeDtypeStruct(xs_zero.shape, xs_zero.dtype),
        input_output_aliases={3: 0},
        compiler_params=pltpu.CompilerParams(dimension_semantics=("arbitrary",)),
        name="moe_dispatch",
    )(pstart, idx, xn, xs_zero)


def _expert_kernel(blk_e_ref, n_valid_ref, xs_ref, w13_ref, w2_ref, yb_ref):
    del blk_e_ref
    i = pl.program_id(0)

    @pl.when(i < n_valid_ref[0])
    def _():
        x = xs_ref[...].astype(BF16)
        hh = _dot(x, w13_ref[...])
        hb = jax.nn.silu(hh[:, :D_EXPERT]) * hh[:, D_EXPERT:]
        yb_ref[...] = _dot(hb.astype(BF16), w2_ref[...])

    @pl.when(i >= n_valid_ref[0])
    def _():
        yb_ref[...] = jnp.zeros_like(yb_ref)


def _experts(blk_e, n_valid, xs, w13, w2):
    n_rows = xs.shape[0]
    grid_spec = pltpu.PrefetchScalarGridSpec(
        num_scalar_prefetch=2,
        grid=(n_rows // EXPERT_BLOCK,),
        in_specs=[pl.BlockSpec((EXPERT_BLOCK, D_MODEL), lambda i, be, nv: (i, 0)),
                  pl.BlockSpec((None, D_MODEL, 2 * D_EXPERT), lambda i, be, nv: (be[i], 0, 0)),
                  pl.BlockSpec((None, D_EXPERT, D_MODEL), lambda i, be, nv: (be[i], 0, 0))],
        out_specs=pl.BlockSpec((EXPERT_BLOCK, D_MODEL), lambda i, be, nv: (i, 0)),
    )
    return pl.pallas_call(
        _expert_kernel,
        grid_spec=grid_spec,
        out_shape=jax.ShapeDtypeStruct((n_rows, D_MODEL), F32),
        compiler_params=pltpu.CompilerParams(dimension_semantics=("arbitrary",),
                                             vmem_limit_bytes=VMEM_LIMIT),
        name="moe_experts",
    )(blk_e, n_valid, xs, w13, w2)


def _combine_kernel(pstart_ref, idx_ref, idx_next_ref, h_ref, rw_ref, p_ref, wpg_ref, bpg_ref,
                    wple_ref, yb_ref, out_ref, y0_ref, y1_ref, sem, *, tc):
    i = pl.program_id(0)
    slot = i % 2

    def gather(tile_idx_ref, to_slot):
        def issue(r, _):
            d0 = pstart_ref[tile_idx_ref[0, 0, 4 * r]] + tile_idx_ref[0, 0, 4 * r + 2]
            d1 = pstart_ref[tile_idx_ref[0, 0, 4 * r + 1]] + tile_idx_ref[0, 0, 4 * r + 3]
            _row_copy(yb_ref, d0, y0_ref.at[to_slot], r, sem.at[to_slot]).start()
            _row_copy(yb_ref, d1, y1_ref.at[to_slot], r, sem.at[to_slot]).start()
            return 0

        lax.fori_loop(0, tc, issue, 0, unroll=DMA_ISSUE_UNROLL)

    @pl.when(i == 0)
    def _():
        gather(idx_ref, 0)

    @pl.when(i + 1 < pl.num_programs(0))
    def _():
        gather(idx_next_ref, 1 - slot)

    _rows_wait(yb_ref, y0_ref.at[slot], tc, sem.at[slot])
    _rows_wait(yb_ref, y1_ref.at[slot], tc, sem.at[slot])

    rw = rw_ref[...]
    h2 = h_ref[...] + rw[:, 0:1] * y0_ref[slot] + rw[:, 1:2] * y1_ref[slot]
    gate = jax.nn.sigmoid(_dot(h2.astype(BF16), wpg_ref[...]) + bpg_ref[...])
    out_ref[...] = h2 + gate * _dot(p_ref[...].astype(BF16), wple_ref[...])


def _combine(pstart, idx, h1, rw, p, lp, yb, tc):
    t = h1.shape[0]
    n_tiles = t // tc
    const = lambda *shape: pl.BlockSpec(shape, lambda i, ps: (0,) * len(shape))
    rows = lambda w: pl.BlockSpec((tc, w), lambda i, ps: (i, 0))
    grid_spec = pltpu.PrefetchScalarGridSpec(
        num_scalar_prefetch=1,
        grid=(n_tiles,),
        in_specs=[pl.BlockSpec((1, 1, 4 * tc), lambda i, ps: (i, 0, 0), memory_space=pltpu.SMEM),
                  pl.BlockSpec((1, 1, 4 * tc), lambda i, ps: (jnp.minimum(i + 1, n_tiles - 1), 0, 0),
                               memory_space=pltpu.SMEM),
                  rows(D_MODEL), rows(LANES), rows(PLE_DIM), const(D_MODEL, D_MODEL), const(1, D_MODEL),
                  const(PLE_DIM, D_MODEL), pl.BlockSpec(memory_space=pl.ANY)],
        out_specs=rows(D_MODEL),
        scratch_shapes=[pltpu.VMEM((2, tc, D_MODEL), F32), pltpu.VMEM((2, tc, D_MODEL), F32),
                        pltpu.SemaphoreType.DMA((2,))],
    )
    return pl.pallas_call(
        functools.partial(_combine_kernel, tc=tc),
        grid_spec=grid_spec,
        out_shape=jax.ShapeDtypeStruct((t, D_MODEL), F32),
        compiler_params=pltpu.CompilerParams(dimension_semantics=("arbitrary",),
                                             vmem_limit_bytes=VMEM_LIMIT),
        name="moe_combine_ple",
    )(pstart, idx, idx, h1, rw, p, lp["wpg"], lp["bpg"], lp["wple"], yb)


def _rope_lane_tables(seq):
    inv = 1.0 / (ROPE_THETA ** (jnp.arange(0, ROPE_DIM, 2, dtype=F32) / ROPE_DIM))
    ang = jnp.arange(seq, dtype=F32)[:, None] * inv[None, :]
    cos, sin = jnp.cos(ang), jnp.sin(ang)
    half = ROPE_DIM // 2
    cos128 = jnp.concatenate([jnp.ones((seq, NOPE_DIM), F32), cos, cos,
                              jnp.zeros((seq, LANES - QK_DIM), F32)], axis=1)
    sin128 = jnp.concatenate([jnp.zeros((seq, NOPE_DIM), F32), -sin, sin,
                              jnp.zeros((seq, LANES - QK_DIM), F32)], axis=1)
    del half
    return cos128, sin128


def _swap_rope_halves(w):
    half = ROPE_DIM // 2
    return jnp.concatenate([w[..., half:], w[..., :half]], axis=-1)


def _pad_heads(w, width):
    return jnp.pad(w, ((0, 0), (0, 0), (0, HEAD_PAD - width))).reshape(w.shape[0], QKV_PAD)


def _rope_gain_rows(g):
    pad = jnp.zeros((LANES - QK_DIM,), F32)
    ga = jnp.concatenate([g, pad])
    gb = jnp.concatenate([jnp.zeros((NOPE_DIM,), F32), _swap_rope_halves(g[NOPE_DIM:]), pad])
    return ga, gb


def _ssm_tables(lam_re, lam_im, log_dt, b_re, b_im, c_re, c_im, d_skip, n_steps):
    hp = lax.Precision.HIGHEST
    lr = jnp.minimum(lam_re, -1e-4)
    li = lam_im
    dt = jnp.exp(log_dt)
    mag = jnp.exp(lr * dt)
    ab_re, ab_im = mag * jnp.cos(li * dt), mag * jnp.sin(li * dt)
    den = lr * lr + li * li
    nr, ni = ab_re - 1.0, ab_im
    fr = (nr * lr + ni * li) / den
    fi = (ni * lr - nr * li) / den
    bb_re = fr[..., None] * b_re - fi[..., None] * b_im
    bb_im = fr[..., None] * b_im + fi[..., None] * b_re

    steps = jnp.arange(SSM_CHUNK + 1, dtype=F32)[:, None, None]
    pmag = jnp.exp(lr * dt * steps)
    pw_re, pw_im = pmag * jnp.cos(li * dt * steps), pmag * jnp.sin(li * dt * steps)

    w_re = pw_re[:SSM_CHUNK, :, :, None] * bb_re - pw_im[:SSM_CHUNK, :, :, None] * bb_im
    w_im = pw_re[:SSM_CHUNK, :, :, None] * bb_im + pw_im[:SSM_CHUNK, :, :, None] * bb_re
    taps = (jnp.einsum("gcn,tgnd->tgcd", c_re, w_re, precision=hp)
            - jnp.einsum("gcn,tgnd->tgcd", c_im, w_im, precision=hp))
    taps = taps.at[0].add(d_skip[:, :, None] * jnp.eye(SSM_CH, dtype=F32)[None])
    t_in = jnp.arange(SSM_CHUNK)[:, None]
    t_out = jnp.arange(SSM_CHUNK)[None, :]
    lag = t_out - t_in
    m = jnp.where((lag >= 0)[:, :, None, None, None], taps[jnp.clip(lag, 0)], 0.0)
    m = m.transpose(2, 0, 4, 1, 3)

    p_re = w_re[::-1].transpose(1, 0, 3, 2)
    p_im = w_im[::-1].transpose(1, 0, 3, 2)

    up_re, up_im = pw_re[1:], pw_im[1:]
    q_re = (c_re[None] * up_re[:, :, None, :] - c_im[None] * up_im[:, :, None, :])
    q_im = -(c_re[None] * up_im[:, :, None, :] + c_im[None] * up_re[:, :, None, :])
    q_re = q_re.transpose(1, 3, 0, 2)
    q_im = q_im.transpose(1, 3, 0, 2)

    are, aim = [pw_re[SSM_CHUNK]], [pw_im[SSM_CHUNK]]
    for _ in range(n_steps - 1):
        r, im = are[-1], aim[-1]
        are.append(r * r - im * im)
        aim.append(2.0 * r * im)
    are, aim = jnp.stack(are, axis=1), jnp.stack(aim, axis=1)

    nbu = SSM_GROUPS // SSM_BUNDLE
    eye = jnp.eye(SSM_BUNDLE, dtype=F32)
    bundle = lambda x: x.reshape((nbu, SSM_BUNDLE) + x.shape[1:])
    m_b = (bundle(m).transpose(0, 2, 1, 3, 4, 5)[:, :, :, :, :, None, :]
           * eye[None, None, :, None, None, :, None]).reshape(nbu, SSM_COLS, SSM_COLS)

    def state_in(x):
        x = bundle(x).transpose(0, 2, 1, 3, 4)[:, :, :, :, None, :] * eye[None, None, :, None, :, None]
        return x.reshape(nbu, SSM_COLS, SSM_SLANES)

    def state_out(x):
        x = bundle(x)[:, :, :, :, None, :] * eye[None, :, None, None, :, None]
        return x.reshape(nbu, SSM_SLANES, SSM_COLS)

    def lanes(x):
        return bundle(x).transpose(0, 2, 1, 3).reshape(nbu, x.shape[1], SSM_SLANES)

    return dict(ssm_m=m_b.astype(BF16), ssm_pre=state_in(p_re).astype(BF16),
                ssm_pim=state_in(p_im).astype(BF16), ssm_qre=state_out(q_re).astype(BF16),
                ssm_qim=state_out(q_im).astype(BF16), ssm_are=lanes(are), ssm_aim=lanes(aim))


def _layer_params(i, tm, n_scan_steps, ln1, w_in, g_cq, g_ckv, w_uq, w_ukv, g_qh, g_kh, sg_ln_g,
                  sg_ln_b, sg_w, sg_b, lam_re, lam_im, log_dt, ssm_b_re, ssm_b_im, ssm_c_re, ssm_c_im,
                  ssm_d, w_glu, b_glu, g_mix, w_out, ln2, w_rg, b_rg, w_rexp, b_rexp, w1, w3, w2, w_pg,
                  b_pg, w_ple):
    lp = {}
    wi = w_in[i]
    off_kr = Q_RANK + KV_RANK
    off_sg = off_kr + ROPE_DIM
    kr = wi[:, off_kr:off_sg]
    zpad = lambda n: jnp.zeros((D_MODEL, n), F32)
    kra = jnp.concatenate([zpad(NOPE_DIM), kr, zpad(LANES - QK_DIM)], axis=1)
    krb = jnp.concatenate([zpad(NOPE_DIM), _swap_rope_halves(kr), zpad(LANES - QK_DIM)], axis=1)
    lp["win"] = jnp.concatenate([wi[:, :off_kr], kra, krb, wi[:, off_sg:]], axis=1).astype(BF16)
    lp["ln1"] = ln1[i][None]
    lp["gcq"] = g_cq[i][None]
    lp["gckv"] = g_ckv[i][None]

    wq = w_uq[i].reshape(Q_RANK, ATTN_HEADS, QK_DIM)
    lp["wqa"] = _pad_heads(wq, QK_DIM).astype(BF16)
    wq_sw = jnp.concatenate([jnp.zeros((Q_RANK, ATTN_HEADS, NOPE_DIM), F32),
                             _swap_rope_halves(wq[..., NOPE_DIM:])], axis=-1)
    lp["wqb"] = _pad_heads(wq_sw, QK_DIM).astype(BF16)
    wkv = w_ukv[i].reshape(KV_RANK, ATTN_HEADS, NOPE_DIM + V_DIM)
    lp["wk"] = _pad_heads(wkv[..., :NOPE_DIM], NOPE_DIM).astype(BF16)
    lp["wv"] = _pad_heads(wkv[..., NOPE_DIM:], V_DIM).astype(BF16)
    gq_a, gq_b = _rope_gain_rows(g_qh[i])
    gk_a, gk_b = _rope_gain_rows(g_kh[i])
    lp["gqk"] = jnp.concatenate([jnp.stack([gq_a, gq_b, gk_a, gk_b]), jnp.zeros((4, LANES), F32)])
    lp["vones"] = jnp.tile((jnp.arange(HEAD_PAD) == V_DIM).astype(F32), ATTN_HEADS)[None]

    lp["sg_lng"] = sg_ln_g[i][None]
    lp["sg_lnb"] = sg_ln_b[i][None]
    lp["sg_w"] = sg_w[i]
    lp["sg_bias"] = jnp.repeat(sg_b[i].T, SG_CH, axis=1)
    grp = jnp.arange(SG_WIDTH) // SG_CH
    lp["mmean"] = ((grp[:, None] == grp[None, :]).astype(F32) / SG_CH).astype(BF16)

    lp.update(_ssm_tables(lam_re[i], lam_im[i], log_dt[i], ssm_b_re[i], ssm_b_im[i], ssm_c_re[i],
                          ssm_c_im[i], ssm_d[i], n_scan_steps))

    gm = g_mix[i]
    aw = ATTN_HEADS * V_DIM
    lp["gm_a"] = _pad_heads(gm[:aw].reshape(1, ATTN_HEADS, V_DIM), V_DIM)
    lp["gm_g"] = gm[aw:aw + SG_WIDTH][None]
    lp["gm_s"] = gm[aw + SG_WIDTH:][None]
    wo = w_out[i]
    woa = wo[:aw].reshape(ATTN_HEADS, V_DIM, D_MODEL)
    lp["wo_a"] = jnp.pad(woa, ((0, 0), (0, HEAD_PAD - V_DIM), (0, 0))).reshape(QKV_PAD, D_MODEL).astype(BF16)
    lp["wo_g"] = wo[aw:aw + SG_WIDTH].astype(BF16)
    lp["wo_s"] = wo[aw + SG_WIDTH:].astype(BF16)
    lp["wglu"] = w_glu[i].astype(BF16)
    lp["bglu"] = b_glu[i][None]
    lp["ln2"] = ln2[i][None]
    wr = jnp.concatenate([w_rg[i], w_rexp[i], jnp.zeros((D_MODEL, LANES - MOE_GROUPS - N_EXPERTS), F32)],
                         axis=1)
    lp["wr_hi"], lp["wr_lo"] = _split_bf16(wr)
    lp["br"] = jnp.concatenate([b_rg[i], b_rexp[i], jnp.zeros((LANES - MOE_GROUPS - N_EXPERTS,), F32)])[None]
    lp["ltri"] = (jnp.arange(tm)[None, :] < jnp.arange(tm)[:, None]).astype(BF16)

    lp["w13"] = jnp.concatenate([w1[i], w3[i]], axis=-1).astype(BF16)
    lp["w2"] = w2[i].astype(BF16)
    lp["wpg"] = w_pg[i].astype(BF16)
    lp["bpg"] = b_pg[i][None]
    lp["wple"] = w_ple[i].astype(BF16)
    return lp


def kernel(x, p, ln1, w_in, g_cq, g_ckv, w_uq, w_ukv, g_qh, g_kh, sg_ln_g, sg_ln_b, sg_w, sg_b, lam_re, lam_im, log_dt, ssm_b_re, ssm_b_im, ssm_c_re, ssm_c_im, ssm_d, w_glu, b_glu, g_mix, w_out, ln2, w_rg, b_rg, w_rexp, b_rexp, w1, w3, w2, w_pg, b_pg, w_ple):
    batch, seq, _ = x.shape
    depth = p.shape[0]
    t = batch * seq
    tm = min(512, seq)
    tq = min(512, seq)
    td = min(512, seq)
    tc = min(256, seq)
    assert seq % tm == 0 and seq % SSM_CHUNK == 0 and tm % SG_CHUNK == 0
    rows_per_batch = seq // SSM_CHUNK
    n_scan_steps = max(1, (rows_per_batch - 1).bit_length())
    cr = t // SSM_CHUNK
    n_bundles = SSM_GROUPS // SSM_BUNDLE
    bundle_w = SSM_BUNDLE * SSM_CH
    tk2 = 2 * t
    n_blocks = -(-(tk2 + N_EXPERTS * (EXPERT_BLOCK - 1)) // EXPERT_BLOCK)
    n_rows = n_blocks * EXPERT_BLOCK

    cos128, sin128 = _rope_lane_tables(seq)
    h = x.reshape(t, D_MODEL)
    weights = (ln1, w_in, g_cq, g_ckv, w_uq, w_ukv, g_qh, g_kh, sg_ln_g, sg_ln_b, sg_w, sg_b, lam_re,
               lam_im, log_dt, ssm_b_re, ssm_b_im, ssm_c_re, ssm_c_im, ssm_d, w_glu, b_glu, g_mix, w_out,
               ln2, w_rg, b_rg, w_rexp, b_rexp, w1, w3, w2, w_pg, b_pg, w_ple)
    for i in range(depth):
        lp = _layer_params(i, tm, n_scan_steps, *weights)
        q, k, v, sg, xs_ssm = _mixer_in(h, lp, cos128, sin128, tm, seq // tm)
        a = _attention(q, k, v, batch, seq, tq)

        xp = xs_ssm.reshape(cr, SSM_CHUNK, n_bundles, bundle_w).transpose(2, 0, 1, 3)
        yp = _ssm(xp.reshape(n_bundles, cr, SSM_COLS), lp, rows_per_batch)
        y = yp.reshape(n_bundles, cr, SSM_CHUNK, bundle_w).transpose(1, 2, 0, 3).reshape(t, SSM_WIDTH)

        h1, xn, ri, rw, cnt = _mixer_out(a, sg, y, h, lp, tm)

        counts = cnt[0, :N_EXPERTS].astype(jnp.int32)
        pcounts = (counts + EXPERT_BLOCK - 1) // EXPERT_BLOCK * EXPERT_BLOCK
        pend = jnp.cumsum(pcounts)
        pstart = (pend - pcounts).astype(jnp.int32)
        n_valid = (pend[-1:] // EXPERT_BLOCK).astype(jnp.int32)
        blk_first_row = jnp.arange(n_blocks, dtype=jnp.int32) * EXPERT_BLOCK
        blk_e = jnp.minimum(jnp.sum((pend[None, :] <= blk_first_row[:, None]).astype(jnp.int32), axis=1),
                            N_EXPERTS - 1)
        idx = ri[:, :4]
        xs = _dispatch(pstart, idx.reshape(t // td, 1, 4 * td), xn, jnp.zeros((n_rows, D_MODEL), F32), td)
        yb = _experts(blk_e, n_valid, xs, lp["w13"], lp["w2"])
        h = _combine(pstart, idx.reshape(t // tc, 1, 4 * tc), h1, rw, p[i].reshape(t, PLE_DIM), lp, yb, tc)
    return h.reshape(batch, seq, D_MODEL)
```

```python
import functools
import math

import jax
import jax.numpy as jnp
from jax import lax
from jax.experimental import pallas as pl
from jax.experimental.pallas import tpu as pltpu

F32 = jnp.float32
BF16 = jnp.bfloat16

D_MODEL = 1024
ATTN_HEADS = 8
NOPE_DIM = 64
ROPE_DIM = 32
QK_DIM = NOPE_DIM + ROPE_DIM
V_DIM = 64
Q_RANK = 384
KV_RANK = 256
SG_GROUPS = 4
SG_CH = 64
SG_WIDTH = SG_GROUPS * SG_CH
SG_CHUNK = 128
SSM_GROUPS = 16
SSM_CH = 16
SSM_WIDTH = SSM_GROUPS * SSM_CH
SSM_STATE = 64
MOE_GROUPS = 4
EXPERTS_PER_GROUP = 8
N_EXPERTS = MOE_GROUPS * EXPERTS_PER_GROUP
D_EXPERT = 256
PLE_DIM = 256
ROPE_THETA = 10000.0
EPS = 1e-6

LANES = 128
HEAD_PAD = LANES
QKV_PAD = ATTN_HEADS * HEAD_PAD
ATTN_HEADS_PER_STEP = 2
ATTN_BLOCKS_PER_ITER = 4
SSM_CHUNK = 16
SSM_BUNDLE = 4
SSM_COLS = SSM_BUNDLE * SSM_CHUNK * SSM_CH
SSM_SLANES = SSM_BUNDLE * SSM_STATE
EXPERT_BLOCK = 256
SCALAR_LOOP_UNROLL = 8
VMEM_LIMIT = 56 * 1024 * 1024

C_Q = 0
C_KV = C_Q + Q_RANK
C_KRA = C_KV + KV_RANK
C_KRB = C_KRA + LANES
C_SG = C_KRB + LANES
C_SSM = C_SG + 2 * SG_WIDTH
C_END = C_SSM + SSM_WIDTH


def _dot(a, b):
    return jnp.dot(a, b, preferred_element_type=F32)


def _split_bf16(x):
    hi = x.astype(BF16)
    lo = (x - hi.astype(F32)).astype(BF16)
    return hi, lo


def _rms(x, g, width=None):
    width = x.shape[-1] if width is None else width
    ms = jnp.sum(x * x, axis=-1, keepdims=True) * (1.0 / width)
    return x * lax.rsqrt(ms + EPS) * g


def _mixer_in_kernel(h_ref, ln1_ref, win_ref, gcq_ref, gckv_ref, wqa_ref, wqb_ref, wk_ref, wv_ref,
                     cos_ref, sin_ref, gqk_ref, vones_ref, lng_ref, lnb_ref, sgw_ref, sgb_ref,
                     mmean_ref, q_out, k_out, v_out, sg_out, ssm_out, *, tm):
    xn = _rms(h_ref[...], ln1_ref[...])
    z = _dot(xn.astype(BF16), win_ref[...])

    cos = cos_ref[...]
    sin = sin_ref[...]
    gqk = gqk_ref[...]
    gqa = cos * gqk[0:1, :]
    gqb = sin * gqk[1:2, :]
    gka = cos * gqk[2:3, :]
    gkb = sin * gqk[3:4, :]
    scale = QK_DIM ** -0.5 * math.log2(math.e)

    qn = _rms(z[:, C_Q:C_KV], gcq_ref[...]).astype(BF16)
    qa = _dot(qn, wqa_ref[...])
    qb = _dot(qn, wqb_ref[...])
    for hd in range(ATTN_HEADS):
        sl = slice(hd * HEAD_PAD, (hd + 1) * HEAD_PAD)
        a = qa[:, sl]
        r = lax.rsqrt(jnp.sum(a * a, axis=-1, keepdims=True) * (1.0 / QK_DIM) + EPS)
        q_out[:, sl] = ((a * gqa + qb[:, sl] * gqb) * (r * scale)).astype(BF16)

    kvn = _rms(z[:, C_KV:C_KRA], gckv_ref[...]).astype(BF16)
    ka = _dot(kvn, wk_ref[...])
    v_out[...] = (_dot(kvn, wv_ref[...]) + vones_ref[...]).astype(BF16)
    kra = z[:, C_KRA:C_KRB]
    krb = z[:, C_KRB:C_SG]
    ss_rope = jnp.sum(kra * kra, axis=-1, keepdims=True)
    krot = kra * gka + krb * gkb
    for hd in range(ATTN_HEADS):
        sl = slice(hd * HEAD_PAD, (hd + 1) * HEAD_PAD)
        a = ka[:, sl]
        r = lax.rsqrt((jnp.sum(a * a, axis=-1, keepdims=True) + ss_rope) * (1.0 / QK_DIM) + EPS)
        k_out[:, sl] = ((a * gka + krot) * r).astype(BF16)

    zg = jax.nn.gelu(z[:, C_SG:C_SSM])
    u = zg[:, :SG_WIDTH]
    vv = zg[:, SG_WIDTH:]
    mmean = mmean_ref[...]
    vh, vl = _split_bf16(vv)
    mu = _dot(vh, mmean) + _dot(vl, mmean)
    xc = vv - mu
    sh, sl_ = _split_bf16(xc * xc)
    var = _dot(sh, mmean) + _dot(sl_, mmean)
    vn = (xc * lax.rsqrt(var + EPS) * lng_ref[...] + lnb_ref[...]).astype(BF16)
    tri = (lax.broadcasted_iota(jnp.int32, (SG_CHUNK, SG_CHUNK), 1)
           <= lax.broadcasted_iota(jnp.int32, (SG_CHUNK, SG_CHUNK), 0))
    lane_grp = lax.broadcasted_iota(jnp.int32, (SG_CHUNK, SG_WIDTH), 1) // SG_CH
    ws = [jnp.where(tri, sgw_ref[g], 0.0).astype(BF16) for g in range(SG_GROUPS)]
    for c in range(tm // SG_CHUNK):
        rows = slice(c * SG_CHUNK, (c + 1) * SG_CHUNK)
        vc = vn[rows, :]
        mixed = sgb_ref[...]
        for g in range(SG_GROUPS):
            mixed = mixed + jnp.where(lane_grp == g, _dot(ws[g], vc), 0.0)
        sg_out[rows, :] = u[rows, :] * mixed

    ssm_out[...] = z[:, C_SSM:C_END].astype(BF16)


def _mixer_in(h, lp, cos128, sin128, tm, n_pos_tiles):
    t = h.shape[0]
    const = lambda *shape: pl.BlockSpec(shape, lambda i: (0,) * len(shape))
    rows = lambda w: pl.BlockSpec((tm, w), lambda i: (i, 0))
    pos = pl.BlockSpec((tm, LANES), lambda i: (i % n_pos_tiles, 0))
    return pl.pallas_call(
        functools.partial(_mixer_in_kernel, tm=tm),
        grid=(t // tm,),
        in_specs=[rows(D_MODEL), const(1, D_MODEL), const(D_MODEL, C_END), const(1, Q_RANK),
                  const(1, KV_RANK), const(Q_RANK, QKV_PAD), const(Q_RANK, QKV_PAD),
                  const(KV_RANK, QKV_PAD), const(KV_RANK, QKV_PAD), pos, pos, const(8, LANES),
                  const(1, QKV_PAD), const(1, SG_WIDTH), const(1, SG_WIDTH),
                  const(SG_GROUPS, SG_CHUNK, SG_CHUNK), const(SG_CHUNK, SG_WIDTH),
                  const(SG_WIDTH, SG_WIDTH)],
        out_specs=[rows(QKV_PAD), rows(QKV_PAD), rows(QKV_PAD), rows(SG_WIDTH), rows(SSM_WIDTH)],
        out_shape=[jax.ShapeDtypeStruct((t, QKV_PAD), BF16), jax.ShapeDtypeStruct((t, QKV_PAD), BF16),
                   jax.ShapeDtypeStruct((t, QKV_PAD), BF16), jax.ShapeDtypeStruct((t, SG_WIDTH), F32),
                   jax.ShapeDtypeStruct((t, SSM_WIDTH), BF16)],
        compiler_params=pltpu.CompilerParams(dimension_semantics=("parallel",),
                                             vmem_limit_bytes=VMEM_LIMIT),
        name="mixer_in",
    )(h, lp["ln1"], lp["win"], lp["gcq"], lp["gckv"], lp["wqa"], lp["wqb"], lp["wk"], lp["wv"],
      cos128, sin128, lp["gqk"], lp["vones"], lp["sg_lng"], lp["sg_lnb"], lp["sg_w"], lp["sg_bias"],
      lp["mmean"])


def _attn_kernel(q_ref, k_ref, v_ref, o_ref, s_even, s_odd, m_ref, acc_ref, *, tq):
    i = pl.program_id(2)
    heads = [slice(hh * HEAD_PAD, (hh + 1) * HEAD_PAD) for hh in range(ATTN_HEADS_PER_STEP)]

    def block_rows(j):
        return pl.ds(pl.multiple_of(j * tq, tq), tq)

    def scores(j, s_ref):
        for hh, sl in enumerate(heads):
            s_ref[hh] = lax.dot_general(q_ref[:, sl], k_ref[block_rows(j), sl],
                                        (((1,), (1,)), ((), ())), preferred_element_type=F32)

    def accumulate(j, s_ref, masked):
        for hh, sl in enumerate(heads):
            s = s_ref[hh]
            if masked:
                keep = (lax.broadcasted_iota(jnp.int32, (tq, tq), 1)
                        <= lax.broadcasted_iota(jnp.int32, (tq, tq), 0))
                s = jnp.where(keep, s, -jnp.inf)
            m = m_ref[hh]
            m_new = jnp.maximum(m, jnp.max(s, axis=-1, keepdims=True))
            alpha = jnp.exp2(m - m_new)
            p = jnp.exp2(s - m_new).astype(BF16)
            m_ref[hh] = m_new
            acc_ref[hh] = alpha * acc_ref[hh] + _dot(p, v_ref[block_rows(j), sl])

    m_ref[...] = jnp.full(m_ref.shape, -jnp.inf, F32)
    acc_ref[...] = jnp.zeros(acc_ref.shape, F32)
    scores(0, s_even)

    def pipelined(t, _):
        j = ATTN_BLOCKS_PER_ITER * t
        for u in range(0, ATTN_BLOCKS_PER_ITER, 2):
            scores(j + u + 1, s_odd)
            accumulate(j + u, s_even, False)
            scores(j + u + 2, s_even)
            accumulate(j + u + 1, s_odd, False)
        return 0

    n_pipelined = i // ATTN_BLOCKS_PER_ITER
    lax.fori_loop(0, n_pipelined, pipelined, 0)

    def leftover(j, _):
        accumulate(j, s_even, False)
        scores(j + 1, s_even)
        return 0

    lax.fori_loop(n_pipelined * ATTN_BLOCKS_PER_ITER, i, leftover, 0)
    accumulate(i, s_even, True)

    lane = lax.broadcasted_iota(jnp.int32, (tq, HEAD_PAD), 1)
    for hh, sl in enumerate(heads):
        acc = acc_ref[hh]
        out = acc / acc[:, V_DIM:V_DIM + 1]
        o_ref[:, sl] = jnp.where(lane < V_DIM, out, 0.0).astype(BF16)


def _attention(q, k, v, batch, seq, tq):
    t = q.shape[0]
    nq = seq // tq
    width = ATTN_HEADS_PER_STEP * HEAD_PAD
    return pl.pallas_call(
        functools.partial(_attn_kernel, tq=tq),
        grid=(batch, ATTN_HEADS // ATTN_HEADS_PER_STEP, nq),
        in_specs=[pl.BlockSpec((tq, width), lambda b, h, i: (b * nq + i, h)),
                  pl.BlockSpec((seq, width), lambda b, h, i: (b, h)),
                  pl.BlockSpec((seq, width), lambda b, h, i: (b, h))],
        out_specs=pl.BlockSpec((tq, width), lambda b, h, i: (b * nq + i, h)),
        out_shape=jax.ShapeDtypeStruct((t, QKV_PAD), BF16),
        scratch_shapes=[pltpu.VMEM((ATTN_HEADS_PER_STEP, tq, tq), F32),
                        pltpu.VMEM((ATTN_HEADS_PER_STEP, tq, tq), F32),
                        pltpu.VMEM((ATTN_HEADS_PER_STEP, tq, 1), F32),
                        pltpu.VMEM((ATTN_HEADS_PER_STEP, tq, HEAD_PAD), F32)],
        compiler_params=pltpu.CompilerParams(
            dimension_semantics=("parallel", "parallel", "arbitrary"), vmem_limit_bytes=VMEM_LIMIT),
        name="mla_attention",
    )(q, k, v)


def _ssm_kernel(x_ref, m_ref, pre_ref, pim_ref, qre_ref, qim_ref, are_ref, aim_ref, y_ref, *,
                rows_per_batch, n_steps):
    x = x_ref[0]
    cr = x.shape[0]
    y_local = _dot(x, m_ref[0])
    sre = _dot(x, pre_ref[0])
    sim = _dot(x, pim_ref[0])
    row = lax.broadcasted_iota(jnp.int32, (cr, SSM_SLANES), 0) % rows_per_batch
    are = are_ref[0]
    aim = aim_ref[0]
    for k in range(n_steps):
        sh = 1 << k
        ok = row >= sh
        pr = jnp.where(ok, pltpu.roll(sre, sh, 0), 0.0)
        pi = jnp.where(ok, pltpu.roll(sim, sh, 0), 0.0)
        ar = are[k:k + 1, :]
        ai = aim[k:k + 1, :]
        sre, sim = sre + ar * pr - ai * pi, sim + ar * pi + ai * pr
    ok = row >= 1
    hre = jnp.where(ok, pltpu.roll(sre, 1, 0), 0.0).astype(BF16)
    him = jnp.where(ok, pltpu.roll(sim, 1, 0), 0.0).astype(BF16)
    y_ref[0] = y_local + _dot(hre, qre_ref[0]) + _dot(him, qim_ref[0])


def _ssm(xp, lp, rows_per_batch):
    n_bundles, cr, _ = xp.shape
    n_steps = lp["ssm_are"].shape[1]
    blk = lambda *shape: pl.BlockSpec((1,) + shape, lambda p: (p,) + (0,) * len(shape))
    return pl.pallas_call(
        functools.partial(_ssm_kernel, rows_per_batch=rows_per_batch, n_steps=n_steps),
        grid=(n_bundles,),
        in_specs=[blk(cr, SSM_COLS), blk(SSM_COLS, SSM_COLS), blk(SSM_COLS, SSM_SLANES),
                  blk(SSM_COLS, SSM_SLANES), blk(SSM_SLANES, SSM_COLS), blk(SSM_SLANES, SSM_COLS),
                  blk(n_steps, SSM_SLANES), blk(n_steps, SSM_SLANES)],
        out_specs=blk(cr, SSM_COLS),
        out_shape=jax.ShapeDtypeStruct((n_bundles, cr, SSM_COLS), F32),
        compiler_params=pltpu.CompilerParams(dimension_semantics=("parallel",),
                                             vmem_limit_bytes=VMEM_LIMIT),
        name="s5_mixer",
    )(xp, lp["ssm_m"], lp["ssm_pre"], lp["ssm_pim"], lp["ssm_qre"], lp["ssm_qim"], lp["ssm_are"],
      lp["ssm_aim"])


def _mixer_out_kernel(a_ref, sg_ref, y_ref, h_ref, gma_ref, gmg_ref, gms_ref, woa_ref, wog_ref,
                      wos_ref, wglu_ref, bglu_ref, ln2_ref, wrh_ref, wrl_ref, br_ref, ltri_ref,
                      h1_out, xn_out, ri_out, rw_out, cnt_out, run_ref):
    i = pl.program_id(0)

    @pl.when(i == 0)
    def _():
        run_ref[...] = jnp.zeros_like(run_ref)

    g = jax.nn.gelu(y_ref[...])
    glu = g * jax.nn.sigmoid(_dot(g.astype(BF16), wglu_ref[...]) + bglu_ref[...])
    a_n = _rms(a_ref[...].astype(F32), gma_ref[...], ATTN_HEADS * V_DIM)
    g_n = _rms(sg_ref[...], gmg_ref[...])
    s_n = _rms(glu, gms_ref[...])
    h1 = (h_ref[...] + _dot(a_n.astype(BF16), woa_ref[...]) + _dot(g_n.astype(BF16), wog_ref[...])
          + _dot(s_n.astype(BF16), wos_ref[...]))
    h1_out[...] = h1
    xn = _rms(h1, ln2_ref[...])
    xn_out[...] = xn

    xh, xl = _split_bf16(xn)
    logit = _dot(xh, wrh_ref[...]) + _dot(xl, wrh_ref[...]) + _dot(xh, wrl_ref[...]) + br_ref[...]
    shape = logit.shape
    lane = lax.broadcasted_iota(jnp.int32, shape, 1)
    neg = -jnp.inf
    is_grp = lane < MOE_GROUPS
    gl = jnp.where(is_grp, logit, neg)
    gmax = jnp.max(gl, axis=-1, keepdims=True)
    gsel = jnp.min(jnp.where(gl == gmax, lane, LANES), axis=-1, keepdims=True)
    gsum = jnp.sum(jnp.where(is_grp, jnp.exp(logit - gmax), 0.0), axis=-1, keepdims=True)
    g_w = 1.0 / gsum
    lo = MOE_GROUPS + gsel * EXPERTS_PER_GROUP
    in_grp = (lane >= lo) & (lane < lo + EXPERTS_PER_GROUP)
    el = jnp.where(in_grp, logit, neg)
    m1 = jnp.max(el, axis=-1, keepdims=True)
    i1 = jnp.min(jnp.where(el == m1, lane, LANES), axis=-1, keepdims=True)
    el2 = jnp.where(lane == i1, neg, el)
    m2 = jnp.max(el2, axis=-1, keepdims=True)
    i2 = jnp.min(jnp.where(el2 == m2, lane, LANES), axis=-1, keepdims=True)
    e2 = jnp.exp(m2 - m1)
    w1 = g_w / (1.0 + e2)
    w2 = g_w * e2 / (1.0 + e2)
    e_a = i1 - MOE_GROUPS
    e_b = i2 - MOE_GROUPS

    hit_a = lane == e_a
    hit_b = lane == e_b
    onehot = jnp.where(hit_a | hit_b, 1.0, 0.0).astype(BF16)
    before = _dot(ltri_ref[...], onehot) + run_ref[...]
    c_a = jnp.sum(jnp.where(hit_a, before, 0.0), axis=-1, keepdims=True).astype(jnp.int32)
    c_b = jnp.sum(jnp.where(hit_b, before, 0.0), axis=-1, keepdims=True).astype(jnp.int32)
    run_ref[...] = run_ref[...] + jnp.sum(onehot.astype(F32), axis=0, keepdims=True)
    cnt_out[...] = jnp.broadcast_to(run_ref[...], cnt_out.shape)

    ri_out[...] = jnp.where(lane == 0, e_a, jnp.where(lane == 1, e_b, jnp.where(
        lane == 2, c_a, jnp.where(lane == 3, c_b, 0))))
    rw_out[...] = jnp.where(lane == 0, w1, jnp.where(lane == 1, w2, 0.0))


def _mixer_out(a, sg, y, h, lp, tm):
    t = h.shape[0]
    const = lambda *shape: pl.BlockSpec(shape, lambda i: (0,) * len(shape))
    rows = lambda w: pl.BlockSpec((tm, w), lambda i: (i, 0))
    return pl.pallas_call(
        _mixer_out_kernel,
        grid=(t // tm,),
        in_specs=[rows(QKV_PAD), rows(SG_WIDTH), rows(SSM_WIDTH), rows(D_MODEL), const(1, QKV_PAD),
                  const(1, SG_WIDTH), const(1, SSM_WIDTH), const(QKV_PAD, D_MODEL),
                  const(SG_WIDTH, D_MODEL), const(SSM_WIDTH, D_MODEL), const(SSM_WIDTH, SSM_WIDTH),
                  const(1, SSM_WIDTH), const(1, D_MODEL), const(D_MODEL, LANES), const(D_MODEL, LANES),
                  const(1, LANES), const(tm, tm)],
        out_specs=[rows(D_MODEL), rows(D_MODEL), rows(LANES), rows(LANES), const(8, LANES)],
        out_shape=[jax.ShapeDtypeStruct((t, D_MODEL), F32), jax.ShapeDtypeStruct((t, D_MODEL), F32),
                   jax.ShapeDtypeStruct((t, LANES), jnp.int32), jax.ShapeDtypeStruct((t, LANES), F32),
                   jax.ShapeDtypeStruct((8, LANES), F32)],
        scratch_shapes=[pltpu.VMEM((1, LANES), F32)],
        compiler_params=pltpu.CompilerParams(dimension_semantics=("arbitrary",),
                                             vmem_limit_bytes=VMEM_LIMIT),
        name="mixer_out_router",
    )(a, sg, y, h, lp["gm_a"], lp["gm_g"], lp["gm_s"], lp["wo_a"], lp["wo_g"], lp["wo_s"], lp["wglu"],
      lp["bglu"], lp["ln2"], lp["wr_hi"], lp["wr_lo"], lp["br"], lp["ltri"])


def _row_copy(src_ref, src_row, dst_ref, dst_row, sem):
    return pltpu.make_async_copy(src_ref.at[pl.ds(src_row, 1)], dst_ref.at[pl.ds(dst_row, 1)], sem)


def _rows_wait(src_ref, dst_ref, n_rows, sem):
    pltpu.make_async_copy(src_ref.at[pl.ds(0, n_rows)], dst_ref.at[pl.ds(0, n_rows)], sem).wait()


def _slot_map_kernel(dest_ref, zeros_ref, row_tok_ref, sem, *, td):
    i = pl.program_id(0)

    @pl.when(i == 0)
    def _():
        clear = pltpu.make_async_copy(zeros_ref, row_tok_ref, sem)
        clear.start()
        clear.wait()

    def place(r, _):
        row_tok_ref[dest_ref[0, 0, 2 * r]] = i * td + r
        row_tok_ref[dest_ref[0, 0, 2 * r + 1]] = i * td + r
        return 0

    lax.fori_loop(0, td, place, 0, unroll=SCALAR_LOOP_UNROLL)


def _slot_map(dest, n_rows, td):
    n_tiles = dest.shape[0]
    return pl.pallas_call(
        functools.partial(_slot_map_kernel, td=td),
        grid=(n_tiles,),
        in_specs=[pl.BlockSpec((1, 1, 2 * td), lambda i: (i, 0, 0), memory_space=pltpu.SMEM),
                  pl.BlockSpec(memory_space=pl.ANY)],
        out_specs=pl.BlockSpec(memory_space=pltpu.SMEM),
        out_shape=jax.ShapeDtypeStruct((n_rows,), jnp.int32),
        scratch_shapes=[pltpu.SemaphoreType.DMA(())],
        compiler_params=pltpu.CompilerParams(dimension_semantics=("arbitrary",)),
        name="moe_slot_map",
    )(dest, jnp.zeros((n_rows,), jnp.int32))


def _expert_kernel(row_tok_ref, blk_e_ref, x_ref, w13_ref, w2_ref, yb_ref, xbuf, sem):
    del blk_e_ref
    b = pl.program_id(0)
    last = pl.num_programs(0) - 1
    slot = b % 2

    def gather(block, to_slot):
        base = block * EXPERT_BLOCK
        for r in range(EXPERT_BLOCK):
            _row_copy(x_ref, row_tok_ref[base + r], xbuf.at[to_slot], r, sem.at[to_slot]).start()

    @pl.when(b == 0)
    def _():
        gather(0, 0)

    gather(jnp.minimum(b + 1, last), 1 - slot)
    _rows_wait(x_ref, xbuf.at[slot], EXPERT_BLOCK, sem.at[slot])
    x = xbuf[slot].astype(BF16)
    hh = _dot(x, w13_ref[...])
    hb = jax.nn.silu(hh[:, :D_EXPERT]) * hh[:, D_EXPERT:]
    yb_ref[...] = _dot(hb.astype(BF16), w2_ref[...])

    @pl.when(b == last)
    def _():
        _rows_wait(x_ref, xbuf.at[1 - slot], EXPERT_BLOCK, sem.at[1 - slot])


def _experts(row_tok, blk_e, xn, w13, w2):
    n_rows = row_tok.shape[0]
    grid_spec = pltpu.PrefetchScalarGridSpec(
        num_scalar_prefetch=2,
        grid=(n_rows // EXPERT_BLOCK,),
        in_specs=[pl.BlockSpec(memory_space=pl.ANY),
                  pl.BlockSpec((None, D_MODEL, 2 * D_EXPERT), lambda i, rt, be: (be[i], 0, 0)),
                  pl.BlockSpec((None, D_EXPERT, D_MODEL), lambda i, rt, be: (be[i], 0, 0))],
        out_specs=pl.BlockSpec((EXPERT_BLOCK, D_MODEL), lambda i, rt, be: (i, 0)),
        scratch_shapes=[pltpu.VMEM((2, EXPERT_BLOCK, D_MODEL), F32), pltpu.SemaphoreType.DMA((2,))],
    )
    return pl.pallas_call(
        _expert_kernel,
        grid_spec=grid_spec,
        out_shape=jax.ShapeDtypeStruct((n_rows, D_MODEL), F32),
        compiler_params=pltpu.CompilerParams(dimension_semantics=("arbitrary",),
                                             vmem_limit_bytes=VMEM_LIMIT),
        name="moe_experts",
    )(row_tok, blk_e, xn, w13, w2)


def _combine_kernel(dest_ref, dest_next_ref, h_ref, rw_ref, p_ref, wpg_ref, bpg_ref,
                    wple_ref, yb_ref, out_ref, y0_ref, y1_ref, sem, *, tc):
    i = pl.program_id(0)
    last = pl.num_programs(0) - 1
    slot = i % 2

    def gather(tile_dest_ref, to_slot):
        for r in range(tc):
            _row_copy(yb_ref, tile_dest_ref[0, 0, 2 * r], y0_ref.at[to_slot], r, sem.at[to_slot]).start()
            _row_copy(yb_ref, tile_dest_ref[0, 0, 2 * r + 1], y1_ref.at[to_slot], r,
                      sem.at[to_slot]).start()

    def wait(of_slot):
        _rows_wait(yb_ref, y0_ref.at[of_slot], tc, sem.at[of_slot])
        _rows_wait(yb_ref, y1_ref.at[of_slot], tc, sem.at[of_slot])

    @pl.when(i == 0)
    def _():
        gather(dest_ref, 0)

    gather(dest_next_ref, 1 - slot)
    wait(slot)

    rw = rw_ref[...]
    h2 = h_ref[...] + rw[:, 0:1] * y0_ref[slot] + rw[:, 1:2] * y1_ref[slot]
    gate = jax.nn.sigmoid(_dot(h2.astype(BF16), wpg_ref[...]) + bpg_ref[...])
    out_ref[...] = h2 + gate * _dot(p_ref[...].astype(BF16), wple_ref[...])

    @pl.when(i == last)
    def _():
        wait(1 - slot)


def _combine(dest, h1, rw, p, lp, yb, tc):
    t = h1.shape[0]
    n_tiles = t // tc
    const = lambda *shape: pl.BlockSpec(shape, lambda i: (0,) * len(shape))
    rows = lambda w: pl.BlockSpec((tc, w), lambda i: (i, 0))
    grid_spec = pl.GridSpec(
        grid=(n_tiles,),
        in_specs=[pl.BlockSpec((1, 1, 2 * tc), lambda i: (i, 0, 0), memory_space=pltpu.SMEM),
                  pl.BlockSpec((1, 1, 2 * tc), lambda i: (jnp.minimum(i + 1, n_tiles - 1), 0, 0),
                               memory_space=pltpu.SMEM),
                  rows(D_MODEL), rows(LANES), rows(PLE_DIM), const(D_MODEL, D_MODEL), const(1, D_MODEL),
                  const(PLE_DIM, D_MODEL), pl.BlockSpec(memory_space=pl.ANY)],
        out_specs=rows(D_MODEL),
        scratch_shapes=[pltpu.VMEM((2, tc, D_MODEL), F32), pltpu.VMEM((2, tc, D_MODEL), F32),
                        pltpu.SemaphoreType.DMA((2,))],
    )
    return pl.pallas_call(
        functools.partial(_combine_kernel, tc=tc),
        grid_spec=grid_spec,
        out_shape=jax.ShapeDtypeStruct((t, D_MODEL), F32),
        compiler_params=pltpu.CompilerParams(dimension_semantics=("arbitrary",),
                                             vmem_limit_bytes=VMEM_LIMIT),
        name="moe_combine_ple",
    )(dest, dest, h1, rw, p, lp["wpg"], lp["bpg"], lp["wple"], yb)


def _rope_lane_tables(seq):
    inv = 1.0 / (ROPE_THETA ** (jnp.arange(0, ROPE_DIM, 2, dtype=F32) / ROPE_DIM))
    ang = jnp.arange(seq, dtype=F32)[:, None] * inv[None, :]
    cos, sin = jnp.cos(ang), jnp.sin(ang)
    half = ROPE_DIM // 2
    cos128 = jnp.concatenate([jnp.ones((seq, NOPE_DIM), F32), cos, cos,
                              jnp.zeros((seq, LANES - QK_DIM), F32)], axis=1)
    sin128 = jnp.concatenate([jnp.zeros((seq, NOPE_DIM), F32), -sin, sin,
                              jnp.zeros((seq, LANES - QK_DIM), F32)], axis=1)
    del half
    return cos128, sin128


def _swap_rope_halves(w):
    half = ROPE_DIM // 2
    return jnp.concatenate([w[..., half:], w[..., :half]], axis=-1)


def _pad_heads(w, width):
    return jnp.pad(w, ((0, 0), (0, 0), (0, HEAD_PAD - width))).reshape(w.shape[0], QKV_PAD)


def _rope_gain_rows(g):
    pad = jnp.zeros((LANES - QK_DIM,), F32)
    ga = jnp.concatenate([g, pad])
    gb = jnp.concatenate([jnp.zeros((NOPE_DIM,), F32), _swap_rope_halves(g[NOPE_DIM:]), pad])
    return ga, gb


def _ssm_tables(lam_re, lam_im, log_dt, b_re, b_im, c_re, c_im, d_skip, n_steps):
    hp = lax.Precision.HIGHEST
    lr = jnp.minimum(lam_re, -1e-4)
    li = lam_im
    dt = jnp.exp(log_dt)
    mag = jnp.exp(lr * dt)
    ab_re, ab_im = mag * jnp.cos(li * dt), mag * jnp.sin(li * dt)
    den = lr * lr + li * li
    nr, ni = ab_re - 1.0, ab_im
    fr = (nr * lr + ni * li) / den
    fi = (ni * lr - nr * li) / den
    bb_re = fr[..., None] * b_re - fi[..., None] * b_im
    bb_im = fr[..., None] * b_im + fi[..., None] * b_re

    steps = jnp.arange(SSM_CHUNK + 1, dtype=F32)[:, None, None]
    pmag = jnp.exp(lr * dt * steps)
    pw_re, pw_im = pmag * jnp.cos(li * dt * steps), pmag * jnp.sin(li * dt * steps)

    w_re = pw_re[:SSM_CHUNK, :, :, None] * bb_re - pw_im[:SSM_CHUNK, :, :, None] * bb_im
    w_im = pw_re[:SSM_CHUNK, :, :, None] * bb_im + pw_im[:SSM_CHUNK, :, :, None] * bb_re
    taps = (jnp.einsum("gcn,tgnd->tgcd", c_re, w_re, precision=hp)
            - jnp.einsum("gcn,tgnd->tgcd", c_im, w_im, precision=hp))
    taps = taps.at[0].add(d_skip[:, :, None] * jnp.eye(SSM_CH, dtype=F32)[None])
    t_in = jnp.arange(SSM_CHUNK)[:, None]
    t_out = jnp.arange(SSM_CHUNK)[None, :]
    lag = t_out - t_in
    m = jnp.where((lag >= 0)[:, :, None, None, None], taps[jnp.clip(lag, 0)], 0.0)
    m = m.transpose(2, 0, 4, 1, 3)

    p_re = w_re[::-1].transpose(1, 0, 3, 2)
    p_im = w_im[::-1].transpose(1, 0, 3, 2)

    up_re, up_im = pw_re[1:], pw_im[1:]
    q_re = (c_re[None] * up_re[:, :, None, :] - c_im[None] * up_im[:, :, None, :])
    q_im = -(c_re[None] * up_im[:, :, None, :] + c_im[None] * up_re[:, :, None, :])
    q_re = q_re.transpose(1, 3, 0, 2)
    q_im = q_im.transpose(1, 3, 0, 2)

    are, aim = [pw_re[SSM_CHUNK]], [pw_im[SSM_CHUNK]]
    for _ in range(n_steps - 1):
        r, im = are[-1], aim[-1]
        are.append(r * r - im * im)
        aim.append(2.0 * r * im)
    are, aim = jnp.stack(are, axis=1), jnp.stack(aim, axis=1)

    nbu = SSM_GROUPS // SSM_BUNDLE
    eye = jnp.eye(SSM_BUNDLE, dtype=F32)
    bundle = lambda x: x.reshape((nbu, SSM_BUNDLE) + x.shape[1:])
    m_b = (bundle(m).transpose(0, 2, 1, 3, 4, 5)[:, :, :, :, :, None, :]
           * eye[None, None, :, None, None, :, None]).reshape(nbu, SSM_COLS, SSM_COLS)

    def state_in(x):
        x = bundle(x).transpose(0, 2, 1, 3, 4)[:, :, :, :, None, :] * eye[None, None, :, None, :, None]
        return x.reshape(nbu, SSM_COLS, SSM_SLANES)

    def state_out(x):
        x = bundle(x)[:, :, :, :, None, :] * eye[None, :, None, None, :, None]
        return x.reshape(nbu, SSM_SLANES, SSM_COLS)

    def lanes(x):
        return bundle(x).transpose(0, 2, 1, 3).reshape(nbu, x.shape[1], SSM_SLANES)

    return dict(ssm_m=m_b.astype(BF16), ssm_pre=state_in(p_re).astype(BF16),
                ssm_pim=state_in(p_im).astype(BF16), ssm_qre=state_out(q_re).astype(BF16),
                ssm_qim=state_out(q_im).astype(BF16), ssm_are=lanes(are), ssm_aim=lanes(aim))


def _layer_params(i, tm, n_scan_steps, ln1, w_in, g_cq, g_ckv, w_uq, w_ukv, g_qh, g_kh, sg_ln_g,
                  sg_ln_b, sg_w, sg_b, lam_re, lam_im, log_dt, ssm_b_re, ssm_b_im, ssm_c_re, ssm_c_im,
                  ssm_d, w_glu, b_glu, g_mix, w_out, ln2, w_rg, b_rg, w_rexp, b_rexp, w1, w3, w2, w_pg,
                  b_pg, w_ple):
    lp = {}
    wi = w_in[i]
    off_kr = Q_RANK + KV_RANK
    off_sg = off_kr + ROPE_DIM
    kr = wi[:, off_kr:off_sg]
    zpad = lambda n: jnp.zeros((D_MODEL, n), F32)
    kra = jnp.concatenate([zpad(NOPE_DIM), kr, zpad(LANES - QK_DIM)], axis=1)
    krb = jnp.concatenate([zpad(NOPE_DIM), _swap_rope_halves(kr), zpad(LANES - QK_DIM)], axis=1)
    lp["win"] = jnp.concatenate([wi[:, :off_kr], kra, krb, wi[:, off_sg:]], axis=1).astype(BF16)
    lp["ln1"] = ln1[i][None]
    lp["gcq"] = g_cq[i][None]
    lp["gckv"] = g_ckv[i][None]

    wq = w_uq[i].reshape(Q_RANK, ATTN_HEADS, QK_DIM)
    lp["wqa"] = _pad_heads(wq, QK_DIM).astype(BF16)
    wq_sw = jnp.concatenate([jnp.zeros((Q_RANK, ATTN_HEADS, NOPE_DIM), F32),
                             _swap_rope_halves(wq[..., NOPE_DIM:])], axis=-1)
    lp["wqb"] = _pad_heads(wq_sw, QK_DIM).astype(BF16)
    wkv = w_ukv[i].reshape(KV_RANK, ATTN_HEADS, NOPE_DIM + V_DIM)
    lp["wk"] = _pad_heads(wkv[..., :NOPE_DIM], NOPE_DIM).astype(BF16)
    lp["wv"] = _pad_heads(wkv[..., NOPE_DIM:], V_DIM).astype(BF16)
    gq_a, gq_b = _rope_gain_rows(g_qh[i])
    gk_a, gk_b = _rope_gain_rows(g_kh[i])
    lp["gqk"] = jnp.concatenate([jnp.stack([gq_a, gq_b, gk_a, gk_b]), jnp.zeros((4, LANES), F32)])
    lp["vones"] = jnp.tile((jnp.arange(HEAD_PAD) == V_DIM).astype(F32), ATTN_HEADS)[None]

    lp["sg_lng"] = sg_ln_g[i][None]
    lp["sg_lnb"] = sg_ln_b[i][None]
    lp["sg_w"] = sg_w[i]
    lp["sg_bias"] = jnp.repeat(sg_b[i].T, SG_CH, axis=1)
    grp = jnp.arange(SG_WIDTH) // SG_CH
    lp["mmean"] = ((grp[:, None] == grp[None, :]).astype(F32) / SG_CH).astype(BF16)

    lp.update(_ssm_tables(lam_re[i], lam_im[i], log_dt[i], ssm_b_re[i], ssm_b_im[i], ssm_c_re[i],
                          ssm_c_im[i], ssm_d[i], n_scan_steps))

    gm = g_mix[i]
    aw = ATTN_HEADS * V_DIM
    lp["gm_a"] = _pad_heads(gm[:aw].reshape(1, ATTN_HEADS, V_DIM), V_DIM)
    lp["gm_g"] = gm[aw:aw + SG_WIDTH][None]
    lp["gm_s"] = gm[aw + SG_WIDTH:][None]
    wo = w_out[i]
    woa = wo[:aw].reshape(ATTN_HEADS, V_DIM, D_MODEL)
    lp["wo_a"] = jnp.pad(woa, ((0, 0), (0, HEAD_PAD - V_DIM), (0, 0))).reshape(QKV_PAD, D_MODEL).astype(BF16)
    lp["wo_g"] = wo[aw:aw + SG_WIDTH].astype(BF16)
    lp["wo_s"] = wo[aw + SG_WIDTH:].astype(BF16)
    lp["wglu"] = w_glu[i].astype(BF16)
    lp["bglu"] = b_glu[i][None]
    lp["ln2"] = ln2[i][None]
    wr = jnp.concatenate([w_rg[i], w_rexp[i], jnp.zeros((D_MODEL, LANES - MOE_GROUPS - N_EXPERTS), F32)],
                         axis=1)
    lp["wr_hi"], lp["wr_lo"] = _split_bf16(wr)
    lp["br"] = jnp.concatenate([b_rg[i], b_rexp[i], jnp.zeros((LANES - MOE_GROUPS - N_EXPERTS,), F32)])[None]
    lp["ltri"] = (jnp.arange(tm)[None, :] < jnp.arange(tm)[:, None]).astype(BF16)

    lp["w13"] = jnp.concatenate([w1[i], w3[i]], axis=-1).astype(BF16)
    lp["w2"] = w2[i].astype(BF16)
    lp["wpg"] = w_pg[i].astype(BF16)
    lp["bpg"] = b_pg[i][None]
    lp["wple"] = w_ple[i].astype(BF16)
    return lp


def kernel(x, p, ln1, w_in, g_cq, g_ckv, w_uq, w_ukv, g_qh, g_kh, sg_ln_g, sg_ln_b, sg_w, sg_b, lam_re, lam_im, log_dt, ssm_b_re, ssm_b_im, ssm_c_re, ssm_c_im, ssm_d, w_glu, b_glu, g_mix, w_out, ln2, w_rg, b_rg, w_rexp, b_rexp, w1, w3, w2, w_pg, b_pg, w_ple):
    batch, seq, _ = x.shape
    depth = p.shape[0]
    t = batch * seq
    tm = min(512, seq)
    tq = min(512, seq)
    td = min(1024, seq)
    tc = min(256, seq)
    assert seq % tm == 0 and seq % SSM_CHUNK == 0 and tm % SG_CHUNK == 0
    rows_per_batch = seq // SSM_CHUNK
    n_scan_steps = max(1, (rows_per_batch - 1).bit_length())
    cr = t // SSM_CHUNK
    n_bundles = SSM_GROUPS // SSM_BUNDLE
    bundle_w = SSM_BUNDLE * SSM_CH
    tk2 = 2 * t
    n_blocks = -(-(tk2 + N_EXPERTS * (EXPERT_BLOCK - 1)) // EXPERT_BLOCK)
    n_rows = n_blocks * EXPERT_BLOCK

    cos128, sin128 = _rope_lane_tables(seq)
    h = x.reshape(t, D_MODEL)
    weights = (ln1, w_in, g_cq, g_ckv, w_uq, w_ukv, g_qh, g_kh, sg_ln_g, sg_ln_b, sg_w, sg_b, lam_re,
               lam_im, log_dt, ssm_b_re, ssm_b_im, ssm_c_re, ssm_c_im, ssm_d, w_glu, b_glu, g_mix, w_out,
               ln2, w_rg, b_rg, w_rexp, b_rexp, w1, w3, w2, w_pg, b_pg, w_ple)
    for i in range(depth):
        lp = _layer_params(i, tm, n_scan_steps, *weights)
        q, k, v, sg, xs_ssm = _mixer_in(h, lp, cos128, sin128, tm, seq // tm)
        a = _attention(q, k, v, batch, seq, tq)

        xp = xs_ssm.reshape(cr, SSM_CHUNK, n_bundles, bundle_w).transpose(2, 0, 1, 3)
        yp = _ssm(xp.reshape(n_bundles, cr, SSM_COLS), lp, rows_per_batch)
        y = yp.reshape(n_bundles, cr, SSM_CHUNK, bundle_w).transpose(1, 2, 0, 3).reshape(t, SSM_WIDTH)

        h1, xn, ri, rw, cnt = _mixer_out(a, sg, y, h, lp, tm)

        counts = cnt[0, :N_EXPERTS].astype(jnp.int32)
        pcounts = (counts + EXPERT_BLOCK - 1) // EXPERT_BLOCK * EXPERT_BLOCK
        pend = jnp.cumsum(pcounts)
        pstart = (pend - pcounts).astype(jnp.int32)
        blk_first_row = jnp.arange(n_blocks, dtype=jnp.int32) * EXPERT_BLOCK
        blk_e = jnp.minimum(jnp.sum((pend[None, :] <= blk_first_row[:, None]).astype(jnp.int32), axis=1),
                            N_EXPERTS - 1)
        dest = pstart[ri[:, 0:2]] + ri[:, 2:4]
        row_tok = _slot_map(dest.reshape(t // td, 1, 2 * td), n_rows, td)
        yb = _experts(row_tok, blk_e, xn, lp["w13"], lp["w2"])
        h = _combine(dest.reshape(t // tc, 1, 2 * tc), h1, rw, p[i].reshape(t, PLE_DIM), lp, yb, tc)
    return h.reshape(batch, seq, D_MODEL)
```

```python
import functools
import math

import jax
import jax.numpy as jnp
from jax import lax
from jax.experimental import pallas as pl
from jax.experimental.pallas import tpu as pltpu

F32 = jnp.float32
BF16 = jnp.bfloat16

D_MODEL = 1024
ATTN_HEADS = 8
NOPE_DIM = 64
ROPE_DIM = 32
QK_DIM = NOPE_DIM + ROPE_DIM
V_DIM = 64
Q_RANK = 384
KV_RANK = 256
SG_GROUPS = 4
SG_CH = 64
SG_WIDTH = SG_GROUPS * SG_CH
SG_CHUNK = 128
SSM_GROUPS = 16
SSM_CH = 16
SSM_WIDTH = SSM_GROUPS * SSM_CH
SSM_STATE = 64
MOE_GROUPS = 4
EXPERTS_PER_GROUP = 8
N_EXPERTS = MOE_GROUPS * EXPERTS_PER_GROUP
D_EXPERT = 256
PLE_DIM = 256
ROPE_THETA = 10000.0
EPS = 1e-6

LANES = 128
HEAD_PAD = LANES
QKV_PAD = ATTN_HEADS * HEAD_PAD
ATTN_HEADS_PER_STEP = 2
ATTN_BLOCKS_PER_ITER = 4
SSM_CHUNK = 16
SSM_BUNDLE = 4
SSM_COLS = SSM_BUNDLE * SSM_CHUNK * SSM_CH
SSM_SLANES = SSM_BUNDLE * SSM_STATE
EXPERT_BLOCK = 256
SCALAR_LOOP_UNROLL = 8
VMEM_LIMIT = 56 * 1024 * 1024

C_Q = 0
C_KV = C_Q + Q_RANK
C_KRA = C_KV + KV_RANK
C_KRB = C_KRA + LANES
C_SG = C_KRB + LANES
C_SSM = C_SG + 2 * SG_WIDTH
C_END = C_SSM + SSM_WIDTH


def _dot(a, b):
    return jnp.dot(a, b, preferred_element_type=F32)


def _split_bf16(x):
    hi = x.astype(BF16)
    lo = (x - hi.astype(F32)).astype(BF16)
    return hi, lo


def _rms(x, g, width=None):
    width = x.shape[-1] if width is None else width
    ms = jnp.sum(x * x, axis=-1, keepdims=True) * (1.0 / width)
    return x * lax.rsqrt(ms + EPS) * g


def _mixer_in_kernel(h_ref, ln1_ref, win_ref, gcq_ref, gckv_ref, wqa_ref, wqb_ref, wk_ref, wv_ref,
                     cos_ref, sin_ref, gqk_ref, vones_ref, lng_ref, lnb_ref, sgw_ref, sgb_ref,
                     mmean_ref, q_out, k_out, v_out, sg_out, ssm_out, *, tm):
    xn = _rms(h_ref[...], ln1_ref[...])
    z = _dot(xn.astype(BF16), win_ref[...])

    cos = cos_ref[...]
    sin = sin_ref[...]
    gqk = gqk_ref[...]
    gqa = cos * gqk[0:1, :]
    gqb = sin * gqk[1:2, :]
    gka = cos * gqk[2:3, :]
    gkb = sin * gqk[3:4, :]
    scale = QK_DIM ** -0.5 * math.log2(math.e)

    qn = _rms(z[:, C_Q:C_KV], gcq_ref[...]).astype(BF16)
    qa = _dot(qn, wqa_ref[...])
    qb = _dot(qn, wqb_ref[...])
    for hd in range(ATTN_HEADS):
        sl = slice(hd * HEAD_PAD, (hd + 1) * HEAD_PAD)
        a = qa[:, sl]
        r = lax.rsqrt(jnp.sum(a * a, axis=-1, keepdims=True) * (1.0 / QK_DIM) + EPS)
        q_out[:, sl] = ((a * gqa + qb[:, sl] * gqb) * (r * scale)).astype(BF16)

    kvn = _rms(z[:, C_KV:C_KRA], gckv_ref[...]).astype(BF16)
    ka = _dot(kvn, wk_ref[...])
    v_out[...] = (_dot(kvn, wv_ref[...]) + vones_ref[...]).astype(BF16)
    kra = z[:, C_KRA:C_KRB]
    krb = z[:, C_KRB:C_SG]
    ss_rope = jnp.sum(kra * kra, axis=-1, keepdims=True)
    krot = kra * gka + krb * gkb
    for hd in range(ATTN_HEADS):
        sl = slice(hd * HEAD_PAD, (hd + 1) * HEAD_PAD)
        a = ka[:, sl]
        r = lax.rsqrt((jnp.sum(a * a, axis=-1, keepdims=True) + ss_rope) * (1.0 / QK_DIM) + EPS)
        k_out[:, sl] = ((a * gka + krot) * r).astype(BF16)

    zg = jax.nn.gelu(z[:, C_SG:C_SSM])
    u = zg[:, :SG_WIDTH]
    vv = zg[:, SG_WIDTH:]
    mmean = mmean_ref[...]
    vh, vl = _split_bf16(vv)
    mu = _dot(vh, mmean) + _dot(vl, mmean)
    xc = vv - mu
    sh, sl_ = _split_bf16(xc * xc)
    var = _dot(sh, mmean) + _dot(sl_, mmean)
    vn = (xc * lax.rsqrt(var + EPS) * lng_ref[...] + lnb_ref[...]).astype(BF16)
    tri = (lax.broadcasted_iota(jnp.int32, (SG_CHUNK, SG_CHUNK), 1)
           <= lax.broadcasted_iota(jnp.int32, (SG_CHUNK, SG_CHUNK), 0))
    lane_grp = lax.broadcasted_iota(jnp.int32, (SG_CHUNK, SG_WIDTH), 1) // SG_CH
    ws = [jnp.where(tri, sgw_ref[g], 0.0).astype(BF16) for g in range(SG_GROUPS)]
    for c in range(tm // SG_CHUNK):
        rows = slice(c * SG_CHUNK, (c + 1) * SG_CHUNK)
        vc = vn[rows, :]
        mixed = sgb_ref[...]
        for g in range(SG_GROUPS):
            mixed = mixed + jnp.where(lane_grp == g, _dot(ws[g], vc), 0.0)
        sg_out[rows, :] = u[rows, :] * mixed

    ssm_out[...] = z[:, C_SSM:C_END].astype(BF16)


def _mixer_in(h, lp, cos128, sin128, tm, n_pos_tiles):
    t = h.shape[0]
    const = lambda *shape: pl.BlockSpec(shape, lambda i: (0,) * len(shape))
    rows = lambda w: pl.BlockSpec((tm, w), lambda i: (i, 0))
    pos = pl.BlockSpec((tm, LANES), lambda i: (i % n_pos_tiles, 0))
    return pl.pallas_call(
        functools.partial(_mixer_in_kernel, tm=tm),
        grid=(t // tm,),
        in_specs=[rows(D_MODEL), const(1, D_MODEL), const(D_MODEL, C_END), const(1, Q_RANK),
                  const(1, KV_RANK), const(Q_RANK, QKV_PAD), const(Q_RANK, QKV_PAD),
                  const(KV_RANK, QKV_PAD), const(KV_RANK, QKV_PAD), pos, pos, const(8, LANES),
                  const(1, QKV_PAD), const(1, SG_WIDTH), const(1, SG_WIDTH),
                  const(SG_GROUPS, SG_CHUNK, SG_CHUNK), const(SG_CHUNK, SG_WIDTH),
                  const(SG_WIDTH, SG_WIDTH)],
        out_specs=[rows(QKV_PAD), rows(QKV_PAD), rows(QKV_PAD), rows(SG_WIDTH), rows(SSM_WIDTH)],
        out_shape=[jax.ShapeDtypeStruct((t, QKV_PAD), BF16), jax.ShapeDtypeStruct((t, QKV_PAD), BF16),
                   jax.ShapeDtypeStruct((t, QKV_PAD), BF16), jax.ShapeDtypeStruct((t, SG_WIDTH), F32),
                   jax.ShapeDtypeStruct((t, SSM_WIDTH), BF16)],
        compiler_params=pltpu.CompilerParams(dimension_semantics=("parallel",),
                                             vmem_limit_bytes=VMEM_LIMIT),
        name="mixer_in",
    )(h, lp["ln1"], lp["win"], lp["gcq"], lp["gckv"], lp["wqa"], lp["wqb"], lp["wk"], lp["wv"],
      cos128, sin128, lp["gqk"], lp["vones"], lp["sg_lng"], lp["sg_lnb"], lp["sg_w"], lp["sg_bias"],
      lp["mmean"])


def _attn_kernel(q_ref, k_ref, v_ref, o_ref, s_even, s_odd, m_ref, acc_ref, *, tq):
    i = pl.program_id(2)
    heads = [slice(hh * HEAD_PAD, (hh + 1) * HEAD_PAD) for hh in range(ATTN_HEADS_PER_STEP)]

    def block_rows(j):
        return pl.ds(pl.multiple_of(j * tq, tq), tq)

    def scores(j, s_ref):
        for hh, sl in enumerate(heads):
            s_ref[hh] = lax.dot_general(q_ref[:, sl], k_ref[block_rows(j), sl],
                                        (((1,), (1,)), ((), ())), preferred_element_type=F32)

    def accumulate(j, s_ref, masked):
        for hh, sl in enumerate(heads):
            s = s_ref[hh]
            if masked:
                keep = (lax.broadcasted_iota(jnp.int32, (tq, tq), 1)
                        <= lax.broadcasted_iota(jnp.int32, (tq, tq), 0))
                s = jnp.where(keep, s, -jnp.inf)
            m = m_ref[hh]
            m_new = jnp.maximum(m, jnp.max(s, axis=-1, keepdims=True))
            alpha = jnp.exp2(m - m_new)
            p = jnp.exp2(s - m_new).astype(BF16)
            m_ref[hh] = m_new
            acc_ref[hh] = alpha * acc_ref[hh] + _dot(p, v_ref[block_rows(j), sl])

    m_ref[...] = jnp.full(m_ref.shape, -jnp.inf, F32)
    acc_ref[...] = jnp.zeros(acc_ref.shape, F32)
    scores(0, s_even)

    def pipelined(t, _):
        j = ATTN_BLOCKS_PER_ITER * t
        for u in range(0, ATTN_BLOCKS_PER_ITER, 2):
            scores(j + u + 1, s_odd)
            accumulate(j + u, s_even, False)
            scores(j + u + 2, s_even)
            accumulate(j + u + 1, s_odd, False)
        return 0

    n_pipelined = i // ATTN_BLOCKS_PER_ITER
    lax.fori_loop(0, n_pipelined, pipelined, 0)

    def leftover(j, _):
        accumulate(j, s_even, False)
        scores(j + 1, s_even)
        return 0

    lax.fori_loop(n_pipelined * ATTN_BLOCKS_PER_ITER, i, leftover, 0)
    accumulate(i, s_even, True)

    lane = lax.broadcasted_iota(jnp.int32, (tq, HEAD_PAD), 1)
    for hh, sl in enumerate(heads):
        acc = acc_ref[hh]
        out = acc / acc[:, V_DIM:V_DIM + 1]
        o_ref[:, sl] = jnp.where(lane < V_DIM, out, 0.0).astype(BF16)


def _attention(q, k, v, batch, seq, tq):
    t = q.shape[0]
    nq = seq // tq
    width = ATTN_HEADS_PER_STEP * HEAD_PAD
    return pl.pallas_call(
        functools.partial(_attn_kernel, tq=tq),
        grid=(batch, ATTN_HEADS // ATTN_HEADS_PER_STEP, nq),
        in_specs=[pl.BlockSpec((tq, width), lambda b, h, i: (b * nq + i, h)),
                  pl.BlockSpec((seq, width), lambda b, h, i: (b, h)),
                  pl.BlockSpec((seq, width), lambda b, h, i: (b, h))],
        out_specs=pl.BlockSpec((tq, width), lambda b, h, i: (b * nq + i, h)),
        out_shape=jax.ShapeDtypeStruct((t, QKV_PAD), BF16),
        scratch_shapes=[pltpu.VMEM((ATTN_HEADS_PER_STEP, tq, tq), F32),
                        pltpu.VMEM((ATTN_HEADS_PER_STEP, tq, tq), F32),
                        pltpu.VMEM((ATTN_HEADS_PER_STEP, tq, 1), F32),
                        pltpu.VMEM((ATTN_HEADS_PER_STEP, tq, HEAD_PAD), F32)],
        compiler_params=pltpu.CompilerParams(
            dimension_semantics=("parallel", "parallel", "arbitrary"), vmem_limit_bytes=VMEM_LIMIT),
        name="mla_attention",
    )(q, k, v)


def _ssm_kernel(x_ref, m_ref, pre_ref, pim_ref, qre_ref, qim_ref, are_ref, aim_ref, y_ref, *,
                rows_per_batch, n_steps):
    x = x_ref[0]
    cr = x.shape[0]
    y_local = _dot(x, m_ref[0])
    sre = _dot(x, pre_ref[0])
    sim = _dot(x, pim_ref[0])
    row = lax.broadcasted_iota(jnp.int32, (cr, SSM_SLANES), 0) % rows_per_batch
    are = are_ref[0]
    aim = aim_ref[0]
    for k in range(n_steps):
        sh = 1 << k
        ok = row >= sh
        pr = jnp.where(ok, pltpu.roll(sre, sh, 0), 0.0)
        pi = jnp.where(ok, pltpu.roll(sim, sh, 0), 0.0)
        ar = are[k:k + 1, :]
        ai = aim[k:k + 1, :]
        sre, sim = sre + ar * pr - ai * pi, sim + ar * pi + ai * pr
    ok = row >= 1
    hre = jnp.where(ok, pltpu.roll(sre, 1, 0), 0.0).astype(BF16)
    him = jnp.where(ok, pltpu.roll(sim, 1, 0), 0.0).astype(BF16)
    y_ref[0] = y_local + _dot(hre, qre_ref[0]) + _dot(him, qim_ref[0])


def _ssm(xp, lp, rows_per_batch):
    n_bundles, cr, _ = xp.shape
    n_steps = lp["ssm_are"].shape[1]
    blk = lambda *shape: pl.BlockSpec((1,) + shape, lambda p: (p,) + (0,) * len(shape))
    return pl.pallas_call(
        functools.partial(_ssm_kernel, rows_per_batch=rows_per_batch, n_steps=n_steps),
        grid=(n_bundles,),
        in_specs=[blk(cr, SSM_COLS), blk(SSM_COLS, SSM_COLS), blk(SSM_COLS, SSM_SLANES),
                  blk(SSM_COLS, SSM_SLANES), blk(SSM_SLANES, SSM_COLS), blk(SSM_SLANES, SSM_COLS),
                  blk(n_steps, SSM_SLANES), blk(n_steps, SSM_SLANES)],
        out_specs=blk(cr, SSM_COLS),
        out_shape=jax.ShapeDtypeStruct((n_bundles, cr, SSM_COLS), F32),
        compiler_params=pltpu.CompilerParams(dimension_semantics=("parallel",),
                                             vmem_limit_bytes=VMEM_LIMIT),
        name="s5_mixer",
    )(xp, lp["ssm_m"], lp["ssm_pre"], lp["ssm_pim"], lp["ssm_qre"], lp["ssm_qim"], lp["ssm_are"],
      lp["ssm_aim"])


def _mixer_out_kernel(a_ref, sg_ref, y_ref, h_ref, gma_ref, gmg_ref, gms_ref, woa_ref, wog_ref,
                      wos_ref, wglu_ref, bglu_ref, ln2_ref, wrh_ref, wrl_ref, br_ref, ltri_ref,
                      h1_out, xn_out, ri_out, rw_out, cnt_out, run_ref):
    i = pl.program_id(0)

    @pl.when(i == 0)
    def _():
        run_ref[...] = jnp.zeros_like(run_ref)

    g = jax.nn.gelu(y_ref[...])
    glu = g * jax.nn.sigmoid(_dot(g.astype(BF16), wglu_ref[...]) + bglu_ref[...])
    a_n = _rms(a_ref[...].astype(F32), gma_ref[...], ATTN_HEADS * V_DIM)
    g_n = _rms(sg_ref[...], gmg_ref[...])
    s_n = _rms(glu, gms_ref[...])
    h1 = (h_ref[...] + _dot(a_n.astype(BF16), woa_ref[...]) + _dot(g_n.astype(BF16), wog_ref[...])
          + _dot(s_n.astype(BF16), wos_ref[...]))
    h1_out[...] = h1
    xn = _rms(h1, ln2_ref[...])
    xn_out[...] = xn

    xh, xl = _split_bf16(xn)
    logit = _dot(xh, wrh_ref[...]) + _dot(xl, wrh_ref[...]) + _dot(xh, wrl_ref[...]) + br_ref[...]
    shape = logit.shape
    lane = lax.broadcasted_iota(jnp.int32, shape, 1)
    neg = -jnp.inf
    is_grp = lane < MOE_GROUPS
    gl = jnp.where(is_grp, logit, neg)
    gmax = jnp.max(gl, axis=-1, keepdims=True)
    gsel = jnp.min(jnp.where(gl == gmax, lane, LANES), axis=-1, keepdims=True)
    gsum = jnp.sum(jnp.where(is_grp, jnp.exp(logit - gmax), 0.0), axis=-1, keepdims=True)
    g_w = 1.0 / gsum
    lo = MOE_GROUPS + gsel * EXPERTS_PER_GROUP
    in_grp = (lane >= lo) & (lane < lo + EXPERTS_PER_GROUP)
    el = jnp.where(in_grp, logit, neg)
    m1 = jnp.max(el, axis=-1, keepdims=True)
    i1 = jnp.min(jnp.where(el == m1, lane, LANES), axis=-1, keepdims=True)
    el2 = jnp.where(lane == i1, neg, el)
    m2 = jnp.max(el2, axis=-1, keepdims=True)
    i2 = jnp.min(jnp.where(el2 == m2, lane, LANES), axis=-1, keepdims=True)
    e2 = jnp.exp(m2 - m1)
    w1 = g_w / (1.0 + e2)
    w2 = g_w * e2 / (1.0 + e2)
    e_a = i1 - MOE_GROUPS
    e_b = i2 - MOE_GROUPS

    hit_a = lane == e_a
    hit_b = lane == e_b
    onehot = jnp.where(hit_a | hit_b, 1.0, 0.0).astype(BF16)
    before = _dot(ltri_ref[...], onehot) + run_ref[...]
    c_a = jnp.sum(jnp.where(hit_a, before, 0.0), axis=-1, keepdims=True).astype(jnp.int32)
    c_b = jnp.sum(jnp.where(hit_b, before, 0.0), axis=-1, keepdims=True).astype(jnp.int32)
    run_ref[...] = run_ref[...] + jnp.sum(onehot.astype(F32), axis=0, keepdims=True)
    cnt_out[...] = jnp.broadcast_to(run_ref[...], cnt_out.shape)

    ri_out[...] = jnp.where(lane == 0, e_a, jnp.where(lane == 1, e_b, jnp.where(
        lane == 2, c_a, jnp.where(lane == 3, c_b, 0))))
    rw_out[...] = jnp.where(lane == 0, w1, jnp.where(lane == 1, w2, 0.0))


def _mixer_out(a, sg, y, h, lp, tm):
    t = h.shape[0]
    const = lambda *shape: pl.BlockSpec(shape, lambda i: (0,) * len(shape))
    rows = lambda w: pl.BlockSpec((tm, w), lambda i: (i, 0))
    return pl.pallas_call(
        _mixer_out_kernel,
        grid=(t // tm,),
        in_specs=[rows(QKV_PAD), rows(SG_WIDTH), rows(SSM_WIDTH), rows(D_MODEL), const(1, QKV_PAD),
                  const(1, SG_WIDTH), const(1, SSM_WIDTH), const(QKV_PAD, D_MODEL),
                  const(SG_WIDTH, D_MODEL), const(SSM_WIDTH, D_MODEL), const(SSM_WIDTH, SSM_WIDTH),
                  const(1, SSM_WIDTH), const(1, D_MODEL), const(D_MODEL, LANES), const(D_MODEL, LANES),
                  const(1, LANES), const(tm, tm)],
        out_specs=[rows(D_MODEL), rows(D_MODEL), rows(LANES), rows(LANES), const(8, LANES)],
        out_shape=[jax.ShapeDtypeStruct((t, D_MODEL), F32), jax.ShapeDtypeStruct((t, D_MODEL), F32),
                   jax.ShapeDtypeStruct((t, LANES), jnp.int32), jax.ShapeDtypeStruct((t, LANES), F32),
                   jax.ShapeDtypeStruct((8, LANES), F32)],
        scratch_shapes=[pltpu.VMEM((1, LANES), F32)],
        compiler_params=pltpu.CompilerParams(dimension_semantics=("arbitrary",),
                                             vmem_limit_bytes=VMEM_LIMIT),
        name="mixer_out_router",
    )(a, sg, y, h, lp["gm_a"], lp["gm_g"], lp["gm_s"], lp["wo_a"], lp["wo_g"], lp["wo_s"], lp["wglu"],
      lp["bglu"], lp["ln2"], lp["wr_hi"], lp["wr_lo"], lp["br"], lp["ltri"])


def _row_copy(src_ref, src_row, dst_ref, dst_row, sem):
    return pltpu.make_async_copy(src_ref.at[pl.ds(src_row, 1)], dst_ref.at[pl.ds(dst_row, 1)], sem)


def _rows_wait(src_ref, dst_ref, n_rows, sem):
    pltpu.make_async_copy(src_ref.at[pl.ds(0, n_rows)], dst_ref.at[pl.ds(0, n_rows)], sem).wait()


def _slot_map_kernel(dest_ref, zeros_ref, row_tok_ref, sem, *, td):
    i = pl.program_id(0)

    @pl.when(i == 0)
    def _():
        clear = pltpu.make_async_copy(zeros_ref, row_tok_ref, sem)
        clear.start()
        clear.wait()

    def place(r, _):
        row_tok_ref[dest_ref[0, 0, 2 * r]] = i * td + r
        row_tok_ref[dest_ref[0, 0, 2 * r + 1]] = i * td + r
        return 0

    lax.fori_loop(0, td, place, 0, unroll=SCALAR_LOOP_UNROLL)


def _slot_map(dest, n_rows, td):
    n_tiles = dest.shape[0]
    return pl.pallas_call(
        functools.partial(_slot_map_kernel, td=td),
        grid=(n_tiles,),
        in_specs=[pl.BlockSpec((1, 1, 2 * td), lambda i: (i, 0, 0), memory_space=pltpu.SMEM),
                  pl.BlockSpec(memory_space=pl.ANY)],
        out_specs=pl.BlockSpec(memory_space=pltpu.SMEM),
        out_shape=jax.ShapeDtypeStruct((n_rows,), jnp.int32),
        scratch_shapes=[pltpu.SemaphoreType.DMA(())],
        compiler_params=pltpu.CompilerParams(dimension_semantics=("arbitrary",)),
        name="moe_slot_map",
    )(dest, jnp.zeros((n_rows,), jnp.int32))


def _expert_kernel(row_tok_ref, blk_e_ref, x_ref, w13_a_ref, w2_a_ref, w13_b_ref, w2_b_ref, yb_ref,
                   xbuf_a, xbuf_b, sem):
    del blk_e_ref
    g = pl.program_id(0)
    last = pl.num_programs(0) - 1

    def gather(block, buf, buf_sem):
        base = block * EXPERT_BLOCK
        for r in range(EXPERT_BLOCK):
            _row_copy(x_ref, row_tok_ref[base + r], buf, r, buf_sem).start()

    def mlp(buf, w13_ref, w2_ref):
        hh = _dot(buf[...].astype(BF16), w13_ref[...])
        hb = jax.nn.silu(hh[:, :D_EXPERT]) * hh[:, D_EXPERT:]
        return _dot(hb.astype(BF16), w2_ref[...])

    @pl.when(g == 0)
    def _():
        gather(0, xbuf_a, sem.at[0])

    _rows_wait(x_ref, xbuf_a, EXPERT_BLOCK, sem.at[0])
    gather(2 * g + 1, xbuf_b, sem.at[1])
    yb_ref[:EXPERT_BLOCK, :] = mlp(xbuf_a, w13_a_ref, w2_a_ref)
    _rows_wait(x_ref, xbuf_b, EXPERT_BLOCK, sem.at[1])
    gather(jnp.minimum(2 * g + 2, 2 * last), xbuf_a, sem.at[0])
    yb_ref[EXPERT_BLOCK:, :] = mlp(xbuf_b, w13_b_ref, w2_b_ref)

    @pl.when(g == last)
    def _():
        _rows_wait(x_ref, xbuf_a, EXPERT_BLOCK, sem.at[0])


def _experts(row_tok, blk_e, xn, w13, w2):
    n_rows = row_tok.shape[0]
    w13_spec = lambda half: pl.BlockSpec((None, D_MODEL, 2 * D_EXPERT),
                                         lambda i, rt, be: (be[2 * i + half], 0, 0))
    w2_spec = lambda half: pl.BlockSpec((None, D_EXPERT, D_MODEL),
                                        lambda i, rt, be: (be[2 * i + half], 0, 0))
    grid_spec = pltpu.PrefetchScalarGridSpec(
        num_scalar_prefetch=2,
        grid=(n_rows // (2 * EXPERT_BLOCK),),
        in_specs=[pl.BlockSpec(memory_space=pl.ANY), w13_spec(0), w2_spec(0), w13_spec(1), w2_spec(1)],
        out_specs=pl.BlockSpec((2 * EXPERT_BLOCK, D_MODEL), lambda i, rt, be: (i, 0)),
        scratch_shapes=[pltpu.VMEM((EXPERT_BLOCK, D_MODEL), F32), pltpu.VMEM((EXPERT_BLOCK, D_MODEL), F32),
                        pltpu.SemaphoreType.DMA((2,))],
    )
    return pl.pallas_call(
        _expert_kernel,
        grid_spec=grid_spec,
        out_shape=jax.ShapeDtypeStruct((n_rows, D_MODEL), F32),
        compiler_params=pltpu.CompilerParams(dimension_semantics=("arbitrary",),
                                             vmem_limit_bytes=VMEM_LIMIT),
        name="moe_experts",
    )(row_tok, blk_e, xn, w13, w2, w13, w2)


def _combine_kernel(dest_ref, dest_next_ref, h_ref, rw_ref, p_ref, wpg_ref, bpg_ref,
                    wple_ref, yb_ref, out_ref, y0_ref, y1_ref, sem, *, tc):
    i = pl.program_id(0)
    last = pl.num_programs(0) - 1
    slot = i % 2

    def gather(tile_dest_ref, to_slot):
        for r in range(tc):
            _row_copy(yb_ref, tile_dest_ref[0, 0, 2 * r], y0_ref.at[to_slot], r, sem.at[to_slot]).start()
            _row_copy(yb_ref, tile_dest_ref[0, 0, 2 * r + 1], y1_ref.at[to_slot], r,
                      sem.at[to_slot]).start()

    def wait(of_slot):
        _rows_wait(yb_ref, y0_ref.at[of_slot], tc, sem.at[of_slot])
        _rows_wait(yb_ref, y1_ref.at[of_slot], tc, sem.at[of_slot])

    @pl.when(i == 0)
    def _():
        gather(dest_ref, 0)

    gather(dest_next_ref, 1 - slot)
    wait(slot)

    rw = rw_ref[...]
    h2 = h_ref[...] + rw[:, 0:1] * y0_ref[slot] + rw[:, 1:2] * y1_ref[slot]
    gate = jax.nn.sigmoid(_dot(h2.astype(BF16), wpg_ref[...]) + bpg_ref[...])
    out_ref[...] = h2 + gate * _dot(p_ref[...].astype(BF16), wple_ref[...])

    @pl.when(i == last)
    def _():
        wait(1 - slot)


def _combine(dest, h1, rw, p, lp, yb, tc):
    t = h1.shape[0]
    n_tiles = t // tc
    const = lambda *shape: pl.BlockSpec(shape, lambda i: (0,) * len(shape))
    rows = lambda w: pl.BlockSpec((tc, w), lambda i: (i, 0))
    grid_spec = pl.GridSpec(
        grid=(n_tiles,),
        in_specs=[pl.BlockSpec((1, 1, 2 * tc), lambda i: (i, 0, 0), memory_space=pltpu.SMEM),
                  pl.BlockSpec((1, 1, 2 * tc), lambda i: (jnp.minimum(i + 1, n_tiles - 1), 0, 0),
                               memory_space=pltpu.SMEM),
                  rows(D_MODEL), rows(LANES), rows(PLE_DIM), const(D_MODEL, D_MODEL), const(1, D_MODEL),
                  const(PLE_DIM, D_MODEL), pl.BlockSpec(memory_space=pl.ANY)],
        out_specs=rows(D_MODEL),
        scratch_shapes=[pltpu.VMEM((2, tc, D_MODEL), F32), pltpu.VMEM((2, tc, D_MODEL), F32),
                        pltpu.SemaphoreType.DMA((2,))],
    )
    return pl.pallas_call(
        functools.partial(_combine_kernel, tc=tc),
        grid_spec=grid_spec,
        out_shape=jax.ShapeDtypeStruct((t, D_MODEL), F32),
        compiler_params=pltpu.CompilerParams(dimension_semantics=("arbitrary",),
                                             vmem_limit_bytes=VMEM_LIMIT),
        name="moe_combine_ple",
    )(dest, dest, h1, rw, p, lp["wpg"], lp["bpg"], lp["wple"], yb)


def _rope_lane_tables(seq):
    inv = 1.0 / (ROPE_THETA ** (jnp.arange(0, ROPE_DIM, 2, dtype=F32) / ROPE_DIM))
    ang = jnp.arange(seq, dtype=F32)[:, None] * inv[None, :]
    cos, sin = jnp.cos(ang), jnp.sin(ang)
    half = ROPE_DIM // 2
    cos128 = jnp.concatenate([jnp.ones((seq, NOPE_DIM), F32), cos, cos,
                              jnp.zeros((seq, LANES - QK_DIM), F32)], axis=1)
    sin128 = jnp.concatenate([jnp.zeros((seq, NOPE_DIM), F32), -sin, sin,
                              jnp.zeros((seq, LANES - QK_DIM), F32)], axis=1)
    del half
    return cos128, sin128


def _swap_rope_halves(w):
    half = ROPE_DIM // 2
    return jnp.concatenate([w[..., half:], w[..., :half]], axis=-1)


def _pad_heads(w, width):
    return jnp.pad(w, ((0, 0), (0, 0), (0, HEAD_PAD - width))).reshape(w.shape[0], QKV_PAD)


def _rope_gain_rows(g):
    pad = jnp.zeros((LANES - QK_DIM,), F32)
    ga = jnp.concatenate([g, pad])
    gb = jnp.concatenate([jnp.zeros((NOPE_DIM,), F32), _swap_rope_halves(g[NOPE_DIM:]), pad])
    return ga, gb


def _ssm_tables(lam_re, lam_im, log_dt, b_re, b_im, c_re, c_im, d_skip, n_steps):
    hp = lax.Precision.HIGHEST
    lr = jnp.minimum(lam_re, -1e-4)
    li = lam_im
    dt = jnp.exp(log_dt)
    mag = jnp.exp(lr * dt)
    ab_re, ab_im = mag * jnp.cos(li * dt), mag * jnp.sin(li * dt)
    den = lr * lr + li * li
    nr, ni = ab_re - 1.0, ab_im
    fr = (nr * lr + ni * li) / den
    fi = (ni * lr - nr * li) / den
    bb_re = fr[..., None] * b_re - fi[..., None] * b_im
    bb_im = fr[..., None] * b_im + fi[..., None] * b_re

    steps = jnp.arange(SSM_CHUNK + 1, dtype=F32)[:, None, None]
    pmag = jnp.exp(lr * dt * steps)
    pw_re, pw_im = pmag * jnp.cos(li * dt * steps), pmag * jnp.sin(li * dt * steps)

    w_re = pw_re[:SSM_CHUNK, :, :, None] * bb_re - pw_im[:SSM_CHUNK, :, :, None] * bb_im
    w_im = pw_re[:SSM_CHUNK, :, :, None] * bb_im + pw_im[:SSM_CHUNK, :, :, None] * bb_re
    taps = (jnp.einsum("gcn,tgnd->tgcd", c_re, w_re, precision=hp)
            - jnp.einsum("gcn,tgnd->tgcd", c_im, w_im, precision=hp))
    taps = taps.at[0].add(d_skip[:, :, None] * jnp.eye(SSM_CH, dtype=F32)[None])
    t_in = jnp.arange(SSM_CHUNK)[:, None]
    t_out = jnp.arange(SSM_CHUNK)[None, :]
    lag = t_out - t_in
    m = jnp.where((lag >= 0)[:, :, None, None, None], taps[jnp.clip(lag, 0)], 0.0)
    m = m.transpose(2, 0, 4, 1, 3)

    p_re = w_re[::-1].transpose(1, 0, 3, 2)
    p_im = w_im[::-1].transpose(1, 0, 3, 2)

    up_re, up_im = pw_re[1:], pw_im[1:]
    q_re = (c_re[None] * up_re[:, :, None, :] - c_im[None] * up_im[:, :, None, :])
    q_im = -(c_re[None] * up_im[:, :, None, :] + c_im[None] * up_re[:, :, None, :])
    q_re = q_re.transpose(1, 3, 0, 2)
    q_im = q_im.transpose(1, 3, 0, 2)

    are, aim = [pw_re[SSM_CHUNK]], [pw_im[SSM_CHUNK]]
    for _ in range(n_steps - 1):
        r, im = are[-1], aim[-1]
        are.append(r * r - im * im)
        aim.append(2.0 * r * im)
    are, aim = jnp.stack(are, axis=1), jnp.stack(aim, axis=1)

    nbu = SSM_GROUPS // SSM_BUNDLE
    eye = jnp.eye(SSM_BUNDLE, dtype=F32)
    bundle = lambda x: x.reshape((nbu, SSM_BUNDLE) + x.shape[1:])
    m_b = (bundle(m).transpose(0, 2, 1, 3, 4, 5)[:, :, :, :, :, None, :]
           * eye[None, None, :, None, None, :, None]).reshape(nbu, SSM_COLS, SSM_COLS)

    def state_in(x):
        x = bundle(x).transpose(0, 2, 1, 3, 4)[:, :, :, :, None, :] * eye[None, None, :, None, :, None]
        return x.reshape(nbu, SSM_COLS, SSM_SLANES)

    def state_out(x):
        x = bundle(x)[:, :, :, :, None, :] * eye[None, :, None, None, :, None]
        return x.reshape(nbu, SSM_SLANES, SSM_COLS)

    def lanes(x):
        return bundle(x).transpose(0, 2, 1, 3).reshape(nbu, x.shape[1], SSM_SLANES)

    return dict(ssm_m=m_b.astype(BF16), ssm_pre=state_in(p_re).astype(BF16),
                ssm_pim=state_in(p_im).astype(BF16), ssm_qre=state_out(q_re).astype(BF16),
                ssm_qim=state_out(q_im).astype(BF16), ssm_are=lanes(are), ssm_aim=lanes(aim))


def _layer_params(i, tm, n_scan_steps, ln1, w_in, g_cq, g_ckv, w_uq, w_ukv, g_qh, g_kh, sg_ln_g,
                  sg_ln_b, sg_w, sg_b, lam_re, lam_im, log_dt, ssm_b_re, ssm_b_im, ssm_c_re, ssm_c_im,
                  ssm_d, w_glu, b_glu, g_mix, w_out, ln2, w_rg, b_rg, w_rexp, b_rexp, w1, w3, w2, w_pg,
                  b_pg, w_ple):
    lp = {}
    wi = w_in[i]
    off_kr = Q_RANK + KV_RANK
    off_sg = off_kr + ROPE_DIM
    kr = wi[:, off_kr:off_sg]
    zpad = lambda n: jnp.zeros((D_MODEL, n), F32)
    kra = jnp.concatenate([zpad(NOPE_DIM), kr, zpad(LANES - QK_DIM)], axis=1)
    krb = jnp.concatenate([zpad(NOPE_DIM), _swap_rope_halves(kr), zpad(LANES - QK_DIM)], axis=1)
    lp["win"] = jnp.concatenate([wi[:, :off_kr], kra, krb, wi[:, off_sg:]], axis=1).astype(BF16)
    lp["ln1"] = ln1[i][None]
    lp["gcq"] = g_cq[i][None]
    lp["gckv"] = g_ckv[i][None]

    wq = w_uq[i].reshape(Q_RANK, ATTN_HEADS, QK_DIM)
    lp["wqa"] = _pad_heads(wq, QK_DIM).astype(BF16)
    wq_sw = jnp.concatenate([jnp.zeros((Q_RANK, ATTN_HEADS, NOPE_DIM), F32),
                             _swap_rope_halves(wq[..., NOPE_DIM:])], axis=-1)
    lp["wqb"] = _pad_heads(wq_sw, QK_DIM).astype(BF16)
    wkv = w_ukv[i].reshape(KV_RANK, ATTN_HEADS, NOPE_DIM + V_DIM)
    lp["wk"] = _pad_heads(wkv[..., :NOPE_DIM], NOPE_DIM).astype(BF16)
    lp["wv"] = _pad_heads(wkv[..., NOPE_DIM:], V_DIM).astype(BF16)
    gq_a, gq_b = _rope_gain_rows(g_qh[i])
    gk_a, gk_b = _rope_gain_rows(g_kh[i])
    lp["gqk"] = jnp.concatenate([jnp.stack([gq_a, gq_b, gk_a, gk_b]), jnp.zeros((4, LANES), F32)])
    lp["vones"] = jnp.tile((jnp.arange(HEAD_PAD) == V_DIM).astype(F32), ATTN_HEADS)[None]

    lp["sg_lng"] = sg_ln_g[i][None]
    lp["sg_lnb"] = sg_ln_b[i][None]
    lp["sg_w"] = sg_w[i]
    lp["sg_bias"] = jnp.repeat(sg_b[i].T, SG_CH, axis=1)
    grp = jnp.arange(SG_WIDTH) // SG_CH
    lp["mmean"] = ((grp[:, None] == grp[None, :]).astype(F32) / SG_CH).astype(BF16)

    lp.update(_ssm_tables(lam_re[i], lam_im[i], log_dt[i], ssm_b_re[i], ssm_b_im[i], ssm_c_re[i],
                          ssm_c_im[i], ssm_d[i], n_scan_steps))

    gm = g_mix[i]
    aw = ATTN_HEADS * V_DIM
    lp["gm_a"] = _pad_heads(gm[:aw].reshape(1, ATTN_HEADS, V_DIM), V_DIM)
    lp["gm_g"] = gm[aw:aw + SG_WIDTH][None]
    lp["gm_s"] = gm[aw + SG_WIDTH:][None]
    wo = w_out[i]
    woa = wo[:aw].reshape(ATTN_HEADS, V_DIM, D_MODEL)
    lp["wo_a"] = jnp.pad(woa, ((0, 0), (0, HEAD_PAD - V_DIM), (0, 0))).reshape(QKV_PAD, D_MODEL).astype(BF16)
    lp["wo_g"] = wo[aw:aw + SG_WIDTH].astype(BF16)
    lp["wo_s"] = wo[aw + SG_WIDTH:].astype(BF16)
    lp["wglu"] = w_glu[i].astype(BF16)
    lp["bglu"] = b_glu[i][None]
    lp["ln2"] = ln2[i][None]
    wr = jnp.concatenate([w_rg[i], w_rexp[i], jnp.zeros((D_MODEL, LANES - MOE_GROUPS - N_EXPERTS), F32)],
                         axis=1)
    lp["wr_hi"], lp["wr_lo"] = _split_bf16(wr)
    lp["br"] = jnp.concatenate([b_rg[i], b_rexp[i], jnp.zeros((LANES - MOE_GROUPS - N_EXPERTS,), F32)])[None]
    lp["ltri"] = (jnp.arange(tm)[None, :] < jnp.arange(tm)[:, None]).astype(BF16)

    lp["w13"] = jnp.concatenate([w1[i], w3[i]], axis=-1).astype(BF16)
    lp["w2"] = w2[i].astype(BF16)
    lp["wpg"] = w_pg[i].astype(BF16)
    lp["bpg"] = b_pg[i][None]
    lp["wple"] = w_ple[i].astype(BF16)
    return lp


def kernel(x, p, ln1, w_in, g_cq, g_ckv, w_uq, w_ukv, g_qh, g_kh, sg_ln_g, sg_ln_b, sg_w, sg_b, lam_re, lam_im, log_dt, ssm_b_re, ssm_b_im, ssm_c_re, ssm_c_im, ssm_d, w_glu, b_glu, g_mix, w_out, ln2, w_rg, b_rg, w_rexp, b_rexp, w1, w3, w2, w_pg, b_pg, w_ple):
    batch, seq, _ = x.shape
    depth = p.shape[0]
    t = batch * seq
    tm = min(512, seq)
    tq = min(512, seq)
    td = min(1024, seq)
    tc = min(256, seq)
    assert seq % tm == 0 and seq % SSM_CHUNK == 0 and tm % SG_CHUNK == 0
    rows_per_batch = seq // SSM_CHUNK
    n_scan_steps = max(1, (rows_per_batch - 1).bit_length())
    cr = t // SSM_CHUNK
    n_bundles = SSM_GROUPS // SSM_BUNDLE
    bundle_w = SSM_BUNDLE * SSM_CH
    tk2 = 2 * t
    n_blocks = -(-(tk2 + N_EXPERTS * (EXPERT_BLOCK - 1)) // EXPERT_BLOCK)
    n_blocks += n_blocks % 2
    n_rows = n_blocks * EXPERT_BLOCK

    cos128, sin128 = _rope_lane_tables(seq)
    h = x.reshape(t, D_MODEL)
    weights = (ln1, w_in, g_cq, g_ckv, w_uq, w_ukv, g_qh, g_kh, sg_ln_g, sg_ln_b, sg_w, sg_b, lam_re,
               lam_im, log_dt, ssm_b_re, ssm_b_im, ssm_c_re, ssm_c_im, ssm_d, w_glu, b_glu, g_mix, w_out,
               ln2, w_rg, b_rg, w_rexp, b_rexp, w1, w3, w2, w_pg, b_pg, w_ple)
    for i in range(depth):
        lp = _layer_params(i, tm, n_scan_steps, *weights)
        q, k, v, sg, xs_ssm = _mixer_in(h, lp, cos128, sin128, tm, seq // tm)
        a = _attention(q, k, v, batch, seq, tq)

        xp = xs_ssm.reshape(cr, SSM_CHUNK, n_bundles, bundle_w).transpose(2, 0, 1, 3)
        yp = _ssm(xp.reshape(n_bundles, cr, SSM_COLS), lp, rows_per_batch)
        y = yp.reshape(n_bundles, cr, SSM_CHUNK, bundle_w).transpose(1, 2, 0, 3).reshape(t, SSM_WIDTH)

        h1, xn, ri, rw, cnt = _mixer_out(a, sg, y, h, lp, tm)

        counts = cnt[0, :N_EXPERTS].astype(jnp.int32)
        pcounts = (counts + EXPERT_BLOCK - 1) // EXPERT_BLOCK * EXPERT_BLOCK
        pend = jnp.cumsum(pcounts)
        pstart = (pend - pcounts).astype(jnp.int32)
        blk_first_row = jnp.arange(n_blocks, dtype=jnp.int32) * EXPERT_BLOCK
        blk_e = jnp.minimum(jnp.sum((pend[None, :] <= blk_first_row[:, None]).astype(jnp.int32), axis=1),
                            N_EXPERTS - 1)
        dest = pstart[ri[:, 0:2]] + ri[:, 2:4]
        row_tok = _slot_map(dest.reshape(t // td, 1, 2 * td), n_rows, td)
        yb = _experts(row_tok, blk_e, xn, lp["w13"], lp["w2"])
        h = _combine(dest.reshape(t // tc, 1, 2 * tc), h1, rw, p[i].reshape(t, PLE_DIM), lp, yb, tc)
    return h.reshape(batch, seq, D_MODEL)
```

```python
import functools
import math

import jax
import jax.numpy as jnp
from jax import lax
from jax.experimental import pallas as pl
from jax.experimental.pallas import tpu as pltpu

F32 = jnp.float32
BF16 = jnp.bfloat16

D_MODEL = 1024
ATTN_HEADS = 8
NOPE_DIM = 64
ROPE_DIM = 32
QK_DIM = NOPE_DIM + ROPE_DIM
V_DIM = 64
Q_RANK = 384
KV_RANK = 256
SG_GROUPS = 4
SG_CH = 64
SG_WIDTH = SG_GROUPS * SG_CH
SG_CHUNK = 128
SSM_GROUPS = 16
SSM_CH = 16
SSM_WIDTH = SSM_GROUPS * SSM_CH
SSM_STATE = 64
MOE_GROUPS = 4
EXPERTS_PER_GROUP = 8
N_EXPERTS = MOE_GROUPS * EXPERTS_PER_GROUP
D_EXPERT = 256
PLE_DIM = 256
ROPE_THETA = 10000.0
EPS = 1e-6

LANES = 128
HEAD_PAD = LANES
QKV_PAD = ATTN_HEADS * HEAD_PAD
ATTN_HEADS_PER_STEP = 2
ATTN_BLOCKS_PER_ITER = 4
SSM_CHUNK = 16
SSM_BUNDLE = 4
SSM_COLS = SSM_BUNDLE * SSM_CHUNK * SSM_CH
SSM_SLANES = SSM_BUNDLE * SSM_STATE
EXPERT_BLOCK = 256
EXPERT_GATHER_AHEAD = 2
SCALAR_LOOP_UNROLL = 8
VMEM_LIMIT = 56 * 1024 * 1024

C_Q = 0
C_KV = C_Q + Q_RANK
C_KRA = C_KV + KV_RANK
C_KRB = C_KRA + LANES
C_SG = C_KRB + LANES
C_SSM = C_SG + 2 * SG_WIDTH
C_END = C_SSM + SSM_WIDTH


def _dot(a, b):
    return jnp.dot(a, b, preferred_element_type=F32)


def _split_bf16(x):
    hi = x.astype(BF16)
    lo = (x - hi.astype(F32)).astype(BF16)
    return hi, lo


def _rms(x, g, width=None):
    width = x.shape[-1] if width is None else width
    ms = jnp.sum(x * x, axis=-1, keepdims=True) * (1.0 / width)
    return x * lax.rsqrt(ms + EPS) * g


def _mixer_in_kernel(h_ref, ln1_ref, win_ref, gcq_ref, gckv_ref, wqa_ref, wqb_ref, wk_ref, wv_ref,
                     cos_ref, sin_ref, gqk_ref, vones_ref, lng_ref, lnb_ref, sgw_ref, sgb_ref,
                     mmean_ref, q_out, k_out, v_out, sg_out, ssm_out, *, tm):
    xn = _rms(h_ref[...], ln1_ref[...])
    z = _dot(xn.astype(BF16), win_ref[...])

    cos = cos_ref[...]
    sin = sin_ref[...]
    gqk = gqk_ref[...]
    gqa = cos * gqk[0:1, :]
    gqb = sin * gqk[1:2, :]
    gka = cos * gqk[2:3, :]
    gkb = sin * gqk[3:4, :]
    scale = QK_DIM ** -0.5 * math.log2(math.e)

    qn = _rms(z[:, C_Q:C_KV], gcq_ref[...]).astype(BF16)
    qa = _dot(qn, wqa_ref[...])
    qb = _dot(qn, wqb_ref[...])
    for hd in range(ATTN_HEADS):
        sl = slice(hd * HEAD_PAD, (hd + 1) * HEAD_PAD)
        a = qa[:, sl]
        r = lax.rsqrt(jnp.sum(a * a, axis=-1, keepdims=True) * (1.0 / QK_DIM) + EPS)
        q_out[:, sl] = ((a * gqa + qb[:, sl] * gqb) * (r * scale)).astype(BF16)

    kvn = _rms(z[:, C_KV:C_KRA], gckv_ref[...]).astype(BF16)
    ka = _dot(kvn, wk_ref[...])
    v_out[...] = (_dot(kvn, wv_ref[...]) + vones_ref[...]).astype(BF16)
    kra = z[:, C_KRA:C_KRB]
    krb = z[:, C_KRB:C_SG]
    ss_rope = jnp.sum(kra * kra, axis=-1, keepdims=True)
    krot = kra * gka + krb * gkb
    for hd in range(ATTN_HEADS):
        sl = slice(hd * HEAD_PAD, (hd + 1) * HEAD_PAD)
        a = ka[:, sl]
        r = lax.rsqrt((jnp.sum(a * a, axis=-1, keepdims=True) + ss_rope) * (1.0 / QK_DIM) + EPS)
        k_out[:, sl] = ((a * gka + krot) * r).astype(BF16)

    zg = jax.nn.gelu(z[:, C_SG:C_SSM])
    u = zg[:, :SG_WIDTH]
    vv = zg[:, SG_WIDTH:]
    mmean = mmean_ref[...]
    vh, vl = _split_bf16(vv)
    mu = _dot(vh, mmean) + _dot(vl, mmean)
    xc = vv - mu
    sh, sl_ = _split_bf16(xc * xc)
    var = _dot(sh, mmean) + _dot(sl_, mmean)
    vn = (xc * lax.rsqrt(var + EPS) * lng_ref[...] + lnb_ref[...]).astype(BF16)
    tri = (lax.broadcasted_iota(jnp.int32, (SG_CHUNK, SG_CHUNK), 1)
           <= lax.broadcasted_iota(jnp.int32, (SG_CHUNK, SG_CHUNK), 0))
    lane_grp = lax.broadcasted_iota(jnp.int32, (SG_CHUNK, SG_WIDTH), 1) // SG_CH
    ws = [jnp.where(tri, sgw_ref[g], 0.0).astype(BF16) for g in range(SG_GROUPS)]
    for c in range(tm // SG_CHUNK):
        rows = slice(c * SG_CHUNK, (c + 1) * SG_CHUNK)
        vc = vn[rows, :]
        mixed = sgb_ref[...]
        for g in range(SG_GROUPS):
            mixed = mixed + jnp.where(lane_grp == g, _dot(ws[g], vc), 0.0)
        sg_out[rows, :] = u[rows, :] * mixed

    ssm_out[...] = z[:, C_SSM:C_END].astype(BF16)


def _mixer_in(h, lp, cos128, sin128, tm, n_pos_tiles):
    t = h.shape[0]
    const = lambda *shape: pl.BlockSpec(shape, lambda i: (0,) * len(shape))
    rows = lambda w: pl.BlockSpec((tm, w), lambda i: (i, 0))
    pos = pl.BlockSpec((tm, LANES), lambda i: (i % n_pos_tiles, 0))
    return pl.pallas_call(
        functools.partial(_mixer_in_kernel, tm=tm),
        grid=(t // tm,),
        in_specs=[rows(D_MODEL), const(1, D_MODEL), const(D_MODEL, C_END), const(1, Q_RANK),
                  const(1, KV_RANK), const(Q_RANK, QKV_PAD), const(Q_RANK, QKV_PAD),
                  const(KV_RANK, QKV_PAD), const(KV_RANK, QKV_PAD), pos, pos, const(8, LANES),
                  const(1, QKV_PAD), const(1, SG_WIDTH), const(1, SG_WIDTH),
                  const(SG_GROUPS, SG_CHUNK, SG_CHUNK), const(SG_CHUNK, SG_WIDTH),
                  const(SG_WIDTH, SG_WIDTH)],
        out_specs=[rows(QKV_PAD), rows(QKV_PAD), rows(QKV_PAD), rows(SG_WIDTH), rows(SSM_WIDTH)],
        out_shape=[jax.ShapeDtypeStruct((t, QKV_PAD), BF16), jax.ShapeDtypeStruct((t, QKV_PAD), BF16),
                   jax.ShapeDtypeStruct((t, QKV_PAD), BF16), jax.ShapeDtypeStruct((t, SG_WIDTH), F32),
                   jax.ShapeDtypeStruct((t, SSM_WIDTH), BF16)],
        compiler_params=pltpu.CompilerParams(dimension_semantics=("parallel",),
                                             vmem_limit_bytes=VMEM_LIMIT),
        name="mixer_in",
    )(h, lp["ln1"], lp["win"], lp["gcq"], lp["gckv"], lp["wqa"], lp["wqb"], lp["wk"], lp["wv"],
      cos128, sin128, lp["gqk"], lp["vones"], lp["sg_lng"], lp["sg_lnb"], lp["sg_w"], lp["sg_bias"],
      lp["mmean"])


def _attn_kernel(q_ref, k_ref, v_ref, o_ref, s_even, s_odd, m_ref, acc_ref, *, tq):
    i = pl.program_id(2)
    heads = [slice(hh * HEAD_PAD, (hh + 1) * HEAD_PAD) for hh in range(ATTN_HEADS_PER_STEP)]

    def block_rows(j):
        return pl.ds(pl.multiple_of(j * tq, tq), tq)

    def scores(j, s_ref):
        for hh, sl in enumerate(heads):
            s_ref[hh] = lax.dot_general(q_ref[:, sl], k_ref[block_rows(j), sl],
                                        (((1,), (1,)), ((), ())), preferred_element_type=F32)

    def accumulate(j, s_ref, masked):
        for hh, sl in enumerate(heads):
            s = s_ref[hh]
            if masked:
                keep = (lax.broadcasted_iota(jnp.int32, (tq, tq), 1)
                        <= lax.broadcasted_iota(jnp.int32, (tq, tq), 0))
                s = jnp.where(keep, s, -jnp.inf)
            m = m_ref[hh]
            m_new = jnp.maximum(m, jnp.max(s, axis=-1, keepdims=True))
            alpha = jnp.exp2(m - m_new)
            p = jnp.exp2(s - m_new).astype(BF16)
            m_ref[hh] = m_new
            acc_ref[hh] = alpha * acc_ref[hh] + _dot(p, v_ref[block_rows(j), sl])

    m_ref[...] = jnp.full(m_ref.shape, -jnp.inf, F32)
    acc_ref[...] = jnp.zeros(acc_ref.shape, F32)
    scores(0, s_even)

    def pipelined(t, _):
        j = ATTN_BLOCKS_PER_ITER * t
        for u in range(0, ATTN_BLOCKS_PER_ITER, 2):
            scores(j + u + 1, s_odd)
            accumulate(j + u, s_even, False)
            scores(j + u + 2, s_even)
            accumulate(j + u + 1, s_odd, False)
        return 0

    n_pipelined = i // ATTN_BLOCKS_PER_ITER
    lax.fori_loop(0, n_pipelined, pipelined, 0)

    def leftover(j, _):
        accumulate(j, s_even, False)
        scores(j + 1, s_even)
        return 0

    lax.fori_loop(n_pipelined * ATTN_BLOCKS_PER_ITER, i, leftover, 0)
    accumulate(i, s_even, True)

    lane = lax.broadcasted_iota(jnp.int32, (tq, HEAD_PAD), 1)
    for hh, sl in enumerate(heads):
        acc = acc_ref[hh]
        out = acc / acc[:, V_DIM:V_DIM + 1]
        o_ref[:, sl] = jnp.where(lane < V_DIM, out, 0.0).astype(BF16)


def _attention(q, k, v, batch, seq, tq):
    t = q.shape[0]
    nq = seq // tq
    width = ATTN_HEADS_PER_STEP * HEAD_PAD
    return pl.pallas_call(
        functools.partial(_attn_kernel, tq=tq),
        grid=(batch, ATTN_HEADS // ATTN_HEADS_PER_STEP, nq),
        in_specs=[pl.BlockSpec((tq, width), lambda b, h, i: (b * nq + i, h)),
                  pl.BlockSpec((seq, width), lambda b, h, i: (b, h)),
                  pl.BlockSpec((seq, width), lambda b, h, i: (b, h))],
        out_specs=pl.BlockSpec((tq, width), lambda b, h, i: (b * nq + i, h)),
        out_shape=jax.ShapeDtypeStruct((t, QKV_PAD), BF16),
        scratch_shapes=[pltpu.VMEM((ATTN_HEADS_PER_STEP, tq, tq), F32),
                        pltpu.VMEM((ATTN_HEADS_PER_STEP, tq, tq), F32),
                        pltpu.VMEM((ATTN_HEADS_PER_STEP, tq, 1), F32),
                        pltpu.VMEM((ATTN_HEADS_PER_STEP, tq, HEAD_PAD), F32)],
        compiler_params=pltpu.CompilerParams(
            dimension_semantics=("parallel", "parallel", "arbitrary"), vmem_limit_bytes=VMEM_LIMIT),
        name="mla_attention",
    )(q, k, v)


def _ssm_kernel(x_ref, m_ref, pre_ref, pim_ref, qre_ref, qim_ref, are_ref, aim_ref, y_ref, *,
                rows_per_batch, n_steps):
    x = x_ref[0]
    cr = x.shape[0]
    y_local = _dot(x, m_ref[0])
    sre = _dot(x, pre_ref[0])
    sim = _dot(x, pim_ref[0])
    row = lax.broadcasted_iota(jnp.int32, (cr, SSM_SLANES), 0) % rows_per_batch
    are = are_ref[0]
    aim = aim_ref[0]
    for k in range(n_steps):
        sh = 1 << k
        ok = row >= sh
        pr = jnp.where(ok, pltpu.roll(sre, sh, 0), 0.0)
        pi = jnp.where(ok, pltpu.roll(sim, sh, 0), 0.0)
        ar = are[k:k + 1, :]
        ai = aim[k:k + 1, :]
        sre, sim = sre + ar * pr - ai * pi, sim + ar * pi + ai * pr
    ok = row >= 1
    hre = jnp.where(ok, pltpu.roll(sre, 1, 0), 0.0).astype(BF16)
    him = jnp.where(ok, pltpu.roll(sim, 1, 0), 0.0).astype(BF16)
    y_ref[0] = y_local + _dot(hre, qre_ref[0]) + _dot(him, qim_ref[0])


def _ssm(xp, lp, rows_per_batch):
    n_bundles, cr, _ = xp.shape
    n_steps = lp["ssm_are"].shape[1]
    blk = lambda *shape: pl.BlockSpec((1,) + shape, lambda p: (p,) + (0,) * len(shape))
    return pl.pallas_call(
        functools.partial(_ssm_kernel, rows_per_batch=rows_per_batch, n_steps=n_steps),
        grid=(n_bundles,),
        in_specs=[blk(cr, SSM_COLS), blk(SSM_COLS, SSM_COLS), blk(SSM_COLS, SSM_SLANES),
                  blk(SSM_COLS, SSM_SLANES), blk(SSM_SLANES, SSM_COLS), blk(SSM_SLANES, SSM_COLS),
                  blk(n_steps, SSM_SLANES), blk(n_steps, SSM_SLANES)],
        out_specs=blk(cr, SSM_COLS),
        out_shape=jax.ShapeDtypeStruct((n_bundles, cr, SSM_COLS), F32),
        compiler_params=pltpu.CompilerParams(dimension_semantics=("parallel",),
                                             vmem_limit_bytes=VMEM_LIMIT),
        name="s5_mixer",
    )(xp, lp["ssm_m"], lp["ssm_pre"], lp["ssm_pim"], lp["ssm_qre"], lp["ssm_qim"], lp["ssm_are"],
      lp["ssm_aim"])


def _mixer_out_kernel(a_ref, sg_ref, y_ref, h_ref, gma_ref, gmg_ref, gms_ref, woa_ref, wog_ref,
                      wos_ref, wglu_ref, bglu_ref, ln2_ref, wrh_ref, wrl_ref, br_ref, ltri_ref,
                      h1_out, xn_out, ri_out, rw_out, cnt_out, run_ref):
    i = pl.program_id(0)

    @pl.when(i == 0)
    def _():
        run_ref[...] = jnp.zeros_like(run_ref)

    g = jax.nn.gelu(y_ref[...])
    glu = g * jax.nn.sigmoid(_dot(g.astype(BF16), wglu_ref[...]) + bglu_ref[...])
    a_n = _rms(a_ref[...].astype(F32), gma_ref[...], ATTN_HEADS * V_DIM)
    g_n = _rms(sg_ref[...], gmg_ref[...])
    s_n = _rms(glu, gms_ref[...])
    h1 = (h_ref[...] + _dot(a_n.astype(BF16), woa_ref[...]) + _dot(g_n.astype(BF16), wog_ref[...])
          + _dot(s_n.astype(BF16), wos_ref[...]))
    h1_out[...] = h1
    xn = _rms(h1, ln2_ref[...])
    xn_out[...] = _pack_bf16_pairs(xn)

    xh, xl = _split_bf16(xn)
    logit = _dot(xh, wrh_ref[...]) + _dot(xl, wrh_ref[...]) + _dot(xh, wrl_ref[...]) + br_ref[...]
    shape = logit.shape
    lane = lax.broadcasted_iota(jnp.int32, shape, 1)
    neg = -jnp.inf
    is_grp = lane < MOE_GROUPS
    gl = jnp.where(is_grp, logit, neg)
    gmax = jnp.max(gl, axis=-1, keepdims=True)
    gsel = jnp.min(jnp.where(gl == gmax, lane, LANES), axis=-1, keepdims=True)
    gsum = jnp.sum(jnp.where(is_grp, jnp.exp(logit - gmax), 0.0), axis=-1, keepdims=True)
    g_w = 1.0 / gsum
    lo = MOE_GROUPS + gsel * EXPERTS_PER_GROUP
    in_grp = (lane >= lo) & (lane < lo + EXPERTS_PER_GROUP)
    el = jnp.where(in_grp, logit, neg)
    m1 = jnp.max(el, axis=-1, keepdims=True)
    i1 = jnp.min(jnp.where(el == m1, lane, LANES), axis=-1, keepdims=True)
    el2 = jnp.where(lane == i1, neg, el)
    m2 = jnp.max(el2, axis=-1, keepdims=True)
    i2 = jnp.min(jnp.where(el2 == m2, lane, LANES), axis=-1, keepdims=True)
    e2 = jnp.exp(m2 - m1)
    w1 = g_w / (1.0 + e2)
    w2 = g_w * e2 / (1.0 + e2)
    e_a = i1 - MOE_GROUPS
    e_b = i2 - MOE_GROUPS

    hit_a = lane == e_a
    hit_b = lane == e_b
    onehot = jnp.where(hit_a | hit_b, 1.0, 0.0).astype(BF16)
    before = _dot(ltri_ref[...], onehot) + run_ref[...]
    c_a = jnp.sum(jnp.where(hit_a, before, 0.0), axis=-1, keepdims=True).astype(jnp.int32)
    c_b = jnp.sum(jnp.where(hit_b, before, 0.0), axis=-1, keepdims=True).astype(jnp.int32)
    run_ref[...] = run_ref[...] + jnp.sum(onehot.astype(F32), axis=0, keepdims=True)
    cnt_out[...] = jnp.broadcast_to(run_ref[...], cnt_out.shape)

    ri_out[...] = jnp.where(lane == 0, e_a, jnp.where(lane == 1, e_b, jnp.where(
        lane == 2, c_a, jnp.where(lane == 3, c_b, 0))))
    rw_out[...] = jnp.where(lane == 0, w1, jnp.where(lane == 1, w2, 0.0))


def _mixer_out(a, sg, y, h, lp, tm):
    t = h.shape[0]
    const = lambda *shape: pl.BlockSpec(shape, lambda i: (0,) * len(shape))
    rows = lambda w: pl.BlockSpec((tm, w), lambda i: (i, 0))
    return pl.pallas_call(
        _mixer_out_kernel,
        grid=(t // tm,),
        in_specs=[rows(QKV_PAD), rows(SG_WIDTH), rows(SSM_WIDTH), rows(D_MODEL), const(1, QKV_PAD),
                  const(1, SG_WIDTH), const(1, SSM_WIDTH), const(QKV_PAD, D_MODEL),
                  const(SG_WIDTH, D_MODEL), const(SSM_WIDTH, D_MODEL), const(SSM_WIDTH, SSM_WIDTH),
                  const(1, SSM_WIDTH), const(1, D_MODEL), const(D_MODEL, LANES), const(D_MODEL, LANES),
                  const(1, LANES), const(tm, tm)],
        out_specs=[rows(D_MODEL), rows(D_MODEL // 2), rows(LANES), rows(LANES), const(8, LANES)],
        out_shape=[jax.ShapeDtypeStruct((t, D_MODEL), F32), jax.ShapeDtypeStruct((t, D_MODEL // 2), jnp.uint32),
                   jax.ShapeDtypeStruct((t, LANES), jnp.int32), jax.ShapeDtypeStruct((t, LANES), F32),
                   jax.ShapeDtypeStruct((8, LANES), F32)],
        scratch_shapes=[pltpu.VMEM((1, LANES), F32)],
        compiler_params=pltpu.CompilerParams(dimension_semantics=("arbitrary",),
                                             vmem_limit_bytes=VMEM_LIMIT),
        name="mixer_out_router",
    )(a, sg, y, h, lp["gm_a"], lp["gm_g"], lp["gm_s"], lp["wo_a"], lp["wo_g"], lp["wo_s"], lp["wglu"],
      lp["bglu"], lp["ln2"], lp["wr_hi"], lp["wr_lo"], lp["br"], lp["ltri"])


def _row_copy(src_ref, src_row, dst_ref, dst_row, sem):
    return pltpu.make_async_copy(src_ref.at[pl.ds(src_row, 1)], dst_ref.at[pl.ds(dst_row, 1)], sem)


def _rows_wait(src_ref, dst_ref, n_rows, sem):
    pltpu.make_async_copy(src_ref.at[pl.ds(0, n_rows)], dst_ref.at[pl.ds(0, n_rows)], sem).wait()


def _slot_map_kernel(dest_ref, zeros_ref, row_tok_ref, sem, *, td):
    i = pl.program_id(0)

    @pl.when(i == 0)
    def _():
        clear = pltpu.make_async_copy(zeros_ref, row_tok_ref, sem)
        clear.start()
        clear.wait()

    def place(r, _):
        row_tok_ref[dest_ref[0, 0, 2 * r]] = i * td + r
        row_tok_ref[dest_ref[0, 0, 2 * r + 1]] = i * td + r
        return 0

    lax.fori_loop(0, td, place, 0, unroll=SCALAR_LOOP_UNROLL)


def _slot_map(dest, n_rows, td):
    n_tiles = dest.shape[0]
    return pl.pallas_call(
        functools.partial(_slot_map_kernel, td=td),
        grid=(n_tiles,),
        in_specs=[pl.BlockSpec((1, 1, 2 * td), lambda i: (i, 0, 0), memory_space=pltpu.SMEM),
                  pl.BlockSpec(memory_space=pl.ANY)],
        out_specs=pl.BlockSpec(memory_space=pltpu.SMEM),
        out_shape=jax.ShapeDtypeStruct((n_rows,), jnp.int32),
        scratch_shapes=[pltpu.SemaphoreType.DMA(())],
        compiler_params=pltpu.CompilerParams(dimension_semantics=("arbitrary",)),
        name="moe_slot_map",
    )(dest, jnp.zeros((n_rows,), jnp.int32))


def _unpack_bf16_pairs(u, dtype):
    lo = lax.bitcast_convert_type(u << 16, F32)
    hi = lax.bitcast_convert_type(u & jnp.uint32(0xFFFF0000), F32)
    return jnp.concatenate([lo, hi], axis=1).astype(dtype)


def _pack_bf16_pairs(x):
    n = x.shape[1] // 2
    bits = lax.bitcast_convert_type(x.astype(BF16).astype(F32), jnp.uint32)
    return (bits[:, :n] >> 16) | (bits[:, n:] & jnp.uint32(0xFFFF0000))


def _expert_kernel(row_tok_ref, blk_e_ref, x_ref, w13_ref, w2_ref, yb_ref, xbuf, sem):
    del blk_e_ref
    b = pl.program_id(0)
    last = pl.num_programs(0) - 1
    n_buf = EXPERT_GATHER_AHEAD + 1

    def gather(block, to_slot):
        base = block * EXPERT_BLOCK
        for r in range(EXPERT_BLOCK):
            _row_copy(x_ref, row_tok_ref[base + r], xbuf.at[to_slot], r, sem.at[to_slot]).start()

    def wait(of_slot):
        _rows_wait(x_ref, xbuf.at[of_slot], EXPERT_BLOCK, sem.at[of_slot])

    @pl.when(b == 0)
    def _():
        for ahead in range(EXPERT_GATHER_AHEAD):
            gather(ahead, ahead)

    gather(jnp.minimum(b + EXPERT_GATHER_AHEAD, last), (b + EXPERT_GATHER_AHEAD) % n_buf)
    slot = b % n_buf
    wait(slot)
    hh = _dot(_unpack_bf16_pairs(xbuf[slot], BF16), w13_ref[...])
    hb = jax.nn.silu(hh[:, :D_EXPERT]) * hh[:, D_EXPERT:]
    yb_ref[...] = _pack_bf16_pairs(_dot(hb.astype(BF16), w2_ref[...]))

    @pl.when(b == last)
    def _():
        for ahead in range(1, n_buf):
            wait((b + ahead) % n_buf)


def _experts(row_tok, blk_e, xn_packed, w13, w2):
    n_rows = row_tok.shape[0]
    grid_spec = pltpu.PrefetchScalarGridSpec(
        num_scalar_prefetch=2,
        grid=(n_rows // EXPERT_BLOCK,),
        in_specs=[pl.BlockSpec(memory_space=pl.ANY),
                  pl.BlockSpec((None, D_MODEL, 2 * D_EXPERT), lambda i, rt, be: (be[i], 0, 0)),
                  pl.BlockSpec((None, D_EXPERT, D_MODEL), lambda i, rt, be: (be[i], 0, 0))],
        out_specs=pl.BlockSpec((EXPERT_BLOCK, D_MODEL // 2), lambda i, rt, be: (i, 0)),
        scratch_shapes=[pltpu.VMEM((EXPERT_GATHER_AHEAD + 1, EXPERT_BLOCK, D_MODEL // 2), jnp.uint32),
                        pltpu.SemaphoreType.DMA((EXPERT_GATHER_AHEAD + 1,))],
    )
    return pl.pallas_call(
        _expert_kernel,
        grid_spec=grid_spec,
        out_shape=jax.ShapeDtypeStruct((n_rows, D_MODEL // 2), jnp.uint32),
        compiler_params=pltpu.CompilerParams(dimension_semantics=("arbitrary",),
                                             vmem_limit_bytes=VMEM_LIMIT),
        name="moe_experts",
    )(row_tok, blk_e, xn_packed, w13, w2)


def _combine_kernel(dest_ref, dest_next_ref, h_ref, rw_ref, p_ref, wpg_ref, bpg_ref,
                    wple_ref, yb_ref, out_ref, y0_ref, y1_ref, sem, *, tc):
    i = pl.program_id(0)
    last = pl.num_programs(0) - 1
    slot = i % 2

    def gather(tile_dest_ref, to_slot):
        for r in range(tc):
            _row_copy(yb_ref, tile_dest_ref[0, 0, 2 * r], y0_ref.at[to_slot], r, sem.at[to_slot]).start()
            _row_copy(yb_ref, tile_dest_ref[0, 0, 2 * r + 1], y1_ref.at[to_slot], r,
                      sem.at[to_slot]).start()

    def wait(of_slot):
        _rows_wait(yb_ref, y0_ref.at[of_slot], tc, sem.at[of_slot])
        _rows_wait(yb_ref, y1_ref.at[of_slot], tc, sem.at[of_slot])

    @pl.when(i == 0)
    def _():
        gather(dest_ref, 0)

    gather(dest_next_ref, 1 - slot)
    wait(slot)

    rw = rw_ref[...]
    y0 = _unpack_bf16_pairs(y0_ref[slot], F32)
    y1 = _unpack_bf16_pairs(y1_ref[slot], F32)
    h2 = h_ref[...] + rw[:, 0:1] * y0 + rw[:, 1:2] * y1
    gate = jax.nn.sigmoid(_dot(h2.astype(BF16), wpg_ref[...]) + bpg_ref[...])
    out_ref[...] = h2 + gate * _dot(p_ref[...].astype(BF16), wple_ref[...])

    @pl.when(i == last)
    def _():
        wait(1 - slot)


def _combine(dest, h1, rw, p, lp, yb, tc):
    t = h1.shape[0]
    n_tiles = t // tc
    const = lambda *shape: pl.BlockSpec(shape, lambda i: (0,) * len(shape))
    rows = lambda w: pl.BlockSpec((tc, w), lambda i: (i, 0))
    grid_spec = pl.GridSpec(
        grid=(n_tiles,),
        in_specs=[pl.BlockSpec((1, 1, 2 * tc), lambda i: (i, 0, 0), memory_space=pltpu.SMEM),
                  pl.BlockSpec((1, 1, 2 * tc), lambda i: (jnp.minimum(i + 1, n_tiles - 1), 0, 0),
                               memory_space=pltpu.SMEM),
                  rows(D_MODEL), rows(LANES), rows(PLE_DIM), const(D_MODEL, D_MODEL), const(1, D_MODEL),
                  const(PLE_DIM, D_MODEL), pl.BlockSpec(memory_space=pl.ANY)],
        out_specs=rows(D_MODEL),
        scratch_shapes=[pltpu.VMEM((2, tc, D_MODEL // 2), jnp.uint32),
                        pltpu.VMEM((2, tc, D_MODEL // 2), jnp.uint32),
                        pltpu.SemaphoreType.DMA((2,))],
    )
    return pl.pallas_call(
        functools.partial(_combine_kernel, tc=tc),
        grid_spec=grid_spec,
        out_shape=jax.ShapeDtypeStruct((t, D_MODEL), F32),
        compiler_params=pltpu.CompilerParams(dimension_semantics=("arbitrary",),
                                             vmem_limit_bytes=VMEM_LIMIT),
        name="moe_combine_ple",
    )(dest, dest, h1, rw, p, lp["wpg"], lp["bpg"], lp["wple"], yb)


def _rope_lane_tables(seq):
    inv = 1.0 / (ROPE_THETA ** (jnp.arange(0, ROPE_DIM, 2, dtype=F32) / ROPE_DIM))
    ang = jnp.arange(seq, dtype=F32)[:, None] * inv[None, :]
    cos, sin = jnp.cos(ang), jnp.sin(ang)
    half = ROPE_DIM // 2
    cos128 = jnp.concatenate([jnp.ones((seq, NOPE_DIM), F32), cos, cos,
                              jnp.zeros((seq, LANES - QK_DIM), F32)], axis=1)
    sin128 = jnp.concatenate([jnp.zeros((seq, NOPE_DIM), F32), -sin, sin,
                              jnp.zeros((seq, LANES - QK_DIM), F32)], axis=1)
    del half
    return cos128, sin128


def _swap_rope_halves(w):
    half = ROPE_DIM // 2
    return jnp.concatenate([w[..., half:], w[..., :half]], axis=-1)


def _pad_heads(w, width):
    return jnp.pad(w, ((0, 0), (0, 0), (0, HEAD_PAD - width))).reshape(w.shape[0], QKV_PAD)


def _rope_gain_rows(g):
    pad = jnp.zeros((LANES - QK_DIM,), F32)
    ga = jnp.concatenate([g, pad])
    gb = jnp.concatenate([jnp.zeros((NOPE_DIM,), F32), _swap_rope_halves(g[NOPE_DIM:]), pad])
    return ga, gb


def _ssm_tables(lam_re, lam_im, log_dt, b_re, b_im, c_re, c_im, d_skip, n_steps):
    hp = lax.Precision.HIGHEST
    lr = jnp.minimum(lam_re, -1e-4)
    li = lam_im
    dt = jnp.exp(log_dt)
    mag = jnp.exp(lr * dt)
    ab_re, ab_im = mag * jnp.cos(li * dt), mag * jnp.sin(li * dt)
    den = lr * lr + li * li
    nr, ni = ab_re - 1.0, ab_im
    fr = (nr * lr + ni * li) / den
    fi = (ni * lr - nr * li) / den
    bb_re = fr[..., None] * b_re - fi[..., None] * b_im
    bb_im = fr[..., None] * b_im + fi[..., None] * b_re

    steps = jnp.arange(SSM_CHUNK + 1, dtype=F32)[:, None, None]
    pmag = jnp.exp(lr * dt * steps)
    pw_re, pw_im = pmag * jnp.cos(li * dt * steps), pmag * jnp.sin(li * dt * steps)

    w_re = pw_re[:SSM_CHUNK, :, :, None] * bb_re - pw_im[:SSM_CHUNK, :, :, None] * bb_im
    w_im = pw_re[:SSM_CHUNK, :, :, None] * bb_im + pw_im[:SSM_CHUNK, :, :, None] * bb_re
    taps = (jnp.einsum("gcn,tgnd->tgcd", c_re, w_re, precision=hp)
            - jnp.einsum("gcn,tgnd->tgcd", c_im, w_im, precision=hp))
    taps = taps.at[0].add(d_skip[:, :, None] * jnp.eye(SSM_CH, dtype=F32)[None])
    t_in = jnp.arange(SSM_CHUNK)[:, None]
    t_out = jnp.arange(SSM_CHUNK)[None, :]
    lag = t_out - t_in
    m = jnp.where((lag >= 0)[:, :, None, None, None], taps[jnp.clip(lag, 0)], 0.0)
    m = m.transpose(2, 0, 4, 1, 3)

    p_re = w_re[::-1].transpose(1, 0, 3, 2)
    p_im = w_im[::-1].transpose(1, 0, 3, 2)

    up_re, up_im = pw_re[1:], pw_im[1:]
    q_re = (c_re[None] * up_re[:, :, None, :] - c_im[None] * up_im[:, :, None, :])
    q_im = -(c_re[None] * up_im[:, :, None, :] + c_im[None] * up_re[:, :, None, :])
    q_re = q_re.transpose(1, 3, 0, 2)
    q_im = q_im.transpose(1, 3, 0, 2)

    are, aim = [pw_re[SSM_CHUNK]], [pw_im[SSM_CHUNK]]
    for _ in range(n_steps - 1):
        r, im = are[-1], aim[-1]
        are.append(r * r - im * im)
        aim.append(2.0 * r * im)
    are, aim = jnp.stack(are, axis=1), jnp.stack(aim, axis=1)

    nbu = SSM_GROUPS // SSM_BUNDLE
    eye = jnp.eye(SSM_BUNDLE, dtype=F32)
    bundle = lambda x: x.reshape((nbu, SSM_BUNDLE) + x.shape[1:])
    m_b = (bundle(m).transpose(0, 2, 1, 3, 4, 5)[:, :, :, :, :, None, :]
           * eye[None, None, :, None, None, :, None]).reshape(nbu, SSM_COLS, SSM_COLS)

    def state_in(x):
        x = bundle(x).transpose(0, 2, 1, 3, 4)[:, :, :, :, None, :] * eye[None, None, :, None, :, None]
        return x.reshape(nbu, SSM_COLS, SSM_SLANES)

    def state_out(x):
        x = bundle(x)[:, :, :, :, None, :] * eye[None, :, None, None, :, None]
        return x.reshape(nbu, SSM_SLANES, SSM_COLS)

    def lanes(x):
        return bundle(x).transpose(0, 2, 1, 3).reshape(nbu, x.shape[1], SSM_SLANES)

    return dict(ssm_m=m_b.astype(BF16), ssm_pre=state_in(p_re).astype(BF16),
                ssm_pim=state_in(p_im).astype(BF16), ssm_qre=state_out(q_re).astype(BF16),
                ssm_qim=state_out(q_im).astype(BF16), ssm_are=lanes(are), ssm_aim=lanes(aim))


def _layer_params(i, tm, n_scan_steps, ln1, w_in, g_cq, g_ckv, w_uq, w_ukv, g_qh, g_kh, sg_ln_g,
                  sg_ln_b, sg_w, sg_b, lam_re, lam_im, log_dt, ssm_b_re, ssm_b_im, ssm_c_re, ssm_c_im,
                  ssm_d, w_glu, b_glu, g_mix, w_out, ln2, w_rg, b_rg, w_rexp, b_rexp, w1, w3, w2, w_pg,
                  b_pg, w_ple):
    lp = {}
    wi = w_in[i]
    off_kr = Q_RANK + KV_RANK
    off_sg = off_kr + ROPE_DIM
    kr = wi[:, off_kr:off_sg]
    zpad = lambda n: jnp.zeros((D_MODEL, n), F32)
    kra = jnp.concatenate([zpad(NOPE_DIM), kr, zpad(LANES - QK_DIM)], axis=1)
    krb = jnp.concatenate([zpad(NOPE_DIM), _swap_rope_halves(kr), zpad(LANES - QK_DIM)], axis=1)
    lp["win"] = jnp.concatenate([wi[:, :off_kr], kra, krb, wi[:, off_sg:]], axis=1).astype(BF16)
    lp["ln1"] = ln1[i][None]
    lp["gcq"] = g_cq[i][None]
    lp["gckv"] = g_ckv[i][None]

    wq = w_uq[i].reshape(Q_RANK, ATTN_HEADS, QK_DIM)
    lp["wqa"] = _pad_heads(wq, QK_DIM).astype(BF16)
    wq_sw = jnp.concatenate([jnp.zeros((Q_RANK, ATTN_HEADS, NOPE_DIM), F32),
                             _swap_rope_halves(wq[..., NOPE_DIM:])], axis=-1)
    lp["wqb"] = _pad_heads(wq_sw, QK_DIM).astype(BF16)
    wkv = w_ukv[i].reshape(KV_RANK, ATTN_HEADS, NOPE_DIM + V_DIM)
    lp["wk"] = _pad_heads(wkv[..., :NOPE_DIM], NOPE_DIM).astype(BF16)
    lp["wv"] = _pad_heads(wkv[..., NOPE_DIM:], V_DIM).astype(BF16)
    gq_a, gq_b = _rope_gain_rows(g_qh[i])
    gk_a, gk_b = _rope_gain_rows(g_kh[i])
    lp["gqk"] = jnp.concatenate([jnp.stack([gq_a, gq_b, gk_a, gk_b]), jnp.zeros((4, LANES), F32)])
    lp["vones"] = jnp.tile((jnp.arange(HEAD_PAD) == V_DIM).astype(F32), ATTN_HEADS)[None]

    lp["sg_lng"] = sg_ln_g[i][None]
    lp["sg_lnb"] = sg_ln_b[i][None]
    lp["sg_w"] = sg_w[i]
    lp["sg_bias"] = jnp.repeat(sg_b[i].T, SG_CH, axis=1)
    grp = jnp.arange(SG_WIDTH) // SG_CH
    lp["mmean"] = ((grp[:, None] == grp[None, :]).astype(F32) / SG_CH).astype(BF16)

    lp.update(_ssm_tables(lam_re[i], lam_im[i], log_dt[i], ssm_b_re[i], ssm_b_im[i], ssm_c_re[i],
                          ssm_c_im[i], ssm_d[i], n_scan_steps))

    gm = g_mix[i]
    aw = ATTN_HEADS * V_DIM
    lp["gm_a"] = _pad_heads(gm[:aw].reshape(1, ATTN_HEADS, V_DIM), V_DIM)
    lp["gm_g"] = gm[aw:aw + SG_WIDTH][None]
    lp["gm_s"] = gm[aw + SG_WIDTH:][None]
    wo = w_out[i]
    woa = wo[:aw].reshape(ATTN_HEADS, V_DIM, D_MODEL)
    lp["wo_a"] = jnp.pad(woa, ((0, 0), (0, HEAD_PAD - V_DIM), (0, 0))).reshape(QKV_PAD, D_MODEL).astype(BF16)
    lp["wo_g"] = wo[aw:aw + SG_WIDTH].astype(BF16)
    lp["wo_s"] = wo[aw + SG_WIDTH:].astype(BF16)
    lp["wglu"] = w_glu[i].astype(BF16)
    lp["bglu"] = b_glu[i][None]
    lp["ln2"] = ln2[i][None]
    wr = jnp.concatenate([w_rg[i], w_rexp[i], jnp.zeros((D_MODEL, LANES - MOE_GROUPS - N_EXPERTS), F32)],
                         axis=1)
    lp["wr_hi"], lp["wr_lo"] = _split_bf16(wr)
    lp["br"] = jnp.concatenate([b_rg[i], b_rexp[i], jnp.zeros((LANES - MOE_GROUPS - N_EXPERTS,), F32)])[None]
    lp["ltri"] = (jnp.arange(tm)[None, :] < jnp.arange(tm)[:, None]).astype(BF16)

    lp["w13"] = jnp.concatenate([w1[i], w3[i]], axis=-1).astype(BF16)
    lp["w2"] = w2[i].astype(BF16)
    lp["wpg"] = w_pg[i].astype(BF16)
    lp["bpg"] = b_pg[i][None]
    lp["wple"] = w_ple[i].astype(BF16)
    return lp


def kernel(x, p, ln1, w_in, g_cq, g_ckv, w_uq, w_ukv, g_qh, g_kh, sg_ln_g, sg_ln_b, sg_w, sg_b, lam_re, lam_im, log_dt, ssm_b_re, ssm_b_im, ssm_c_re, ssm_c_im, ssm_d, w_glu, b_glu, g_mix, w_out, ln2, w_rg, b_rg, w_rexp, b_rexp, w1, w3, w2, w_pg, b_pg, w_ple):
    batch, seq, _ = x.shape
    depth = p.shape[0]
    t = batch * seq
    tm = min(512, seq)
    tq = min(512, seq)
    td = min(1024, seq)
    tc = min(256, seq)
    assert seq % tm == 0 and seq % SSM_CHUNK == 0 and tm % SG_CHUNK == 0
    rows_per_batch = seq // SSM_CHUNK
    n_scan_steps = max(1, (rows_per_batch - 1).bit_length())
    cr = t // SSM_CHUNK
    n_bundles = SSM_GROUPS // SSM_BUNDLE
    bundle_w = SSM_BUNDLE * SSM_CH
    tk2 = 2 * t
    n_blocks = -(-(tk2 + N_EXPERTS * (EXPERT_BLOCK - 1)) // EXPERT_BLOCK)
    n_rows = n_blocks * EXPERT_BLOCK

    cos128, sin128 = _rope_lane_tables(seq)
    h = x.reshape(t, D_MODEL)
    weights = (ln1, w_in, g_cq, g_ckv, w_uq, w_ukv, g_qh, g_kh, sg_ln_g, sg_ln_b, sg_w, sg_b, lam_re,
               lam_im, log_dt, ssm_b_re, ssm_b_im, ssm_c_re, ssm_c_im, ssm_d, w_glu, b_glu, g_mix, w_out,
               ln2, w_rg, b_rg, w_rexp, b_rexp, w1, w3, w2, w_pg, b_pg, w_ple)
    for i in range(depth):
        lp = _layer_params(i, tm, n_scan_steps, *weights)
        q, k, v, sg, xs_ssm = _mixer_in(h, lp, cos128, sin128, tm, seq // tm)
        a = _attention(q, k, v, batch, seq, tq)

        xp = xs_ssm.reshape(cr, SSM_CHUNK, n_bundles, bundle_w).transpose(2, 0, 1, 3)
        yp = _ssm(xp.reshape(n_bundles, cr, SSM_COLS), lp, rows_per_batch)
        y = yp.reshape(n_bundles, cr, SSM_CHUNK, bundle_w).transpose(1, 2, 0, 3).reshape(t, SSM_WIDTH)

        h1, xn, ri, rw, cnt = _mixer_out(a, sg, y, h, lp, tm)

        counts = cnt[0, :N_EXPERTS].astype(jnp.int32)
        pcounts = (counts + EXPERT_BLOCK - 1) // EXPERT_BLOCK * EXPERT_BLOCK
        pend = jnp.cumsum(pcounts)
        pstart = (pend - pcounts).astype(jnp.int32)
        blk_first_row = jnp.arange(n_blocks, dtype=jnp.int32) * EXPERT_BLOCK
        blk_e = jnp.minimum(jnp.sum((pend[None, :] <= blk_first_row[:, None]).astype(jnp.int32), axis=1),
                            N_EXPERTS - 1)
        dest = pstart[ri[:, 0:2]] + ri[:, 2:4]
        row_tok = _slot_map(dest.reshape(t // td, 1, 2 * td), n_rows, td)
        yb = _experts(row_tok, blk_e, xn, lp["w13"], lp["w2"])
        h = _combine(dest.reshape(t // tc, 1, 2 * tc), h1, rw, p[i].reshape(t, PLE_DIM), lp, yb, tc)
    return h.reshape(batch, seq, D_MODEL)
```

```python
import functools
import math

import jax
import jax.numpy as jnp
from jax import lax
from jax.experimental import pallas as pl
from jax.experimental.pallas import tpu as pltpu

F32 = jnp.float32
BF16 = jnp.bfloat16

D_MODEL = 1024
ATTN_HEADS = 8
NOPE_DIM = 64
ROPE_DIM = 32
QK_DIM = NOPE_DIM + ROPE_DIM
V_DIM = 64
Q_RANK = 384
KV_RANK = 256
SG_GROUPS = 4
SG_CH = 64
SG_WIDTH = SG_GROUPS * SG_CH
SG_CHUNK = 128
SSM_GROUPS = 16
SSM_CH = 16
SSM_WIDTH = SSM_GROUPS * SSM_CH
SSM_STATE = 64
MOE_GROUPS = 4
EXPERTS_PER_GROUP = 8
N_EXPERTS = MOE_GROUPS * EXPERTS_PER_GROUP
D_EXPERT = 256
PLE_DIM = 256
ROPE_THETA = 10000.0
EPS = 1e-6

LANES = 128
HEAD_PAD = LANES
QKV_PAD = ATTN_HEADS * HEAD_PAD
ATTN_HEADS_PER_STEP = 2
ATTN_BLOCKS_PER_ITER = 4
SSM_CHUNK = 16
SSM_BUNDLE = 4
SSM_COLS = SSM_BUNDLE * SSM_CHUNK * SSM_CH
SSM_SLANES = SSM_BUNDLE * SSM_STATE
ROUTE_COLS = 8
EXPERT_BLOCK = 256
EXPERT_GATHER_AHEAD = 2
SCALAR_LOOP_UNROLL = 8
VMEM_LIMIT = 56 * 1024 * 1024

C_Q = 0
C_KV = C_Q + Q_RANK
C_KRA = C_KV + KV_RANK
C_KRB = C_KRA + LANES
C_SG = C_KRB + LANES
C_SSM = C_SG + 2 * SG_WIDTH
C_END = C_SSM + SSM_WIDTH


def _dot(a, b):
    return jnp.dot(a, b, preferred_element_type=F32)


def _split_bf16(x):
    hi = x.astype(BF16)
    lo = (x - hi.astype(F32)).astype(BF16)
    return hi, lo


def _rms(x, g, width=None):
    width = x.shape[-1] if width is None else width
    ms = jnp.sum(x * x, axis=-1, keepdims=True) * (1.0 / width)
    return x * lax.rsqrt(ms + EPS) * g


def _mixer_in_kernel(h_ref, ln1_ref, win_ref, gcq_ref, gckv_ref, wqa_ref, wqb_ref, wk_ref, wv_ref,
                     cos_ref, sin_ref, gqk_ref, vones_ref, lng_ref, lnb_ref, sgw_ref, sgb_ref,
                     mmean_ref, q_out, k_out, v_out, sg_out, ssm_out, *, tm):
    xn = _rms(h_ref[...], ln1_ref[...])
    z = _dot(xn.astype(BF16), win_ref[...])

    cos = cos_ref[...]
    sin = sin_ref[...]
    gqk = gqk_ref[...]
    gqa = cos * gqk[0:1, :]
    gqb = sin * gqk[1:2, :]
    gka = cos * gqk[2:3, :]
    gkb = sin * gqk[3:4, :]
    scale = QK_DIM ** -0.5 * math.log2(math.e)

    qn = _rms(z[:, C_Q:C_KV], gcq_ref[...]).astype(BF16)
    qa = _dot(qn, wqa_ref[...])
    qb = _dot(qn, wqb_ref[...])
    for hd in range(ATTN_HEADS):
        sl = slice(hd * HEAD_PAD, (hd + 1) * HEAD_PAD)
        a = qa[:, sl]
        r = lax.rsqrt(jnp.sum(a * a, axis=-1, keepdims=True) * (1.0 / QK_DIM) + EPS)
        q_out[:, sl] = ((a * gqa + qb[:, sl] * gqb) * (r * scale)).astype(BF16)

    kvn = _rms(z[:, C_KV:C_KRA], gckv_ref[...]).astype(BF16)
    ka = _dot(kvn, wk_ref[...])
    v_out[...] = (_dot(kvn, wv_ref[...]) + vones_ref[...]).astype(BF16)
    kra = z[:, C_KRA:C_KRB]
    krb = z[:, C_KRB:C_SG]
    ss_rope = jnp.sum(kra * kra, axis=-1, keepdims=True)
    krot = kra * gka + krb * gkb
    for hd in range(ATTN_HEADS):
        sl = slice(hd * HEAD_PAD, (hd + 1) * HEAD_PAD)
        a = ka[:, sl]
        r = lax.rsqrt((jnp.sum(a * a, axis=-1, keepdims=True) + ss_rope) * (1.0 / QK_DIM) + EPS)
        k_out[:, sl] = ((a * gka + krot) * r).astype(BF16)

    zg = jax.nn.gelu(z[:, C_SG:C_SSM])
    u = zg[:, :SG_WIDTH]
    vv = zg[:, SG_WIDTH:]
    mmean = mmean_ref[...]
    vh, vl = _split_bf16(vv)
    mu = _dot(vh, mmean) + _dot(vl, mmean)
    xc = vv - mu
    sh, sl_ = _split_bf16(xc * xc)
    var = _dot(sh, mmean) + _dot(sl_, mmean)
    vn = (xc * lax.rsqrt(var + EPS) * lng_ref[...] + lnb_ref[...]).astype(BF16)
    tri = (lax.broadcasted_iota(jnp.int32, (SG_CHUNK, SG_CHUNK), 1)
           <= lax.broadcasted_iota(jnp.int32, (SG_CHUNK, SG_CHUNK), 0))
    lane_grp = lax.broadcasted_iota(jnp.int32, (SG_CHUNK, SG_WIDTH), 1) // SG_CH
    ws = [jnp.where(tri, sgw_ref[g], 0.0).astype(BF16) for g in range(SG_GROUPS)]
    for c in range(tm // SG_CHUNK):
        rows = slice(c * SG_CHUNK, (c + 1) * SG_CHUNK)
        vc = vn[rows, :]
        mixed = sgb_ref[...]
        for g in range(SG_GROUPS):
            mixed = mixed + jnp.where(lane_grp == g, _dot(ws[g], vc), 0.0)
        sg_out[rows, :] = u[rows, :] * mixed

    ssm_out[...] = z[:, C_SSM:C_END].astype(BF16)


def _mixer_in(h, lp, cos128, sin128, tm, n_pos_tiles):
    t = h.shape[0]
    const = lambda *shape: pl.BlockSpec(shape, lambda i: (0,) * len(shape))
    rows = lambda w: pl.BlockSpec((tm, w), lambda i: (i, 0))
    pos = pl.BlockSpec((tm, LANES), lambda i: (i % n_pos_tiles, 0))
    return pl.pallas_call(
        functools.partial(_mixer_in_kernel, tm=tm),
        grid=(t // tm,),
        in_specs=[rows(D_MODEL), const(1, D_MODEL), const(D_MODEL, C_END), const(1, Q_RANK),
                  const(1, KV_RANK), const(Q_RANK, QKV_PAD), const(Q_RANK, QKV_PAD),
                  const(KV_RANK, QKV_PAD), const(KV_RANK, QKV_PAD), pos, pos, const(8, LANES),
                  const(1, QKV_PAD), const(1, SG_WIDTH), const(1, SG_WIDTH),
                  const(SG_GROUPS, SG_CHUNK, SG_CHUNK), const(SG_CHUNK, SG_WIDTH),
                  const(SG_WIDTH, SG_WIDTH)],
        out_specs=[rows(QKV_PAD), rows(QKV_PAD), rows(QKV_PAD), rows(SG_WIDTH), rows(SSM_WIDTH)],
        out_shape=[jax.ShapeDtypeStruct((t, QKV_PAD), BF16), jax.ShapeDtypeStruct((t, QKV_PAD), BF16),
                   jax.ShapeDtypeStruct((t, QKV_PAD), BF16), jax.ShapeDtypeStruct((t, SG_WIDTH), F32),
                   jax.ShapeDtypeStruct((t, SSM_WIDTH), BF16)],
        compiler_params=pltpu.CompilerParams(dimension_semantics=("parallel",),
                                             vmem_limit_bytes=VMEM_LIMIT),
        name="mixer_in",
    )(h, lp["ln1"], lp["win"], lp["gcq"], lp["gckv"], lp["wqa"], lp["wqb"], lp["wk"], lp["wv"],
      cos128, sin128, lp["gqk"], lp["vones"], lp["sg_lng"], lp["sg_lnb"], lp["sg_w"], lp["sg_bias"],
      lp["mmean"])


def _attn_kernel(q_ref, k_ref, v_ref, o_ref, s_even, s_odd, m_ref, acc_ref, *, tq):
    i = pl.program_id(2)
    heads = [slice(hh * HEAD_PAD, (hh + 1) * HEAD_PAD) for hh in range(ATTN_HEADS_PER_STEP)]

    def block_rows(j):
        return pl.ds(pl.multiple_of(j * tq, tq), tq)

    def scores(j, s_ref):
        for hh, sl in enumerate(heads):
            s_ref[hh] = lax.dot_general(q_ref[:, sl], k_ref[block_rows(j), sl],
                                        (((1,), (1,)), ((), ())), preferred_element_type=F32)

    def accumulate(j, s_ref, masked):
        for hh, sl in enumerate(heads):
            s = s_ref[hh]
            if masked:
                keep = (lax.broadcasted_iota(jnp.int32, (tq, tq), 1)
                        <= lax.broadcasted_iota(jnp.int32, (tq, tq), 0))
                s = jnp.where(keep, s, -jnp.inf)
            m = m_ref[hh]
            m_new = jnp.maximum(m, jnp.max(s, axis=-1, keepdims=True))
            alpha = jnp.exp2(m - m_new)
            p = jnp.exp2(s - m_new).astype(BF16)
            m_ref[hh] = m_new
            acc_ref[hh] = alpha * acc_ref[hh] + _dot(p, v_ref[block_rows(j), sl])

    m_ref[...] = jnp.full(m_ref.shape, -jnp.inf, F32)
    acc_ref[...] = jnp.zeros(acc_ref.shape, F32)
    scores(0, s_even)

    def pipelined(t, _):
        j = ATTN_BLOCKS_PER_ITER * t
        for u in range(0, ATTN_BLOCKS_PER_ITER, 2):
            scores(j + u + 1, s_odd)
            accumulate(j + u, s_even, False)
            scores(j + u + 2, s_even)
            accumulate(j + u + 1, s_odd, False)
        return 0

    n_pipelined = i // ATTN_BLOCKS_PER_ITER
    lax.fori_loop(0, n_pipelined, pipelined, 0)

    def leftover(j, _):
        accumulate(j, s_even, False)
        scores(j + 1, s_even)
        return 0

    lax.fori_loop(n_pipelined * ATTN_BLOCKS_PER_ITER, i, leftover, 0)
    accumulate(i, s_even, True)

    lane = lax.broadcasted_iota(jnp.int32, (tq, HEAD_PAD), 1)
    for hh, sl in enumerate(heads):
        acc = acc_ref[hh]
        out = acc / acc[:, V_DIM:V_DIM + 1]
        o_ref[:, sl] = jnp.where(lane < V_DIM, out, 0.0).astype(BF16)


def _attention(q, k, v, batch, seq, tq):
    t = q.shape[0]
    nq = seq // tq
    width = ATTN_HEADS_PER_STEP * HEAD_PAD
    return pl.pallas_call(
        functools.partial(_attn_kernel, tq=tq),
        grid=(batch, ATTN_HEADS // ATTN_HEADS_PER_STEP, nq),
        in_specs=[pl.BlockSpec((tq, width), lambda b, h, i: (b * nq + i, h)),
                  pl.BlockSpec((seq, width), lambda b, h, i: (b, h)),
                  pl.BlockSpec((seq, width), lambda b, h, i: (b, h))],
        out_specs=pl.BlockSpec((tq, width), lambda b, h, i: (b * nq + i, h)),
        out_shape=jax.ShapeDtypeStruct((t, QKV_PAD), BF16),
        scratch_shapes=[pltpu.VMEM((ATTN_HEADS_PER_STEP, tq, tq), F32),
                        pltpu.VMEM((ATTN_HEADS_PER_STEP, tq, tq), F32),
                        pltpu.VMEM((ATTN_HEADS_PER_STEP, tq, 1), F32),
                        pltpu.VMEM((ATTN_HEADS_PER_STEP, tq, HEAD_PAD), F32)],
        compiler_params=pltpu.CompilerParams(
            dimension_semantics=("parallel", "parallel", "arbitrary"), vmem_limit_bytes=VMEM_LIMIT),
        name="mla_attention",
    )(q, k, v)


def _ssm_kernel(x_ref, m_ref, pre_ref, pim_ref, qre_ref, qim_ref, are_ref, aim_ref, y_ref, *,
                rows_per_batch, n_steps):
    x = x_ref[0]
    cr = x.shape[0]
    y_local = _dot(x, m_ref[0])
    sre = _dot(x, pre_ref[0])
    sim = _dot(x, pim_ref[0])
    row = lax.broadcasted_iota(jnp.int32, (cr, SSM_SLANES), 0) % rows_per_batch
    are = are_ref[0]
    aim = aim_ref[0]
    for k in range(n_steps):
        sh = 1 << k
        ok = row >= sh
        pr = jnp.where(ok, pltpu.roll(sre, sh, 0), 0.0)
        pi = jnp.where(ok, pltpu.roll(sim, sh, 0), 0.0)
        ar = are[k:k + 1, :]
        ai = aim[k:k + 1, :]
        sre, sim = sre + ar * pr - ai * pi, sim + ar * pi + ai * pr
    ok = row >= 1
    hre = jnp.where(ok, pltpu.roll(sre, 1, 0), 0.0).astype(BF16)
    him = jnp.where(ok, pltpu.roll(sim, 1, 0), 0.0).astype(BF16)
    y_ref[0] = y_local + _dot(hre, qre_ref[0]) + _dot(him, qim_ref[0])


def _ssm(xp, lp, rows_per_batch):
    n_bundles, cr, _ = xp.shape
    n_steps = lp["ssm_are"].shape[1]
    blk = lambda *shape: pl.BlockSpec((1,) + shape, lambda p: (p,) + (0,) * len(shape))
    return pl.pallas_call(
        functools.partial(_ssm_kernel, rows_per_batch=rows_per_batch, n_steps=n_steps),
        grid=(n_bundles,),
        in_specs=[blk(cr, SSM_COLS), blk(SSM_COLS, SSM_COLS), blk(SSM_COLS, SSM_SLANES),
                  blk(SSM_COLS, SSM_SLANES), blk(SSM_SLANES, SSM_COLS), blk(SSM_SLANES, SSM_COLS),
                  blk(n_steps, SSM_SLANES), blk(n_steps, SSM_SLANES)],
        out_specs=blk(cr, SSM_COLS),
        out_shape=jax.ShapeDtypeStruct((n_bundles, cr, SSM_COLS), F32),
        compiler_params=pltpu.CompilerParams(dimension_semantics=("parallel",),
                                             vmem_limit_bytes=VMEM_LIMIT),
        name="s5_mixer",
    )(xp, lp["ssm_m"], lp["ssm_pre"], lp["ssm_pim"], lp["ssm_qre"], lp["ssm_qim"], lp["ssm_are"],
      lp["ssm_aim"])


def _mixer_out_kernel(a_ref, sg_ref, y_ref, h_ref, gma_ref, gmg_ref, gms_ref, woa_ref, wog_ref,
                      wos_ref, wglu_ref, bglu_ref, ln2_ref, wrh_ref, wrl_ref, br_ref, ltri_ref,
                      h1_out, xn_out, ri_out, rw_out, cnt_out, run_ref):
    i = pl.program_id(0)

    @pl.when(i == 0)
    def _():
        run_ref[...] = jnp.zeros_like(run_ref)

    g = jax.nn.gelu(y_ref[...])
    glu = g * jax.nn.sigmoid(_dot(g.astype(BF16), wglu_ref[...]) + bglu_ref[...])
    a_n = _rms(a_ref[...].astype(F32), gma_ref[...], ATTN_HEADS * V_DIM)
    g_n = _rms(sg_ref[...], gmg_ref[...])
    s_n = _rms(glu, gms_ref[...])
    h1 = (h_ref[...] + _dot(a_n.astype(BF16), woa_ref[...]) + _dot(g_n.astype(BF16), wog_ref[...])
          + _dot(s_n.astype(BF16), wos_ref[...]))
    h1_out[...] = h1
    xn = _rms(h1, ln2_ref[...])
    xn_out[...] = _pack_bf16_pairs(xn)

    xh, xl = _split_bf16(xn)
    logit = _dot(xh, wrh_ref[...]) + _dot(xl, wrh_ref[...]) + _dot(xh, wrl_ref[...]) + br_ref[...]
    shape = logit.shape
    lane = lax.broadcasted_iota(jnp.int32, shape, 1)
    neg = -jnp.inf
    is_grp = lane < MOE_GROUPS
    gl = jnp.where(is_grp, logit, neg)
    gmax = jnp.max(gl, axis=-1, keepdims=True)
    gsel = jnp.min(jnp.where(gl == gmax, lane, LANES), axis=-1, keepdims=True)
    gsum = jnp.sum(jnp.where(is_grp, jnp.exp(logit - gmax), 0.0), axis=-1, keepdims=True)
    g_w = 1.0 / gsum
    lo = MOE_GROUPS + gsel * EXPERTS_PER_GROUP
    in_grp = (lane >= lo) & (lane < lo + EXPERTS_PER_GROUP)
    el = jnp.where(in_grp, logit, neg)
    m1 = jnp.max(el, axis=-1, keepdims=True)
    i1 = jnp.min(jnp.where(el == m1, lane, LANES), axis=-1, keepdims=True)
    el2 = jnp.where(lane == i1, neg, el)
    m2 = jnp.max(el2, axis=-1, keepdims=True)
    i2 = jnp.min(jnp.where(el2 == m2, lane, LANES), axis=-1, keepdims=True)
    e2 = jnp.exp(m2 - m1)
    w1 = g_w / (1.0 + e2)
    w2 = g_w * e2 / (1.0 + e2)
    e_a = i1 - MOE_GROUPS
    e_b = i2 - MOE_GROUPS

    hit_a = lane == e_a
    hit_b = lane == e_b
    onehot = jnp.where(hit_a | hit_b, 1.0, 0.0).astype(BF16)
    before = _dot(ltri_ref[...], onehot) + run_ref[...]
    c_a = jnp.sum(jnp.where(hit_a, before, 0.0), axis=-1, keepdims=True).astype(jnp.int32)
    c_b = jnp.sum(jnp.where(hit_b, before, 0.0), axis=-1, keepdims=True).astype(jnp.int32)
    run_ref[...] = run_ref[...] + jnp.sum(onehot.astype(F32), axis=0, keepdims=True)
    cnt_out[...] = jnp.broadcast_to(run_ref[...], cnt_out.shape)

    ri_out[...] = jnp.where(lane == 0, e_a, jnp.where(lane == 1, e_b, jnp.where(
        lane == 2, c_a, jnp.where(lane == 3, c_b, 0))))[:, :ROUTE_COLS]
    rw_out[...] = jnp.where(lane == 0, w1, jnp.where(lane == 1, w2, 0.0))[:, :ROUTE_COLS]


def _mixer_out(a, sg, y, h, lp, tm):
    t = h.shape[0]
    const = lambda *shape: pl.BlockSpec(shape, lambda i: (0,) * len(shape))
    rows = lambda w: pl.BlockSpec((tm, w), lambda i: (i, 0))
    return pl.pallas_call(
        _mixer_out_kernel,
        grid=(t // tm,),
        in_specs=[rows(QKV_PAD), rows(SG_WIDTH), rows(SSM_WIDTH), rows(D_MODEL), const(1, QKV_PAD),
                  const(1, SG_WIDTH), const(1, SSM_WIDTH), const(QKV_PAD, D_MODEL),
                  const(SG_WIDTH, D_MODEL), const(SSM_WIDTH, D_MODEL), const(SSM_WIDTH, SSM_WIDTH),
                  const(1, SSM_WIDTH), const(1, D_MODEL), const(D_MODEL, LANES), const(D_MODEL, LANES),
                  const(1, LANES), const(tm, tm)],
        out_specs=[rows(D_MODEL), rows(D_MODEL // 2), rows(ROUTE_COLS), rows(ROUTE_COLS), const(8, LANES)],
        out_shape=[jax.ShapeDtypeStruct((t, D_MODEL), F32), jax.ShapeDtypeStruct((t, D_MODEL // 2), jnp.uint32),
                   jax.ShapeDtypeStruct((t, ROUTE_COLS), jnp.int32),
                   jax.ShapeDtypeStruct((t, ROUTE_COLS), F32),
                   jax.ShapeDtypeStruct((8, LANES), F32)],
        scratch_shapes=[pltpu.VMEM((1, LANES), F32)],
        compiler_params=pltpu.CompilerParams(dimension_semantics=("arbitrary",),
                                             vmem_limit_bytes=VMEM_LIMIT),
        name="mixer_out_router",
    )(a, sg, y, h, lp["gm_a"], lp["gm_g"], lp["gm_s"], lp["wo_a"], lp["wo_g"], lp["wo_s"], lp["wglu"],
      lp["bglu"], lp["ln2"], lp["wr_hi"], lp["wr_lo"], lp["br"], lp["ltri"])


def _row_copy(src_ref, src_row, dst_ref, dst_row, sem):
    return pltpu.make_async_copy(src_ref.at[pl.ds(src_row, 1)], dst_ref.at[pl.ds(dst_row, 1)], sem)


def _rows_wait(src_ref, dst_ref, n_rows, sem):
    pltpu.make_async_copy(src_ref.at[pl.ds(0, n_rows)], dst_ref.at[pl.ds(0, n_rows)], sem).wait()


def _slot_map_kernel(dest_ref, zeros_ref, row_tok_ref, sem, *, td):
    i = pl.program_id(0)

    @pl.when(i == 0)
    def _():
        clear = pltpu.make_async_copy(zeros_ref, row_tok_ref, sem)
        clear.start()
        clear.wait()

    def place(r, _):
        row_tok_ref[dest_ref[0, 0, 2 * r]] = i * td + r
        row_tok_ref[dest_ref[0, 0, 2 * r + 1]] = i * td + r
        return 0

    lax.fori_loop(0, td, place, 0, unroll=SCALAR_LOOP_UNROLL)


def _slot_map(dest, n_rows, td):
    n_tiles = dest.shape[0]
    return pl.pallas_call(
        functools.partial(_slot_map_kernel, td=td),
        grid=(n_tiles,),
        in_specs=[pl.BlockSpec((1, 1, 2 * td), lambda i: (i, 0, 0), memory_space=pltpu.SMEM),
                  pl.BlockSpec(memory_space=pl.ANY)],
        out_specs=pl.BlockSpec(memory_space=pltpu.SMEM),
        out_shape=jax.ShapeDtypeStruct((n_rows,), jnp.int32),
        scratch_shapes=[pltpu.SemaphoreType.DMA(())],
        compiler_params=pltpu.CompilerParams(dimension_semantics=("arbitrary",)),
        name="moe_slot_map",
    )(dest, jnp.zeros((n_rows,), jnp.int32))


def _unpack_bf16_pairs(u, dtype):
    lo = lax.bitcast_convert_type(u << 16, F32)
    hi = lax.bitcast_convert_type(u & jnp.uint32(0xFFFF0000), F32)
    return jnp.concatenate([lo, hi], axis=1).astype(dtype)


def _pack_bf16_pairs(x):
    n = x.shape[1] // 2
    bits = lax.bitcast_convert_type(x.astype(BF16).astype(F32), jnp.uint32)
    return (bits[:, :n] >> 16) | (bits[:, n:] & jnp.uint32(0xFFFF0000))


def _expert_kernel(row_tok_ref, blk_e_ref, x_ref, w13_ref, w2_ref, yb_ref, xbuf, sem):
    del blk_e_ref
    b = pl.program_id(0)
    last = pl.num_programs(0) - 1
    n_buf = EXPERT_GATHER_AHEAD + 1

    def gather(block, to_slot):
        base = block * EXPERT_BLOCK
        for r in range(EXPERT_BLOCK):
            _row_copy(x_ref, row_tok_ref[base + r], xbuf.at[to_slot], r,
                      sem.at[to_slot]).start(priority=r % 2)

    def wait(of_slot):
        _rows_wait(x_ref, xbuf.at[of_slot], EXPERT_BLOCK, sem.at[of_slot])

    @pl.when(b == 0)
    def _():
        for ahead in range(EXPERT_GATHER_AHEAD):
            gather(ahead, ahead)

    gather(jnp.minimum(b + EXPERT_GATHER_AHEAD, last), (b + EXPERT_GATHER_AHEAD) % n_buf)
    slot = b % n_buf
    wait(slot)
    hh = _dot(_unpack_bf16_pairs(xbuf[slot], BF16), w13_ref[...])
    hb = jax.nn.silu(hh[:, :D_EXPERT]) * hh[:, D_EXPERT:]
    yb_ref[...] = _pack_bf16_pairs(_dot(hb.astype(BF16), w2_ref[...]))

    @pl.when(b == last)
    def _():
        for ahead in range(1, n_buf):
            wait((b + ahead) % n_buf)


def _experts(row_tok, blk_e, xn_packed, w13, w2):
    n_rows = row_tok.shape[0]
    grid_spec = pltpu.PrefetchScalarGridSpec(
        num_scalar_prefetch=2,
        grid=(n_rows // EXPERT_BLOCK,),
        in_specs=[pl.BlockSpec(memory_space=pl.ANY),
                  pl.BlockSpec((None, D_MODEL, 2 * D_EXPERT), lambda i, rt, be: (be[i], 0, 0)),
                  pl.BlockSpec((None, D_EXPERT, D_MODEL), lambda i, rt, be: (be[i], 0, 0))],
        out_specs=pl.BlockSpec((EXPERT_BLOCK, D_MODEL // 2), lambda i, rt, be: (i, 0)),
        scratch_shapes=[pltpu.VMEM((EXPERT_GATHER_AHEAD + 1, EXPERT_BLOCK, D_MODEL // 2), jnp.uint32),
                        pltpu.SemaphoreType.DMA((EXPERT_GATHER_AHEAD + 1,))],
    )
    return pl.pallas_call(
        _expert_kernel,
        grid_spec=grid_spec,
        out_shape=jax.ShapeDtypeStruct((n_rows, D_MODEL // 2), jnp.uint32),
        compiler_params=pltpu.CompilerParams(dimension_semantics=("arbitrary",),
                                             vmem_limit_bytes=VMEM_LIMIT),
        name="moe_experts",
    )(row_tok, blk_e, xn_packed, w13, w2)


def _combine_kernel(dest_ref, dest_next_ref, h_ref, rw_ref, p_ref, wpg_ref, bpg_ref,
                    wple_ref, yb_ref, out_ref, y0_ref, y1_ref, sem, *, tc):
    i = pl.program_id(0)
    last = pl.num_programs(0) - 1
    slot = i % 2

    def gather(tile_dest_ref, to_slot):
        for r in range(tc):
            _row_copy(yb_ref, tile_dest_ref[0, 0, 2 * r], y0_ref.at[to_slot], r, sem.at[to_slot]).start()
            _row_copy(yb_ref, tile_dest_ref[0, 0, 2 * r + 1], y1_ref.at[to_slot], r,
                      sem.at[to_slot]).start()

    def wait(of_slot):
        _rows_wait(yb_ref, y0_ref.at[of_slot], tc, sem.at[of_slot])
        _rows_wait(yb_ref, y1_ref.at[of_slot], tc, sem.at[of_slot])

    @pl.when(i == 0)
    def _():
        gather(dest_ref, 0)

    gather(dest_next_ref, 1 - slot)
    wait(slot)

    rw = rw_ref[...]
    y0 = _unpack_bf16_pairs(y0_ref[slot], F32)
    y1 = _unpack_bf16_pairs(y1_ref[slot], F32)
    h2 = h_ref[...] + rw[:, 0:1] * y0 + rw[:, 1:2] * y1
    gate = jax.nn.sigmoid(_dot(h2.astype(BF16), wpg_ref[...]) + bpg_ref[...])
    out_ref[...] = h2 + gate * _dot(p_ref[...].astype(BF16), wple_ref[...])

    @pl.when(i == last)
    def _():
        wait(1 - slot)


def _combine(dest, h1, rw, p_all, layer, lp, yb, tc):
    t = h1.shape[0]
    n_tiles = t // tc
    const = lambda *shape: pl.BlockSpec(shape, lambda i: (0,) * len(shape))
    rows = lambda w: pl.BlockSpec((tc, w), lambda i: (i, 0))
    p_rows = pl.BlockSpec((tc, PLE_DIM), lambda i: (layer * n_tiles + i, 0))
    grid_spec = pl.GridSpec(
        grid=(n_tiles,),
        in_specs=[pl.BlockSpec((1, 1, 2 * tc), lambda i: (i, 0, 0), memory_space=pltpu.SMEM),
                  pl.BlockSpec((1, 1, 2 * tc), lambda i: (jnp.minimum(i + 1, n_tiles - 1), 0, 0),
                               memory_space=pltpu.SMEM),
                  rows(D_MODEL), rows(ROUTE_COLS), p_rows, const(D_MODEL, D_MODEL), const(1, D_MODEL),
                  const(PLE_DIM, D_MODEL), pl.BlockSpec(memory_space=pl.ANY)],
        out_specs=rows(D_MODEL),
        scratch_shapes=[pltpu.VMEM((2, tc, D_MODEL // 2), jnp.uint32),
                        pltpu.VMEM((2, tc, D_MODEL // 2), jnp.uint32),
                        pltpu.SemaphoreType.DMA((2,))],
    )
    return pl.pallas_call(
        functools.partial(_combine_kernel, tc=tc),
        grid_spec=grid_spec,
        out_shape=jax.ShapeDtypeStruct((t, D_MODEL), F32),
        compiler_params=pltpu.CompilerParams(dimension_semantics=("arbitrary",),
                                             vmem_limit_bytes=VMEM_LIMIT),
        name="moe_combine_ple",
    )(dest, dest, h1, rw, p_all, lp["wpg"], lp["bpg"], lp["wple"], yb)


def _rope_lane_tables(seq):
    inv = 1.0 / (ROPE_THETA ** (jnp.arange(0, ROPE_DIM, 2, dtype=F32) / ROPE_DIM))
    ang = jnp.arange(seq, dtype=F32)[:, None] * inv[None, :]
    cos, sin = jnp.cos(ang), jnp.sin(ang)
    half = ROPE_DIM // 2
    cos128 = jnp.concatenate([jnp.ones((seq, NOPE_DIM), F32), cos, cos,
                              jnp.zeros((seq, LANES - QK_DIM), F32)], axis=1)
    sin128 = jnp.concatenate([jnp.zeros((seq, NOPE_DIM), F32), -sin, sin,
                              jnp.zeros((seq, LANES - QK_DIM), F32)], axis=1)
    del half
    return cos128, sin128


def _swap_rope_halves(w):
    half = ROPE_DIM // 2
    return jnp.concatenate([w[..., half:], w[..., :half]], axis=-1)


def _pad_heads(w, width):
    return jnp.pad(w, ((0, 0), (0, 0), (0, HEAD_PAD - width))).reshape(w.shape[0], QKV_PAD)


def _rope_gain_rows(g):
    pad = jnp.zeros((LANES - QK_DIM,), F32)
    ga = jnp.concatenate([g, pad])
    gb = jnp.concatenate([jnp.zeros((NOPE_DIM,), F32), _swap_rope_halves(g[NOPE_DIM:]), pad])
    return ga, gb


def _ssm_tables(lam_re, lam_im, log_dt, b_re, b_im, c_re, c_im, d_skip, n_steps):
    hp = lax.Precision.HIGHEST
    lr = jnp.minimum(lam_re, -1e-4)
    li = lam_im
    dt = jnp.exp(log_dt)
    mag = jnp.exp(lr * dt)
    ab_re, ab_im = mag * jnp.cos(li * dt), mag * jnp.sin(li * dt)
    den = lr * lr + li * li
    nr, ni = ab_re - 1.0, ab_im
    fr = (nr * lr + ni * li) / den
    fi = (ni * lr - nr * li) / den
    bb_re = fr[..., None] * b_re - fi[..., None] * b_im
    bb_im = fr[..., None] * b_im + fi[..., None] * b_re

    steps = jnp.arange(SSM_CHUNK + 1, dtype=F32)[:, None, None]
    pmag = jnp.exp(lr * dt * steps)
    pw_re, pw_im = pmag * jnp.cos(li * dt * steps), pmag * jnp.sin(li * dt * steps)

    w_re = pw_re[:SSM_CHUNK, :, :, None] * bb_re - pw_im[:SSM_CHUNK, :, :, None] * bb_im
    w_im = pw_re[:SSM_CHUNK, :, :, None] * bb_im + pw_im[:SSM_CHUNK, :, :, None] * bb_re
    taps = (jnp.einsum("gcn,tgnd->tgcd", c_re, w_re, precision=hp)
            - jnp.einsum("gcn,tgnd->tgcd", c_im, w_im, precision=hp))
    taps = taps.at[0].add(d_skip[:, :, None] * jnp.eye(SSM_CH, dtype=F32)[None])
    t_in = jnp.arange(SSM_CHUNK)[:, None]
    t_out = jnp.arange(SSM_CHUNK)[None, :]
    lag = t_out - t_in
    m = jnp.where((lag >= 0)[:, :, None, None, None], taps[jnp.clip(lag, 0)], 0.0)
    m = m.transpose(2, 0, 4, 1, 3)

    p_re = w_re[::-1].transpose(1, 0, 3, 2)
    p_im = w_im[::-1].transpose(1, 0, 3, 2)

    up_re, up_im = pw_re[1:], pw_im[1:]
    q_re = (c_re[None] * up_re[:, :, None, :] - c_im[None] * up_im[:, :, None, :])
    q_im = -(c_re[None] * up_im[:, :, None, :] + c_im[None] * up_re[:, :, None, :])
    q_re = q_re.transpose(1, 3, 0, 2)
    q_im = q_im.transpose(1, 3, 0, 2)

    are, aim = [pw_re[SSM_CHUNK]], [pw_im[SSM_CHUNK]]
    for _ in range(n_steps - 1):
        r, im = are[-1], aim[-1]
        are.append(r * r - im * im)
        aim.append(2.0 * r * im)
    are, aim = jnp.stack(are, axis=1), jnp.stack(aim, axis=1)

    nbu = SSM_GROUPS // SSM_BUNDLE
    per_group = SSM_CHUNK * SSM_CH
    bundle = lambda x: x.reshape((nbu, SSM_BUNDLE) + x.shape[1:])
    col = jnp.arange(SSM_COLS)
    col_src = (col // (SSM_BUNDLE * SSM_CH)) * SSM_CH + col % SSM_CH
    col_grp = (col // SSM_CH) % SSM_BUNDLE
    sel_io = ((col_grp[None, None, :] == jnp.arange(SSM_BUNDLE)[:, None, None])
              & (col_src[None, None, :] == jnp.arange(per_group)[None, :, None])).astype(F32)
    lane = jnp.arange(SSM_SLANES)
    sel_st = ((lane[None, None, :] // SSM_STATE == jnp.arange(SSM_BUNDLE)[:, None, None])
              & (lane[None, None, :] % SSM_STATE == jnp.arange(SSM_STATE)[None, :, None])).astype(F32)

    def expand(x, rows_sel, cols_sel):
        return jnp.einsum("gaR,qgab,gbS->qRS", rows_sel, bundle(x), cols_sel).astype(BF16)

    def lanes(x):
        return bundle(x).transpose(0, 2, 1, 3).reshape(nbu, x.shape[1], SSM_SLANES)

    flat_io = lambda x: x.reshape(SSM_GROUPS, per_group, -1)
    return dict(ssm_m=expand(m.reshape(SSM_GROUPS, per_group, per_group), sel_io, sel_io),
                ssm_pre=expand(flat_io(p_re), sel_io, sel_st), ssm_pim=expand(flat_io(p_im), sel_io, sel_st),
                ssm_qre=expand(q_re.reshape(SSM_GROUPS, SSM_STATE, per_group), sel_st, sel_io),
                ssm_qim=expand(q_im.reshape(SSM_GROUPS, SSM_STATE, per_group), sel_st, sel_io),
                ssm_are=lanes(are), ssm_aim=lanes(aim))


def _layer_params(i, tm, n_scan_steps, ln1, w_in, g_cq, g_ckv, w_uq, w_ukv, g_qh, g_kh, sg_ln_g,
                  sg_ln_b, sg_w, sg_b, lam_re, lam_im, log_dt, ssm_b_re, ssm_b_im, ssm_c_re, ssm_c_im,
                  ssm_d, w_glu, b_glu, g_mix, w_out, ln2, w_rg, b_rg, w_rexp, b_rexp, w1, w3, w2, w_pg,
                  b_pg, w_ple):
    lp = {}
    wi = w_in[i]
    off_kr = Q_RANK + KV_RANK
    off_sg = off_kr + ROPE_DIM
    kr = wi[:, off_kr:off_sg]
    zpad = lambda n: jnp.zeros((D_MODEL, n), F32)
    kra = jnp.concatenate([zpad(NOPE_DIM), kr, zpad(LANES - QK_DIM)], axis=1)
    krb = jnp.concatenate([zpad(NOPE_DIM), _swap_rope_halves(kr), zpad(LANES - QK_DIM)], axis=1)
    lp["win"] = jnp.concatenate([wi[:, :off_kr], kra, krb, wi[:, off_sg:]], axis=1).astype(BF16)
    lp["ln1"] = ln1[i][None]
    lp["gcq"] = g_cq[i][None]
    lp["gckv"] = g_ckv[i][None]

    wq = w_uq[i].reshape(Q_RANK, ATTN_HEADS, QK_DIM)
    lp["wqa"] = _pad_heads(wq, QK_DIM).astype(BF16)
    wq_sw = jnp.concatenate([jnp.zeros((Q_RANK, ATTN_HEADS, NOPE_DIM), F32),
                             _swap_rope_halves(wq[..., NOPE_DIM:])], axis=-1)
    lp["wqb"] = _pad_heads(wq_sw, QK_DIM).astype(BF16)
    wkv = w_ukv[i].reshape(KV_RANK, ATTN_HEADS, NOPE_DIM + V_DIM)
    lp["wk"] = _pad_heads(wkv[..., :NOPE_DIM], NOPE_DIM).astype(BF16)
    lp["wv"] = _pad_heads(wkv[..., NOPE_DIM:], V_DIM).astype(BF16)
    gq_a, gq_b = _rope_gain_rows(g_qh[i])
    gk_a, gk_b = _rope_gain_rows(g_kh[i])
    lp["gqk"] = jnp.concatenate([jnp.stack([gq_a, gq_b, gk_a, gk_b]), jnp.zeros((4, LANES), F32)])
    lp["vones"] = jnp.tile((jnp.arange(HEAD_PAD) == V_DIM).astype(F32), ATTN_HEADS)[None]

    lp["sg_lng"] = sg_ln_g[i][None]
    lp["sg_lnb"] = sg_ln_b[i][None]
    lp["sg_w"] = sg_w[i]
    lp["sg_bias"] = jnp.repeat(sg_b[i].T, SG_CH, axis=1)
    grp = jnp.arange(SG_WIDTH) // SG_CH
    lp["mmean"] = ((grp[:, None] == grp[None, :]).astype(F32) / SG_CH).astype(BF16)

    lp.update(_ssm_tables(lam_re[i], lam_im[i], log_dt[i], ssm_b_re[i], ssm_b_im[i], ssm_c_re[i],
                          ssm_c_im[i], ssm_d[i], n_scan_steps))

    gm = g_mix[i]
    aw = ATTN_HEADS * V_DIM
    lp["gm_a"] = _pad_heads(gm[:aw].reshape(1, ATTN_HEADS, V_DIM), V_DIM)
    lp["gm_g"] = gm[aw:aw + SG_WIDTH][None]
    lp["gm_s"] = gm[aw + SG_WIDTH:][None]
    wo = w_out[i]
    woa = wo[:aw].reshape(ATTN_HEADS, V_DIM, D_MODEL)
    lp["wo_a"] = jnp.pad(woa, ((0, 0), (0, HEAD_PAD - V_DIM), (0, 0))).reshape(QKV_PAD, D_MODEL).astype(BF16)
    lp["wo_g"] = wo[aw:aw + SG_WIDTH].astype(BF16)
    lp["wo_s"] = wo[aw + SG_WIDTH:].astype(BF16)
    lp["wglu"] = w_glu[i].astype(BF16)
    lp["bglu"] = b_glu[i][None]
    lp["ln2"] = ln2[i][None]
    wr = jnp.concatenate([w_rg[i], w_rexp[i], jnp.zeros((D_MODEL, LANES - MOE_GROUPS - N_EXPERTS), F32)],
                         axis=1)
    lp["wr_hi"], lp["wr_lo"] = _split_bf16(wr)
    lp["br"] = jnp.concatenate([b_rg[i], b_rexp[i], jnp.zeros((LANES - MOE_GROUPS - N_EXPERTS,), F32)])[None]
    lp["ltri"] = (jnp.arange(tm)[None, :] < jnp.arange(tm)[:, None]).astype(BF16)

    lp["w13"] = jnp.concatenate([w1[i], w3[i]], axis=-1).astype(BF16)
    lp["w2"] = w2[i].astype(BF16)
    lp["wpg"] = w_pg[i].astype(BF16)
    lp["bpg"] = b_pg[i][None]
    lp["wple"] = w_ple[i].astype(BF16)
    return lp


def kernel(x, p, ln1, w_in, g_cq, g_ckv, w_uq, w_ukv, g_qh, g_kh, sg_ln_g, sg_ln_b, sg_w, sg_b, lam_re, lam_im, log_dt, ssm_b_re, ssm_b_im, ssm_c_re, ssm_c_im, ssm_d, w_glu, b_glu, g_mix, w_out, ln2, w_rg, b_rg, w_rexp, b_rexp, w1, w3, w2, w_pg, b_pg, w_ple):
    batch, seq, _ = x.shape
    depth = p.shape[0]
    t = batch * seq
    tm = min(512, seq)
    tq = min(512, seq)
    td = min(1024, seq)
    tc = min(256, seq)
    assert seq % tm == 0 and seq % SSM_CHUNK == 0 and tm % SG_CHUNK == 0
    rows_per_batch = seq // SSM_CHUNK
    n_scan_steps = max(1, (rows_per_batch - 1).bit_length())
    cr = t // SSM_CHUNK
    n_bundles = SSM_GROUPS // SSM_BUNDLE
    bundle_w = SSM_BUNDLE * SSM_CH
    tk2 = 2 * t
    n_blocks = -(-(tk2 + N_EXPERTS * (EXPERT_BLOCK - 1)) // EXPERT_BLOCK)
    n_rows = n_blocks * EXPERT_BLOCK

    cos128, sin128 = _rope_lane_tables(seq)
    h = x.reshape(t, D_MODEL)
    p_all = p.reshape(depth * t, PLE_DIM)
    weights = (ln1, w_in, g_cq, g_ckv, w_uq, w_ukv, g_qh, g_kh, sg_ln_g, sg_ln_b, sg_w, sg_b, lam_re,
               lam_im, log_dt, ssm_b_re, ssm_b_im, ssm_c_re, ssm_c_im, ssm_d, w_glu, b_glu, g_mix, w_out,
               ln2, w_rg, b_rg, w_rexp, b_rexp, w1, w3, w2, w_pg, b_pg, w_ple)
    for i in range(depth):
        lp = _layer_params(i, tm, n_scan_steps, *weights)
        q, k, v, sg, xs_ssm = _mixer_in(h, lp, cos128, sin128, tm, seq // tm)
        a = _attention(q, k, v, batch, seq, tq)

        xp = xs_ssm.reshape(cr, SSM_CHUNK, n_bundles, bundle_w).transpose(2, 0, 1, 3)
        yp = _ssm(xp.reshape(n_bundles, cr, SSM_COLS), lp, rows_per_batch)
        y = yp.reshape(n_bundles, cr, SSM_CHUNK, bundle_w).transpose(1, 2, 0, 3).reshape(t, SSM_WIDTH)

        h1, xn, ri, rw, cnt = _mixer_out(a, sg, y, h, lp, tm)

        counts = cnt[0, :N_EXPERTS].astype(jnp.int32)
        pcounts = (counts + EXPERT_BLOCK - 1) // EXPERT_BLOCK * EXPERT_BLOCK
        pend = jnp.cumsum(pcounts)
        pstart = (pend - pcounts).astype(jnp.int32)
        blk_first_row = jnp.arange(n_blocks, dtype=jnp.int32) * EXPERT_BLOCK
        blk_e = jnp.minimum(jnp.sum((pend[None, :] <= blk_first_row[:, None]).astype(jnp.int32), axis=1),
                            N_EXPERTS - 1)
        dest = pstart[ri[:, 0:2]] + ri[:, 2:4]
        row_tok = _slot_map(dest.reshape(t // td, 1, 2 * td), n_rows, td)
        yb = _experts(row_tok, blk_e, xn, lp["w13"], lp["w2"])
        h = _combine(dest.reshape(t // tc, 1, 2 * tc), h1, rw, p_all, i, lp, yb, tc)
    return h.reshape(batch, seq, D_MODEL)
```

```python
import functools
import math

import jax
import jax.numpy as jnp
from jax import lax
from jax.experimental import pallas as pl
from jax.experimental.pallas import tpu as pltpu

F32 = jnp.float32
BF16 = jnp.bfloat16

D_MODEL = 1024
ATTN_HEADS = 8
NOPE_DIM = 64
ROPE_DIM = 32
QK_DIM = NOPE_DIM + ROPE_DIM
V_DIM = 64
Q_RANK = 384
KV_RANK = 256
SG_GROUPS = 4
SG_CH = 64
SG_WIDTH = SG_GROUPS * SG_CH
SG_CHUNK = 128
SSM_GROUPS = 16
SSM_CH = 16
SSM_WIDTH = SSM_GROUPS * SSM_CH
SSM_STATE = 64
MOE_GROUPS = 4
EXPERTS_PER_GROUP = 8
N_EXPERTS = MOE_GROUPS * EXPERTS_PER_GROUP
D_EXPERT = 256
PLE_DIM = 256
ROPE_THETA = 10000.0
EPS = 1e-6

LANES = 128
HEAD_PAD = LANES
QKV_PAD = ATTN_HEADS * HEAD_PAD
ATTN_HEADS_PER_STEP = 2
ATTN_BLOCKS_PER_ITER = 4
SSM_CHUNK = 16
SSM_BUNDLE = 4
SSM_COLS = SSM_BUNDLE * SSM_CHUNK * SSM_CH
SSM_SLANES = SSM_BUNDLE * SSM_STATE
PACKED_ROWS = D_MODEL // 2 // LANES
ROUTE_COLS = 8
EXPERT_BLOCK = 256
EXPERT_GATHER_AHEAD = 2
SCALAR_LOOP_UNROLL = 8
VMEM_LIMIT = 56 * 1024 * 1024

C_Q = 0
C_KV = C_Q + Q_RANK
C_KRA = C_KV + KV_RANK
C_KRB = C_KRA + LANES
C_SG = C_KRB + LANES
C_SSM = C_SG + 2 * SG_WIDTH
C_END = C_SSM + SSM_WIDTH


def _dot(a, b):
    return jnp.dot(a, b, preferred_element_type=F32)


def _split_bf16(x):
    hi = x.astype(BF16)
    lo = (x - hi.astype(F32)).astype(BF16)
    return hi, lo


def _rms(x, g, width=None):
    width = x.shape[-1] if width is None else width
    ms = jnp.sum(x * x, axis=-1, keepdims=True) * (1.0 / width)
    return x * lax.rsqrt(ms + EPS) * g


def _mixer_in_kernel(h_ref, ln1_ref, win_ref, gcq_ref, gckv_ref, wqa_ref, wqb_ref, wk_ref, wv_ref,
                     cos_ref, sin_ref, gqk_ref, vones_ref, lng_ref, lnb_ref, sgw_ref, sgb_ref,
                     mmean_ref, q_out, k_out, v_out, sg_out, ssm_out, *, tm):
    xn = _rms(h_ref[...], ln1_ref[...])
    z = _dot(xn.astype(BF16), win_ref[...])

    cos = cos_ref[...]
    sin = sin_ref[...]
    gqk = gqk_ref[...]
    gqa = cos * gqk[0:1, :]
    gqb = sin * gqk[1:2, :]
    gka = cos * gqk[2:3, :]
    gkb = sin * gqk[3:4, :]
    scale = QK_DIM ** -0.5 * math.log2(math.e)

    qn = _rms(z[:, C_Q:C_KV], gcq_ref[...]).astype(BF16)
    qa = _dot(qn, wqa_ref[...])
    qb = _dot(qn, wqb_ref[...])
    for hd in range(ATTN_HEADS):
        sl = slice(hd * HEAD_PAD, (hd + 1) * HEAD_PAD)
        a = qa[:, sl]
        r = lax.rsqrt(jnp.sum(a * a, axis=-1, keepdims=True) * (1.0 / QK_DIM) + EPS)
        q_out[:, sl] = ((a * gqa + qb[:, sl] * gqb) * (r * scale)).astype(BF16)

    kvn = _rms(z[:, C_KV:C_KRA], gckv_ref[...]).astype(BF16)
    ka = _dot(kvn, wk_ref[...])
    v_out[...] = (_dot(kvn, wv_ref[...]) + vones_ref[...]).astype(BF16)
    kra = z[:, C_KRA:C_KRB]
    krb = z[:, C_KRB:C_SG]
    ss_rope = jnp.sum(kra * kra, axis=-1, keepdims=True)
    krot = kra * gka + krb * gkb
    for hd in range(ATTN_HEADS):
        sl = slice(hd * HEAD_PAD, (hd + 1) * HEAD_PAD)
        a = ka[:, sl]
        r = lax.rsqrt((jnp.sum(a * a, axis=-1, keepdims=True) + ss_rope) * (1.0 / QK_DIM) + EPS)
        k_out[:, sl] = ((a * gka + krot) * r).astype(BF16)

    zg = jax.nn.gelu(z[:, C_SG:C_SSM])
    u = zg[:, :SG_WIDTH]
    vv = zg[:, SG_WIDTH:]
    mmean = mmean_ref[...]
    vh, vl = _split_bf16(vv)
    mu = _dot(vh, mmean) + _dot(vl, mmean)
    xc = vv - mu
    sh, sl_ = _split_bf16(xc * xc)
    var = _dot(sh, mmean) + _dot(sl_, mmean)
    vn = (xc * lax.rsqrt(var + EPS) * lng_ref[...] + lnb_ref[...]).astype(BF16)
    tri = (lax.broadcasted_iota(jnp.int32, (SG_CHUNK, SG_CHUNK), 1)
           <= lax.broadcasted_iota(jnp.int32, (SG_CHUNK, SG_CHUNK), 0))
    lane_grp = lax.broadcasted_iota(jnp.int32, (SG_CHUNK, SG_WIDTH), 1) // SG_CH
    ws = [jnp.where(tri, sgw_ref[g], 0.0).astype(BF16) for g in range(SG_GROUPS)]
    for c in range(tm // SG_CHUNK):
        rows = slice(c * SG_CHUNK, (c + 1) * SG_CHUNK)
        vc = vn[rows, :]
        mixed = sgb_ref[...]
        for g in range(SG_GROUPS):
            mixed = mixed + jnp.where(lane_grp == g, _dot(ws[g], vc), 0.0)
        sg_out[rows, :] = u[rows, :] * mixed

    ssm_out[...] = z[:, C_SSM:C_END].astype(BF16)


def _mixer_in(h, lp, cos128, sin128, tm, n_pos_tiles):
    t = h.shape[0]
    const = lambda *shape: pl.BlockSpec(shape, lambda i: (0,) * len(shape))
    rows = lambda w: pl.BlockSpec((tm, w), lambda i: (i, 0))
    pos = pl.BlockSpec((tm, LANES), lambda i: (i % n_pos_tiles, 0))
    return pl.pallas_call(
        functools.partial(_mixer_in_kernel, tm=tm),
        grid=(t // tm,),
        in_specs=[rows(D_MODEL), const(1, D_MODEL), const(D_MODEL, C_END), const(1, Q_RANK),
                  const(1, KV_RANK), const(Q_RANK, QKV_PAD), const(Q_RANK, QKV_PAD),
                  const(KV_RANK, QKV_PAD), const(KV_RANK, QKV_PAD), pos, pos, const(8, LANES),
                  const(1, QKV_PAD), const(1, SG_WIDTH), const(1, SG_WIDTH),
                  const(SG_GROUPS, SG_CHUNK, SG_CHUNK), const(SG_CHUNK, SG_WIDTH),
                  const(SG_WIDTH, SG_WIDTH)],
        out_specs=[rows(QKV_PAD), rows(QKV_PAD), rows(QKV_PAD), rows(SG_WIDTH), rows(SSM_WIDTH)],
        out_shape=[jax.ShapeDtypeStruct((t, QKV_PAD), BF16), jax.ShapeDtypeStruct((t, QKV_PAD), BF16),
                   jax.ShapeDtypeStruct((t, QKV_PAD), BF16), jax.ShapeDtypeStruct((t, SG_WIDTH), F32),
                   jax.ShapeDtypeStruct((t, SSM_WIDTH), BF16)],
        compiler_params=pltpu.CompilerParams(dimension_semantics=("parallel",),
                                             vmem_limit_bytes=VMEM_LIMIT),
        name="mixer_in",
    )(h, lp["ln1"], lp["win"], lp["gcq"], lp["gckv"], lp["wqa"], lp["wqb"], lp["wk"], lp["wv"],
      cos128, sin128, lp["gqk"], lp["vones"], lp["sg_lng"], lp["sg_lnb"], lp["sg_w"], lp["sg_bias"],
      lp["mmean"])


def _attn_kernel(q_ref, k_ref, v_ref, o_ref, s_even, s_odd, m_ref, acc_ref, *, tq):
    i = pl.program_id(2)
    heads = [slice(hh * HEAD_PAD, (hh + 1) * HEAD_PAD) for hh in range(ATTN_HEADS_PER_STEP)]

    def block_rows(j):
        return pl.ds(pl.multiple_of(j * tq, tq), tq)

    def scores(j, s_ref):
        for hh, sl in enumerate(heads):
            s_ref[hh] = lax.dot_general(q_ref[:, sl], k_ref[block_rows(j), sl],
                                        (((1,), (1,)), ((), ())), preferred_element_type=F32)

    def accumulate(j, s_ref, masked):
        for hh, sl in enumerate(heads):
            s = s_ref[hh]
            if masked:
                keep = (lax.broadcasted_iota(jnp.int32, (tq, tq), 1)
                        <= lax.broadcasted_iota(jnp.int32, (tq, tq), 0))
                s = jnp.where(keep, s, -jnp.inf)
            m = m_ref[hh]
            m_new = jnp.maximum(m, jnp.max(s, axis=-1, keepdims=True))
            alpha = jnp.exp2(m - m_new)
            p = jnp.exp2(s - m_new).astype(BF16)
            m_ref[hh] = m_new
            acc_ref[hh] = alpha * acc_ref[hh] + _dot(p, v_ref[block_rows(j), sl])

    m_ref[...] = jnp.full(m_ref.shape, -jnp.inf, F32)
    acc_ref[...] = jnp.zeros(acc_ref.shape, F32)
    scores(0, s_even)

    def pipelined(t, _):
        j = ATTN_BLOCKS_PER_ITER * t
        for u in range(0, ATTN_BLOCKS_PER_ITER, 2):
            scores(j + u + 1, s_odd)
            accumulate(j + u, s_even, False)
            scores(j + u + 2, s_even)
            accumulate(j + u + 1, s_odd, False)
        return 0

    n_pipelined = i // ATTN_BLOCKS_PER_ITER
    lax.fori_loop(0, n_pipelined, pipelined, 0)

    def leftover(j, _):
        accumulate(j, s_even, False)
        scores(j + 1, s_even)
        return 0

    lax.fori_loop(n_pipelined * ATTN_BLOCKS_PER_ITER, i, leftover, 0)
    accumulate(i, s_even, True)

    lane = lax.broadcasted_iota(jnp.int32, (tq, HEAD_PAD), 1)
    for hh, sl in enumerate(heads):
        acc = acc_ref[hh]
        out = acc / acc[:, V_DIM:V_DIM + 1]
        o_ref[:, sl] = jnp.where(lane < V_DIM, out, 0.0).astype(BF16)


def _attention(q, k, v, batch, seq, tq):
    t = q.shape[0]
    nq = seq // tq
    width = ATTN_HEADS_PER_STEP * HEAD_PAD
    return pl.pallas_call(
        functools.partial(_attn_kernel, tq=tq),
        grid=(batch, ATTN_HEADS // ATTN_HEADS_PER_STEP, nq),
        in_specs=[pl.BlockSpec((tq, width), lambda b, h, i: (b * nq + i, h)),
                  pl.BlockSpec((seq, width), lambda b, h, i: (b, h)),
                  pl.BlockSpec((seq, width), lambda b, h, i: (b, h))],
        out_specs=pl.BlockSpec((tq, width), lambda b, h, i: (b * nq + i, h)),
        out_shape=jax.ShapeDtypeStruct((t, QKV_PAD), BF16),
        scratch_shapes=[pltpu.VMEM((ATTN_HEADS_PER_STEP, tq, tq), F32),
                        pltpu.VMEM((ATTN_HEADS_PER_STEP, tq, tq), F32),
                        pltpu.VMEM((ATTN_HEADS_PER_STEP, tq, 1), F32),
                        pltpu.VMEM((ATTN_HEADS_PER_STEP, tq, HEAD_PAD), F32)],
        compiler_params=pltpu.CompilerParams(
            dimension_semantics=("parallel", "parallel", "arbitrary"), vmem_limit_bytes=VMEM_LIMIT),
        name="mla_attention",
    )(q, k, v)


def _ssm_kernel(x_ref, m_ref, pre_ref, pim_ref, qre_ref, qim_ref, are_ref, aim_ref, y_ref, *,
                rows_per_batch, n_steps):
    x = x_ref[0]
    cr = x.shape[0]
    y_local = _dot(x, m_ref[0])
    sre = _dot(x, pre_ref[0])
    sim = _dot(x, pim_ref[0])
    row = lax.broadcasted_iota(jnp.int32, (cr, SSM_SLANES), 0) % rows_per_batch
    are = are_ref[0]
    aim = aim_ref[0]
    for k in range(n_steps):
        sh = 1 << k
        ok = row >= sh
        pr = jnp.where(ok, pltpu.roll(sre, sh, 0), 0.0)
        pi = jnp.where(ok, pltpu.roll(sim, sh, 0), 0.0)
        ar = are[k:k + 1, :]
        ai = aim[k:k + 1, :]
        sre, sim = sre + ar * pr - ai * pi, sim + ar * pi + ai * pr
    ok = row >= 1
    hre = jnp.where(ok, pltpu.roll(sre, 1, 0), 0.0).astype(BF16)
    him = jnp.where(ok, pltpu.roll(sim, 1, 0), 0.0).astype(BF16)
    y_ref[0] = y_local + _dot(hre, qre_ref[0]) + _dot(him, qim_ref[0])


def _ssm(xp, lp, rows_per_batch):
    n_bundles, cr, _ = xp.shape
    n_steps = lp["ssm_are"].shape[1]
    blk = lambda *shape: pl.BlockSpec((1,) + shape, lambda p: (p,) + (0,) * len(shape))
    return pl.pallas_call(
        functools.partial(_ssm_kernel, rows_per_batch=rows_per_batch, n_steps=n_steps),
        grid=(n_bundles,),
        in_specs=[blk(cr, SSM_COLS), blk(SSM_COLS, SSM_COLS), blk(SSM_COLS, SSM_SLANES),
                  blk(SSM_COLS, SSM_SLANES), blk(SSM_SLANES, SSM_COLS), blk(SSM_SLANES, SSM_COLS),
                  blk(n_steps, SSM_SLANES), blk(n_steps, SSM_SLANES)],
        out_specs=blk(cr, SSM_COLS),
        out_shape=jax.ShapeDtypeStruct((n_bundles, cr, SSM_COLS), F32),
        compiler_params=pltpu.CompilerParams(dimension_semantics=("parallel",),
                                             vmem_limit_bytes=VMEM_LIMIT),
        name="s5_mixer",
    )(xp, lp["ssm_m"], lp["ssm_pre"], lp["ssm_pim"], lp["ssm_qre"], lp["ssm_qim"], lp["ssm_are"],
      lp["ssm_aim"])


def _mixer_out_kernel(a_ref, sg_ref, y_ref, h_ref, gma_ref, gmg_ref, gms_ref, woa_ref, wog_ref,
                      wos_ref, wglu_ref, bglu_ref, ln2_ref, wrh_ref, wrl_ref, br_ref, ltri_ref,
                      h1_out, xn_out, ri_out, rw_out, cnt_out, run_ref):
    i = pl.program_id(0)

    @pl.when(i == 0)
    def _():
        run_ref[...] = jnp.zeros_like(run_ref)

    g = jax.nn.gelu(y_ref[...])
    glu = g * jax.nn.sigmoid(_dot(g.astype(BF16), wglu_ref[...]) + bglu_ref[...])
    a_n = _rms(a_ref[...].astype(F32), gma_ref[...], ATTN_HEADS * V_DIM)
    g_n = _rms(sg_ref[...], gmg_ref[...])
    s_n = _rms(glu, gms_ref[...])
    h1 = (h_ref[...] + _dot(a_n.astype(BF16), woa_ref[...]) + _dot(g_n.astype(BF16), wog_ref[...])
          + _dot(s_n.astype(BF16), wos_ref[...]))
    h1_out[...] = h1
    xn = _rms(h1, ln2_ref[...])
    _store_token_rows(xn_out, _pack_bf16_pairs(xn))

    xh, xl = _split_bf16(xn)
    logit = _dot(xh, wrh_ref[...]) + _dot(xl, wrh_ref[...]) + _dot(xh, wrl_ref[...]) + br_ref[...]
    shape = logit.shape
    lane = lax.broadcasted_iota(jnp.int32, shape, 1)
    neg = -jnp.inf
    is_grp = lane < MOE_GROUPS
    gl = jnp.where(is_grp, logit, neg)
    gmax = jnp.max(gl, axis=-1, keepdims=True)
    gsel = jnp.min(jnp.where(gl == gmax, lane, LANES), axis=-1, keepdims=True)
    gsum = jnp.sum(jnp.where(is_grp, jnp.exp(logit - gmax), 0.0), axis=-1, keepdims=True)
    g_w = 1.0 / gsum
    lo = MOE_GROUPS + gsel * EXPERTS_PER_GROUP
    in_grp = (lane >= lo) & (lane < lo + EXPERTS_PER_GROUP)
    el = jnp.where(in_grp, logit, neg)
    m1 = jnp.max(el, axis=-1, keepdims=True)
    i1 = jnp.min(jnp.where(el == m1, lane, LANES), axis=-1, keepdims=True)
    el2 = jnp.where(lane == i1, neg, el)
    m2 = jnp.max(el2, axis=-1, keepdims=True)
    i2 = jnp.min(jnp.where(el2 == m2, lane, LANES), axis=-1, keepdims=True)
    e2 = jnp.exp(m2 - m1)
    w1 = g_w / (1.0 + e2)
    w2 = g_w * e2 / (1.0 + e2)
    e_a = i1 - MOE_GROUPS
    e_b = i2 - MOE_GROUPS

    hit_a = lane == e_a
    hit_b = lane == e_b
    onehot = jnp.where(hit_a | hit_b, 1.0, 0.0).astype(BF16)
    before = _dot(ltri_ref[...], onehot) + run_ref[...]
    c_a = jnp.sum(jnp.where(hit_a, before, 0.0), axis=-1, keepdims=True).astype(jnp.int32)
    c_b = jnp.sum(jnp.where(hit_b, before, 0.0), axis=-1, keepdims=True).astype(jnp.int32)
    run_ref[...] = run_ref[...] + jnp.sum(onehot.astype(F32), axis=0, keepdims=True)
    cnt_out[...] = jnp.broadcast_to(run_ref[...], cnt_out.shape)

    ri_out[...] = jnp.where(lane == 0, e_a, jnp.where(lane == 1, e_b, jnp.where(
        lane == 2, c_a, jnp.where(lane == 3, c_b, 0))))[:, :ROUTE_COLS]
    rw_out[...] = jnp.where(lane == 0, w1, jnp.where(lane == 1, w2, 0.0))[:, :ROUTE_COLS]


def _mixer_out(a, sg, y, h, lp, tm):
    t = h.shape[0]
    const = lambda *shape: pl.BlockSpec(shape, lambda i: (0,) * len(shape))
    rows = lambda w: pl.BlockSpec((tm, w), lambda i: (i, 0))
    return pl.pallas_call(
        _mixer_out_kernel,
        grid=(t // tm,),
        in_specs=[rows(QKV_PAD), rows(SG_WIDTH), rows(SSM_WIDTH), rows(D_MODEL), const(1, QKV_PAD),
                  const(1, SG_WIDTH), const(1, SSM_WIDTH), const(QKV_PAD, D_MODEL),
                  const(SG_WIDTH, D_MODEL), const(SSM_WIDTH, D_MODEL), const(SSM_WIDTH, SSM_WIDTH),
                  const(1, SSM_WIDTH), const(1, D_MODEL), const(D_MODEL, LANES), const(D_MODEL, LANES),
                  const(1, LANES), const(tm, tm)],
        out_specs=[rows(D_MODEL), pl.BlockSpec((tm * PACKED_ROWS, LANES), lambda i: (i, 0)),
                   rows(ROUTE_COLS), rows(ROUTE_COLS), const(8, LANES)],
        out_shape=[jax.ShapeDtypeStruct((t, D_MODEL), F32),
                   jax.ShapeDtypeStruct((t * PACKED_ROWS, LANES), jnp.uint32),
                   jax.ShapeDtypeStruct((t, ROUTE_COLS), jnp.int32),
                   jax.ShapeDtypeStruct((t, ROUTE_COLS), F32),
                   jax.ShapeDtypeStruct((8, LANES), F32)],
        scratch_shapes=[pltpu.VMEM((1, LANES), F32)],
        compiler_params=pltpu.CompilerParams(dimension_semantics=("arbitrary",),
                                             vmem_limit_bytes=VMEM_LIMIT),
        name="mixer_out_router",
    )(a, sg, y, h, lp["gm_a"], lp["gm_g"], lp["gm_s"], lp["wo_a"], lp["wo_g"], lp["wo_s"], lp["wglu"],
      lp["bglu"], lp["ln2"], lp["wr_hi"], lp["wr_lo"], lp["br"], lp["ltri"])


def _row_copy(src_ref, src_row, dst_ref, dst_row, sem):
    src = src_ref.at[pl.ds(pl.multiple_of(src_row * PACKED_ROWS, PACKED_ROWS), PACKED_ROWS)]
    return pltpu.make_async_copy(src, dst_ref.at[pl.ds(dst_row * PACKED_ROWS, PACKED_ROWS)], sem)


def _rows_wait(src_ref, dst_ref, n_tokens, sem):
    rows = pl.ds(0, n_tokens * PACKED_ROWS)
    pltpu.make_async_copy(src_ref.at[rows], dst_ref.at[rows], sem).wait()


def _store_token_rows(ref, words):
    n = words.shape[0]
    for c in range(PACKED_ROWS):
        ref[pl.ds(c, n, stride=PACKED_ROWS), :] = words[:, c * LANES:(c + 1) * LANES]


def _load_token_rows(ref):
    n = ref.shape[0] // PACKED_ROWS
    return jnp.concatenate([ref[pl.ds(c, n, stride=PACKED_ROWS), :] for c in range(PACKED_ROWS)], axis=1)


def _slot_map_kernel(dest_ref, zeros_ref, row_tok_ref, sem, *, td):
    i = pl.program_id(0)

    @pl.when(i == 0)
    def _():
        clear = pltpu.make_async_copy(zeros_ref, row_tok_ref, sem)
        clear.start()
        clear.wait()

    def place(r, _):
        row_tok_ref[dest_ref[0, 0, 2 * r]] = i * td + r
        row_tok_ref[dest_ref[0, 0, 2 * r + 1]] = i * td + r
        return 0

    lax.fori_loop(0, td, place, 0, unroll=SCALAR_LOOP_UNROLL)


def _slot_map(dest, n_rows, td):
    n_tiles = dest.shape[0]
    return pl.pallas_call(
        functools.partial(_slot_map_kernel, td=td),
        grid=(n_tiles,),
        in_specs=[pl.BlockSpec((1, 1, 2 * td), lambda i: (i, 0, 0), memory_space=pltpu.SMEM),
                  pl.BlockSpec(memory_space=pl.ANY)],
        out_specs=pl.BlockSpec(memory_space=pltpu.SMEM),
        out_shape=jax.ShapeDtypeStruct((n_rows,), jnp.int32),
        scratch_shapes=[pltpu.SemaphoreType.DMA(())],
        compiler_params=pltpu.CompilerParams(dimension_semantics=("arbitrary",)),
        name="moe_slot_map",
    )(dest, jnp.zeros((n_rows,), jnp.int32))


def _unpack_bf16_pairs(u, dtype):
    lo = lax.bitcast_convert_type(u << 16, F32)
    hi = lax.bitcast_convert_type(u & jnp.uint32(0xFFFF0000), F32)
    return jnp.concatenate([lo, hi], axis=1).astype(dtype)


def _pack_bf16_pairs(x):
    n = x.shape[1] // 2
    bits = lax.bitcast_convert_type(x.astype(BF16).astype(F32), jnp.uint32)
    return (bits[:, :n] >> 16) | (bits[:, n:] & jnp.uint32(0xFFFF0000))


def _expert_kernel(row_tok_ref, blk_e_ref, x_ref, w13_ref, w2_ref, yb_ref, xbuf, sem):
    del blk_e_ref
    b = pl.program_id(0)
    last = pl.num_programs(0) - 1
    n_buf = EXPERT_GATHER_AHEAD + 1

    def gather(block, to_slot):
        base = block * EXPERT_BLOCK
        for r in range(EXPERT_BLOCK):
            _row_copy(x_ref, row_tok_ref[base + r], xbuf.at[to_slot], r,
                      sem.at[to_slot]).start(priority=r % 2)

    def wait(of_slot):
        _rows_wait(x_ref, xbuf.at[of_slot], EXPERT_BLOCK, sem.at[of_slot])

    @pl.when(b == 0)
    def _():
        for ahead in range(EXPERT_GATHER_AHEAD):
            gather(ahead, ahead)

    gather(jnp.minimum(b + EXPERT_GATHER_AHEAD, last), (b + EXPERT_GATHER_AHEAD) % n_buf)
    slot = b % n_buf
    wait(slot)
    hh = _dot(_unpack_bf16_pairs(_load_token_rows(xbuf.at[slot]), BF16), w13_ref[...])
    hb = jax.nn.silu(hh[:, :D_EXPERT]) * hh[:, D_EXPERT:]
    _store_token_rows(yb_ref, _pack_bf16_pairs(_dot(hb.astype(BF16), w2_ref[...])))

    @pl.when(b == last)
    def _():
        for ahead in range(1, n_buf):
            wait((b + ahead) % n_buf)


def _experts(row_tok, blk_e, xn_packed, w13, w2):
    n_rows = row_tok.shape[0]
    grid_spec = pltpu.PrefetchScalarGridSpec(
        num_scalar_prefetch=2,
        grid=(n_rows // EXPERT_BLOCK,),
        in_specs=[pl.BlockSpec(memory_space=pl.ANY),
                  pl.BlockSpec((None, D_MODEL, 2 * D_EXPERT), lambda i, rt, be: (be[i], 0, 0)),
                  pl.BlockSpec((None, D_EXPERT, D_MODEL), lambda i, rt, be: (be[i], 0, 0))],
        out_specs=pl.BlockSpec((EXPERT_BLOCK * PACKED_ROWS, LANES), lambda i, rt, be: (i, 0)),
        scratch_shapes=[pltpu.VMEM((EXPERT_GATHER_AHEAD + 1, EXPERT_BLOCK * PACKED_ROWS, LANES), jnp.uint32),
                        pltpu.SemaphoreType.DMA((EXPERT_GATHER_AHEAD + 1,))],
    )
    return pl.pallas_call(
        _expert_kernel,
        grid_spec=grid_spec,
        out_shape=jax.ShapeDtypeStruct((n_rows * PACKED_ROWS, LANES), jnp.uint32),
        compiler_params=pltpu.CompilerParams(dimension_semantics=("arbitrary",),
                                             vmem_limit_bytes=VMEM_LIMIT),
        name="moe_experts",
    )(row_tok, blk_e, xn_packed, w13, w2)


def _combine_kernel(dest_ref, dest_next_ref, h_ref, rw_ref, p_ref, wpg_ref, bpg_ref,
                    wple_ref, yb_ref, out_ref, y0_ref, y1_ref, sem, *, tc):
    i = pl.program_id(0)
    last = pl.num_programs(0) - 1
    slot = i % 2

    def gather(tile_dest_ref, to_slot):
        for r in range(tc):
            _row_copy(yb_ref, tile_dest_ref[0, 0, 2 * r], y0_ref.at[to_slot], r, sem.at[to_slot]).start()
            _row_copy(yb_ref, tile_dest_ref[0, 0, 2 * r + 1], y1_ref.at[to_slot], r,
                      sem.at[to_slot]).start()

    def wait(of_slot):
        _rows_wait(yb_ref, y0_ref.at[of_slot], tc, sem.at[of_slot])
        _rows_wait(yb_ref, y1_ref.at[of_slot], tc, sem.at[of_slot])

    @pl.when(i == 0)
    def _():
        gather(dest_ref, 0)

    gather(dest_next_ref, 1 - slot)
    wait(slot)

    rw = rw_ref[...]
    y0 = _unpack_bf16_pairs(_load_token_rows(y0_ref.at[slot]), F32)
    y1 = _unpack_bf16_pairs(_load_token_rows(y1_ref.at[slot]), F32)
    h2 = h_ref[...] + rw[:, 0:1] * y0 + rw[:, 1:2] * y1
    gate = jax.nn.sigmoid(_dot(h2.astype(BF16), wpg_ref[...]) + bpg_ref[...])
    out_ref[...] = h2 + gate * _dot(p_ref[...].astype(BF16), wple_ref[...])

    @pl.when(i == last)
    def _():
        wait(1 - slot)


def _combine(dest, h1, rw, p_all, layer, lp, yb, tc):
    t = h1.shape[0]
    n_tiles = t // tc
    const = lambda *shape: pl.BlockSpec(shape, lambda i: (0,) * len(shape))
    rows = lambda w: pl.BlockSpec((tc, w), lambda i: (i, 0))
    p_rows = pl.BlockSpec((tc, PLE_DIM), lambda i: (layer * n_tiles + i, 0))
    grid_spec = pl.GridSpec(
        grid=(n_tiles,),
        in_specs=[pl.BlockSpec((1, 1, 2 * tc), lambda i: (i, 0, 0), memory_space=pltpu.SMEM),
                  pl.BlockSpec((1, 1, 2 * tc), lambda i: (jnp.minimum(i + 1, n_tiles - 1), 0, 0),
                               memory_space=pltpu.SMEM),
                  rows(D_MODEL), rows(ROUTE_COLS), p_rows, const(D_MODEL, D_MODEL), const(1, D_MODEL),
                  const(PLE_DIM, D_MODEL), pl.BlockSpec(memory_space=pl.ANY)],
        out_specs=rows(D_MODEL),
        scratch_shapes=[pltpu.VMEM((2, tc * PACKED_ROWS, LANES), jnp.uint32),
                        pltpu.VMEM((2, tc * PACKED_ROWS, LANES), jnp.uint32),
                        pltpu.SemaphoreType.DMA((2,))],
    )
    return pl.pallas_call(
        functools.partial(_combine_kernel, tc=tc),
        grid_spec=grid_spec,
        out_shape=jax.ShapeDtypeStruct((t, D_MODEL), F32),
        compiler_params=pltpu.CompilerParams(dimension_semantics=("arbitrary",),
                                             vmem_limit_bytes=VMEM_LIMIT),
        name="moe_combine_ple",
    )(dest, dest, h1, rw, p_all, lp["wpg"], lp["bpg"], lp["wple"], yb)


def _rope_lane_tables(seq):
    inv = 1.0 / (ROPE_THETA ** (jnp.arange(0, ROPE_DIM, 2, dtype=F32) / ROPE_DIM))
    ang = jnp.arange(seq, dtype=F32)[:, None] * inv[None, :]
    cos, sin = jnp.cos(ang), jnp.sin(ang)
    half = ROPE_DIM // 2
    cos128 = jnp.concatenate([jnp.ones((seq, NOPE_DIM), F32), cos, cos,
                              jnp.zeros((seq, LANES - QK_DIM), F32)], axis=1)
    sin128 = jnp.concatenate([jnp.zeros((seq, NOPE_DIM), F32), -sin, sin,
                              jnp.zeros((seq, LANES - QK_DIM), F32)], axis=1)
    del half
    return cos128, sin128


def _swap_rope_halves(w):
    half = ROPE_DIM // 2
    return jnp.concatenate([w[..., half:], w[..., :half]], axis=-1)


def _pad_heads(w, width):
    return jnp.pad(w, ((0, 0), (0, 0), (0, HEAD_PAD - width))).reshape(w.shape[0], QKV_PAD)


def _rope_gain_rows(g):
    pad = jnp.zeros((LANES - QK_DIM,), F32)
    ga = jnp.concatenate([g, pad])
    gb = jnp.concatenate([jnp.zeros((NOPE_DIM,), F32), _swap_rope_halves(g[NOPE_DIM:]), pad])
    return ga, gb


def _ssm_tables(lam_re, lam_im, log_dt, b_re, b_im, c_re, c_im, d_skip, n_steps):
    hp = lax.Precision.HIGHEST
    lr = jnp.minimum(lam_re, -1e-4)
    li = lam_im
    dt = jnp.exp(log_dt)
    mag = jnp.exp(lr * dt)
    ab_re, ab_im = mag * jnp.cos(li * dt), mag * jnp.sin(li * dt)
    den = lr * lr + li * li
    nr, ni = ab_re - 1.0, ab_im
    fr = (nr * lr + ni * li) / den
    fi = (ni * lr - nr * li) / den
    bb_re = fr[..., None] * b_re - fi[..., None] * b_im
    bb_im = fr[..., None] * b_im + fi[..., None] * b_re

    steps = jnp.arange(SSM_CHUNK + 1, dtype=F32)[:, None, None]
    pmag = jnp.exp(lr * dt * steps)
    pw_re, pw_im = pmag * jnp.cos(li * dt * steps), pmag * jnp.sin(li * dt * steps)

    w_re = pw_re[:SSM_CHUNK, :, :, None] * bb_re - pw_im[:SSM_CHUNK, :, :, None] * bb_im
    w_im = pw_re[:SSM_CHUNK, :, :, None] * bb_im + pw_im[:SSM_CHUNK, :, :, None] * bb_re
    taps = (jnp.einsum("gcn,tgnd->tgcd", c_re, w_re, precision=hp)
            - jnp.einsum("gcn,tgnd->tgcd", c_im, w_im, precision=hp))
    taps = taps.at[0].add(d_skip[:, :, None] * jnp.eye(SSM_CH, dtype=F32)[None])
    t_in = jnp.arange(SSM_CHUNK)[:, None]
    t_out = jnp.arange(SSM_CHUNK)[None, :]
    lag = t_out - t_in
    m = jnp.where((lag >= 0)[:, :, None, None, None], taps[jnp.clip(lag, 0)], 0.0)
    m = m.transpose(2, 0, 4, 1, 3)

    p_re = w_re[::-1].transpose(1, 0, 3, 2)
    p_im = w_im[::-1].transpose(1, 0, 3, 2)

    up_re, up_im = pw_re[1:], pw_im[1:]
    q_re = (c_re[None] * up_re[:, :, None, :] - c_im[None] * up_im[:, :, None, :])
    q_im = -(c_re[None] * up_im[:, :, None, :] + c_im[None] * up_re[:, :, None, :])
    q_re = q_re.transpose(1, 3, 0, 2)
    q_im = q_im.transpose(1, 3, 0, 2)

    are, aim = [pw_re[SSM_CHUNK]], [pw_im[SSM_CHUNK]]
    for _ in range(n_steps - 1):
        r, im = are[-1], aim[-1]
        are.append(r * r - im * im)
        aim.append(2.0 * r * im)
    are, aim = jnp.stack(are, axis=1), jnp.stack(aim, axis=1)

    nbu = SSM_GROUPS // SSM_BUNDLE
    per_group = SSM_CHUNK * SSM_CH
    bundle = lambda x: x.reshape((nbu, SSM_BUNDLE) + x.shape[1:])
    col = jnp.arange(SSM_COLS)
    col_src = (col // (SSM_BUNDLE * SSM_CH)) * SSM_CH + col % SSM_CH
    col_grp = (col // SSM_CH) % SSM_BUNDLE
    sel_io = ((col_grp[None, None, :] == jnp.arange(SSM_BUNDLE)[:, None, None])
              & (col_src[None, None, :] == jnp.arange(per_group)[None, :, None])).astype(F32)
    lane = jnp.arange(SSM_SLANES)
    sel_st = ((lane[None, None, :] // SSM_STATE == jnp.arange(SSM_BUNDLE)[:, None, None])
              & (lane[None, None, :] % SSM_STATE == jnp.arange(SSM_STATE)[None, :, None])).astype(F32)

    def expand(x, rows_sel, cols_sel):
        return jnp.einsum("gaR,qgab,gbS->qRS", rows_sel, bundle(x), cols_sel).astype(BF16)

    def lanes(x):
        return bundle(x).transpose(0, 2, 1, 3).reshape(nbu, x.shape[1], SSM_SLANES)

    flat_io = lambda x: x.reshape(SSM_GROUPS, per_group, -1)
    return dict(ssm_m=expand(m.reshape(SSM_GROUPS, per_group, per_group), sel_io, sel_io),
                ssm_pre=expand(flat_io(p_re), sel_io, sel_st), ssm_pim=expand(flat_io(p_im), sel_io, sel_st),
                ssm_qre=expand(q_re.reshape(SSM_GROUPS, SSM_STATE, per_group), sel_st, sel_io),
                ssm_qim=expand(q_im.reshape(SSM_GROUPS, SSM_STATE, per_group), sel_st, sel_io),
                ssm_are=lanes(are), ssm_aim=lanes(aim))


def _layer_params(i, tm, n_scan_steps, ln1, w_in, g_cq, g_ckv, w_uq, w_ukv, g_qh, g_kh, sg_ln_g,
                  sg_ln_b, sg_w, sg_b, lam_re, lam_im, log_dt, ssm_b_re, ssm_b_im, ssm_c_re, ssm_c_im,
                  ssm_d, w_glu, b_glu, g_mix, w_out, ln2, w_rg, b_rg, w_rexp, b_rexp, w1, w3, w2, w_pg,
                  b_pg, w_ple):
    lp = {}
    wi = w_in[i]
    off_kr = Q_RANK + KV_RANK
    off_sg = off_kr + ROPE_DIM
    kr = wi[:, off_kr:off_sg]
    zpad = lambda n: jnp.zeros((D_MODEL, n), F32)
    kra = jnp.concatenate([zpad(NOPE_DIM), kr, zpad(LANES - QK_DIM)], axis=1)
    krb = jnp.concatenate([zpad(NOPE_DIM), _swap_rope_halves(kr), zpad(LANES - QK_DIM)], axis=1)
    lp["win"] = jnp.concatenate([wi[:, :off_kr], kra, krb, wi[:, off_sg:]], axis=1).astype(BF16)
    lp["ln1"] = ln1[i][None]
    lp["gcq"] = g_cq[i][None]
    lp["gckv"] = g_ckv[i][None]

    wq = w_uq[i].reshape(Q_RANK, ATTN_HEADS, QK_DIM)
    lp["wqa"] = _pad_heads(wq, QK_DIM).astype(BF16)
    wq_sw = jnp.concatenate([jnp.zeros((Q_RANK, ATTN_HEADS, NOPE_DIM), F32),
                             _swap_rope_halves(wq[..., NOPE_DIM:])], axis=-1)
    lp["wqb"] = _pad_heads(wq_sw, QK_DIM).astype(BF16)
    wkv = w_ukv[i].reshape(KV_RANK, ATTN_HEADS, NOPE_DIM + V_DIM)
    lp["wk"] = _pad_heads(wkv[..., :NOPE_DIM], NOPE_DIM).astype(BF16)
    lp["wv"] = _pad_heads(wkv[..., NOPE_DIM:], V_DIM).astype(BF16)
    gq_a, gq_b = _rope_gain_rows(g_qh[i])
    gk_a, gk_b = _rope_gain_rows(g_kh[i])
    lp["gqk"] = jnp.concatenate([jnp.stack([gq_a, gq_b, gk_a, gk_b]), jnp.zeros((4, LANES), F32)])
    lp["vones"] = jnp.tile((jnp.arange(HEAD_PAD) == V_DIM).astype(F32), ATTN_HEADS)[None]

    lp["sg_lng"] = sg_ln_g[i][None]
    lp["sg_lnb"] = sg_ln_b[i][None]
    lp["sg_w"] = sg_w[i]
    lp["sg_bias"] = jnp.repeat(sg_b[i].T, SG_CH, axis=1)
    grp = jnp.arange(SG_WIDTH) // SG_CH
    lp["mmean"] = ((grp[:, None] == grp[None, :]).astype(F32) / SG_CH).astype(BF16)

    lp.update(_ssm_tables(lam_re[i], lam_im[i], log_dt[i], ssm_b_re[i], ssm_b_im[i], ssm_c_re[i],
                          ssm_c_im[i], ssm_d[i], n_scan_steps))

    gm = g_mix[i]
    aw = ATTN_HEADS * V_DIM
    lp["gm_a"] = _pad_heads(gm[:aw].reshape(1, ATTN_HEADS, V_DIM), V_DIM)
    lp["gm_g"] = gm[aw:aw + SG_WIDTH][None]
    lp["gm_s"] = gm[aw + SG_WIDTH:][None]
    wo = w_out[i]
    woa = wo[:aw].reshape(ATTN_HEADS, V_DIM, D_MODEL)
    lp["wo_a"] = jnp.pad(woa, ((0, 0), (0, HEAD_PAD - V_DIM), (0, 0))).reshape(QKV_PAD, D_MODEL).astype(BF16)
    lp["wo_g"] = wo[aw:aw + SG_WIDTH].astype(BF16)
    lp["wo_s"] = wo[aw + SG_WIDTH:].astype(BF16)
    lp["wglu"] = w_glu[i].astype(BF16)
    lp["bglu"] = b_glu[i][None]
    lp["ln2"] = ln2[i][None]
    wr = jnp.concatenate([w_rg[i], w_rexp[i], jnp.zeros((D_MODEL, LANES - MOE_GROUPS - N_EXPERTS), F32)],
                         axis=1)
    lp["wr_hi"], lp["wr_lo"] = _split_bf16(wr)
    lp["br"] = jnp.concatenate([b_rg[i], b_rexp[i], jnp.zeros((LANES - MOE_GROUPS - N_EXPERTS,), F32)])[None]
    lp["ltri"] = (jnp.arange(tm)[None, :] < jnp.arange(tm)[:, None]).astype(BF16)

    lp["w13"] = jnp.concatenate([w1[i], w3[i]], axis=-1).astype(BF16)
    lp["w2"] = w2[i].astype(BF16)
    lp["wpg"] = w_pg[i].astype(BF16)
    lp["bpg"] = b_pg[i][None]
    lp["wple"] = w_ple[i].astype(BF16)
    return lp


def kernel(x, p, ln1, w_in, g_cq, g_ckv, w_uq, w_ukv, g_qh, g_kh, sg_ln_g, sg_ln_b, sg_w, sg_b, lam_re, lam_im, log_dt, ssm_b_re, ssm_b_im, ssm_c_re, ssm_c_im, ssm_d, w_glu, b_glu, g_mix, w_out, ln2, w_rg, b_rg, w_rexp, b_rexp, w1, w3, w2, w_pg, b_pg, w_ple):
    batch, seq, _ = x.shape
    depth = p.shape[0]
    t = batch * seq
    tm = min(512, seq)
    tq = min(512, seq)
    td = min(1024, seq)
    tc = min(256, seq)
    assert seq % tm == 0 and seq % SSM_CHUNK == 0 and tm % SG_CHUNK == 0
    rows_per_batch = seq // SSM_CHUNK
    n_scan_steps = max(1, (rows_per_batch - 1).bit_length())
    cr = t // SSM_CHUNK
    n_bundles = SSM_GROUPS // SSM_BUNDLE
    bundle_w = SSM_BUNDLE * SSM_CH
    tk2 = 2 * t
    n_blocks = -(-(tk2 + N_EXPERTS * (EXPERT_BLOCK - 1)) // EXPERT_BLOCK)
    n_rows = n_blocks * EXPERT_BLOCK

    cos128, sin128 = _rope_lane_tables(seq)
    h = x.reshape(t, D_MODEL)
    p_all = p.reshape(depth * t, PLE_DIM)
    weights = (ln1, w_in, g_cq, g_ckv, w_uq, w_ukv, g_qh, g_kh, sg_ln_g, sg_ln_b, sg_w, sg_b, lam_re,
               lam_im, log_dt, ssm_b_re, ssm_b_im, ssm_c_re, ssm_c_im, ssm_d, w_glu, b_glu, g_mix, w_out,
               ln2, w_rg, b_rg, w_rexp, b_rexp, w1, w3, w2, w_pg, b_pg, w_ple)
    for i in range(depth):
        lp = _layer_params(i, tm, n_scan_steps, *weights)
        q, k, v, sg, xs_ssm = _mixer_in(h, lp, cos128, sin128, tm, seq // tm)
        a = _attention(q, k, v, batch, seq, tq)

        xp = xs_ssm.reshape(cr, SSM_CHUNK, n_bundles, bundle_w).transpose(2, 0, 1, 3)
        yp = _ssm(xp.reshape(n_bundles, cr, SSM_COLS), lp, rows_per_batch)
        y = yp.reshape(n_bundles, cr, SSM_CHUNK, bundle_w).transpose(1, 2, 0, 3).reshape(t, SSM_WIDTH)

        h1, xn, ri, rw, cnt = _mixer_out(a, sg, y, h, lp, tm)

        counts = cnt[0, :N_EXPERTS].astype(jnp.int32)
        pcounts = (counts + EXPERT_BLOCK - 1) // EXPERT_BLOCK * EXPERT_BLOCK
        pend = jnp.cumsum(pcounts)
        pstart = (pend - pcounts).astype(jnp.int32)
        blk_first_row = jnp.arange(n_blocks, dtype=jnp.int32) * EXPERT_BLOCK
        blk_e = jnp.minimum(jnp.sum((pend[None, :] <= blk_first_row[:, None]).astype(jnp.int32), axis=1),
                            N_EXPERTS - 1)
        is_e = ri[:, 0:2, None] == jnp.arange(N_EXPERTS, dtype=jnp.int32)
        dest = jnp.sum(jnp.where(is_e, pstart, 0), axis=-1) + ri[:, 2:4]
        row_tok = _slot_map(dest.reshape(t // td, 1, 2 * td), n_rows, td)
        yb = _experts(row_tok, blk_e, xn, lp["w13"], lp["w2"])
        h = _combine(dest.reshape(t // tc, 1, 2 * tc), h1, rw, p_all, i, lp, yb, tc)
    return h.reshape(batch, seq, D_MODEL)
```

```python
import functools
import math

import jax
import jax.numpy as jnp
from jax import lax
from jax.experimental import pallas as pl
from jax.experimental.pallas import tpu as pltpu

F32 = jnp.float32
BF16 = jnp.bfloat16

D_MODEL = 1024
ATTN_HEADS = 8
NOPE_DIM = 64
ROPE_DIM = 32
QK_DIM = NOPE_DIM + ROPE_DIM
V_DIM = 64
Q_RANK = 384
KV_RANK = 256
SG_GROUPS = 4
SG_CH = 64
SG_WIDTH = SG_GROUPS * SG_CH
SG_CHUNK = 128
SSM_GROUPS = 16
SSM_CH = 16
SSM_WIDTH = SSM_GROUPS * SSM_CH
SSM_STATE = 64
MOE_GROUPS = 4
EXPERTS_PER_GROUP = 8
N_EXPERTS = MOE_GROUPS * EXPERTS_PER_GROUP
D_EXPERT = 256
PLE_DIM = 256
ROPE_THETA = 10000.0
EPS = 1e-6

LANES = 128
HEAD_PAD = LANES
QKV_PAD = ATTN_HEADS * HEAD_PAD
ATTN_HEADS_PER_STEP = 2
ATTN_BLOCKS_PER_ITER = 4
SSM_CHUNK = 16
SSM_BUNDLE = 4
SSM_COLS = SSM_BUNDLE * SSM_CHUNK * SSM_CH
SSM_SLANES = SSM_BUNDLE * SSM_STATE
PACKED_ROWS = D_MODEL // 2 // LANES
ROUTE_COLS = 8
EXPERT_BLOCK = 512
EXPERT_GATHER_AHEAD = 2
SCALAR_LOOP_UNROLL = 8
VMEM_LIMIT = 56 * 1024 * 1024

C_Q = 0
C_KV = C_Q + Q_RANK
C_KRA = C_KV + KV_RANK
C_KRB = C_KRA + LANES
C_SG = C_KRB + LANES
C_SSM = C_SG + 2 * SG_WIDTH
C_END = C_SSM + SSM_WIDTH


def _dot(a, b):
    return jnp.dot(a, b, preferred_element_type=F32)


def _split_bf16(x):
    hi = x.astype(BF16)
    lo = (x - hi.astype(F32)).astype(BF16)
    return hi, lo


def _rms(x, g, width=None):
    width = x.shape[-1] if width is None else width
    ms = jnp.sum(x * x, axis=-1, keepdims=True) * (1.0 / width)
    return x * lax.rsqrt(ms + EPS) * g


def _mixer_in_kernel(h_ref, ln1_ref, win_ref, gcq_ref, gckv_ref, wqa_ref, wqb_ref, wk_ref, wv_ref,
                     cos_ref, sin_ref, gqk_ref, vones_ref, lng_ref, lnb_ref, sgw_ref, sgb_ref,
                     mmean_ref, q_out, k_out, v_out, sg_out, ssm_out, *, tm):
    xn = _rms(h_ref[...], ln1_ref[...])
    z = _dot(xn.astype(BF16), win_ref[...])

    cos = cos_ref[...]
    sin = sin_ref[...]
    gqk = gqk_ref[...]
    gqa = cos * gqk[0:1, :]
    gqb = sin * gqk[1:2, :]
    gka = cos * gqk[2:3, :]
    gkb = sin * gqk[3:4, :]
    scale = QK_DIM ** -0.5 * math.log2(math.e)

    qn = _rms(z[:, C_Q:C_KV], gcq_ref[...]).astype(BF16)
    qa = _dot(qn, wqa_ref[...])
    qb = _dot(qn, wqb_ref[...])
    for hd in range(ATTN_HEADS):
        sl = slice(hd * HEAD_PAD, (hd + 1) * HEAD_PAD)
        a = qa[:, sl]
        r = lax.rsqrt(jnp.sum(a * a, axis=-1, keepdims=True) * (1.0 / QK_DIM) + EPS)
        q_out[:, sl] = ((a * gqa + qb[:, sl] * gqb) * (r * scale)).astype(BF16)

    kvn = _rms(z[:, C_KV:C_KRA], gckv_ref[...]).astype(BF16)
    ka = _dot(kvn, wk_ref[...])
    v_out[...] = (_dot(kvn, wv_ref[...]) + vones_ref[...]).astype(BF16)
    kra = z[:, C_KRA:C_KRB]
    krb = z[:, C_KRB:C_SG]
    ss_rope = jnp.sum(kra * kra, axis=-1, keepdims=True)
    krot = kra * gka + krb * gkb
    for hd in range(ATTN_HEADS):
        sl = slice(hd * HEAD_PAD, (hd + 1) * HEAD_PAD)
        a = ka[:, sl]
        r = lax.rsqrt((jnp.sum(a * a, axis=-1, keepdims=True) + ss_rope) * (1.0 / QK_DIM) + EPS)
        k_out[:, sl] = ((a * gka + krot) * r).astype(BF16)

    zg = jax.nn.gelu(z[:, C_SG:C_SSM])
    u = zg[:, :SG_WIDTH]
    vv = zg[:, SG_WIDTH:]
    mmean = mmean_ref[...]
    vh, vl = _split_bf16(vv)
    mu = _dot(vh, mmean) + _dot(vl, mmean)
    xc = vv - mu
    sh, sl_ = _split_bf16(xc * xc)
    var = _dot(sh, mmean) + _dot(sl_, mmean)
    vn = (xc * lax.rsqrt(var + EPS) * lng_ref[...] + lnb_ref[...]).astype(BF16)
    tri = (lax.broadcasted_iota(jnp.int32, (SG_CHUNK, SG_CHUNK), 1)
           <= lax.broadcasted_iota(jnp.int32, (SG_CHUNK, SG_CHUNK), 0))
    lane_grp = lax.broadcasted_iota(jnp.int32, (SG_CHUNK, SG_WIDTH), 1) // SG_CH
    ws = [jnp.where(tri, sgw_ref[g], 0.0).astype(BF16) for g in range(SG_GROUPS)]
    for c in range(tm // SG_CHUNK):
        rows = slice(c * SG_CHUNK, (c + 1) * SG_CHUNK)
        vc = vn[rows, :]
        mixed = sgb_ref[...]
        for g in range(SG_GROUPS):
            mixed = mixed + jnp.where(lane_grp == g, _dot(ws[g], vc), 0.0)
        sg_out[rows, :] = u[rows, :] * mixed

    ssm_out[...] = z[:, C_SSM:C_END].astype(BF16)


def _mixer_in(h, lp, cos128, sin128, tm, n_pos_tiles):
    t = h.shape[0]
    const = lambda *shape: pl.BlockSpec(shape, lambda i: (0,) * len(shape))
    rows = lambda w: pl.BlockSpec((tm, w), lambda i: (i, 0))
    pos = pl.BlockSpec((tm, LANES), lambda i: (i % n_pos_tiles, 0))
    return pl.pallas_call(
        functools.partial(_mixer_in_kernel, tm=tm),
        grid=(t // tm,),
        in_specs=[rows(D_MODEL), const(1, D_MODEL), const(D_MODEL, C_END), const(1, Q_RANK),
                  const(1, KV_RANK), const(Q_RANK, QKV_PAD), const(Q_RANK, QKV_PAD),
                  const(KV_RANK, QKV_PAD), const(KV_RANK, QKV_PAD), pos, pos, const(8, LANES),
                  const(1, QKV_PAD), const(1, SG_WIDTH), const(1, SG_WIDTH),
                  const(SG_GROUPS, SG_CHUNK, SG_CHUNK), const(SG_CHUNK, SG_WIDTH),
                  const(SG_WIDTH, SG_WIDTH)],
        out_specs=[rows(QKV_PAD), rows(QKV_PAD), rows(QKV_PAD), rows(SG_WIDTH), rows(SSM_WIDTH)],
        out_shape=[jax.ShapeDtypeStruct((t, QKV_PAD), BF16), jax.ShapeDtypeStruct((t, QKV_PAD), BF16),
                   jax.ShapeDtypeStruct((t, QKV_PAD), BF16), jax.ShapeDtypeStruct((t, SG_WIDTH), F32),
                   jax.ShapeDtypeStruct((t, SSM_WIDTH), BF16)],
        compiler_params=pltpu.CompilerParams(dimension_semantics=("parallel",),
                                             vmem_limit_bytes=VMEM_LIMIT),
        name="mixer_in",
    )(h, lp["ln1"], lp["win"], lp["gcq"], lp["gckv"], lp["wqa"], lp["wqb"], lp["wk"], lp["wv"],
      cos128, sin128, lp["gqk"], lp["vones"], lp["sg_lng"], lp["sg_lnb"], lp["sg_w"], lp["sg_bias"],
      lp["mmean"])


def _attn_kernel(q_ref, k_ref, v_ref, o_ref, s_even, s_odd, m_ref, acc_ref, *, tq):
    i = pl.program_id(2)
    heads = [slice(hh * HEAD_PAD, (hh + 1) * HEAD_PAD) for hh in range(ATTN_HEADS_PER_STEP)]

    def block_rows(j):
        return pl.ds(pl.multiple_of(j * tq, tq), tq)

    def scores(j, s_ref):
        for hh, sl in enumerate(heads):
            s_ref[hh] = lax.dot_general(q_ref[:, sl], k_ref[block_rows(j), sl],
                                        (((1,), (1,)), ((), ())), preferred_element_type=F32)

    def accumulate(j, s_ref, masked):
        for hh, sl in enumerate(heads):
            s = s_ref[hh]
            if masked:
                keep = (lax.broadcasted_iota(jnp.int32, (tq, tq), 1)
                        <= lax.broadcasted_iota(jnp.int32, (tq, tq), 0))
                s = jnp.where(keep, s, -jnp.inf)
            m = m_ref[hh]
            m_new = jnp.maximum(m, jnp.max(s, axis=-1, keepdims=True))
            alpha = jnp.exp2(m - m_new)
            p = jnp.exp2(s - m_new).astype(BF16)
            m_ref[hh] = m_new
            acc_ref[hh] = alpha * acc_ref[hh] + _dot(p, v_ref[block_rows(j), sl])

    m_ref[...] = jnp.full(m_ref.shape, -jnp.inf, F32)
    acc_ref[...] = jnp.zeros(acc_ref.shape, F32)
    scores(0, s_even)

    def pipelined(t, _):
        j = ATTN_BLOCKS_PER_ITER * t
        for u in range(0, ATTN_BLOCKS_PER_ITER, 2):
            scores(j + u + 1, s_odd)
            accumulate(j + u, s_even, False)
            scores(j + u + 2, s_even)
            accumulate(j + u + 1, s_odd, False)
        return 0

    n_pipelined = i // ATTN_BLOCKS_PER_ITER
    lax.fori_loop(0, n_pipelined, pipelined, 0)

    def leftover(j, _):
        accumulate(j, s_even, False)
        scores(j + 1, s_even)
        return 0

    lax.fori_loop(n_pipelined * ATTN_BLOCKS_PER_ITER, i, leftover, 0)
    accumulate(i, s_even, True)

    lane = lax.broadcasted_iota(jnp.int32, (tq, HEAD_PAD), 1)
    for hh, sl in enumerate(heads):
        acc = acc_ref[hh]
        out = acc / acc[:, V_DIM:V_DIM + 1]
        o_ref[:, sl] = jnp.where(lane < V_DIM, out, 0.0).astype(BF16)


def _attention(q, k, v, batch, seq, tq):
    t = q.shape[0]
    nq = seq // tq
    width = ATTN_HEADS_PER_STEP * HEAD_PAD
    return pl.pallas_call(
        functools.partial(_attn_kernel, tq=tq),
        grid=(batch, ATTN_HEADS // ATTN_HEADS_PER_STEP, nq),
        in_specs=[pl.BlockSpec((tq, width), lambda b, h, i: (b * nq + i, h)),
                  pl.BlockSpec((seq, width), lambda b, h, i: (b, h)),
                  pl.BlockSpec((seq, width), lambda b, h, i: (b, h))],
        out_specs=pl.BlockSpec((tq, width), lambda b, h, i: (b * nq + i, h)),
        out_shape=jax.ShapeDtypeStruct((t, QKV_PAD), BF16),
        scratch_shapes=[pltpu.VMEM((ATTN_HEADS_PER_STEP, tq, tq), F32),
                        pltpu.VMEM((ATTN_HEADS_PER_STEP, tq, tq), F32),
                        pltpu.VMEM((ATTN_HEADS_PER_STEP, tq, 1), F32),
                        pltpu.VMEM((ATTN_HEADS_PER_STEP, tq, HEAD_PAD), F32)],
        compiler_params=pltpu.CompilerParams(
            dimension_semantics=("parallel", "parallel", "arbitrary"), vmem_limit_bytes=VMEM_LIMIT),
        name="mla_attention",
    )(q, k, v)


def _ssm_kernel(x_ref, m_ref, pre_ref, pim_ref, qre_ref, qim_ref, are_ref, aim_ref, y_ref, *,
                rows_per_batch, n_steps):
    x = x_ref[0]
    cr = x.shape[0]
    y_local = _dot(x, m_ref[0])
    sre = _dot(x, pre_ref[0])
    sim = _dot(x, pim_ref[0])
    row = lax.broadcasted_iota(jnp.int32, (cr, SSM_SLANES), 0) % rows_per_batch
    are = are_ref[0]
    aim = aim_ref[0]
    for k in range(n_steps):
        sh = 1 << k
        ok = row >= sh
        pr = jnp.where(ok, pltpu.roll(sre, sh, 0), 0.0)
        pi = jnp.where(ok, pltpu.roll(sim, sh, 0), 0.0)
        ar = are[k:k + 1, :]
        ai = aim[k:k + 1, :]
        sre, sim = sre + ar * pr - ai * pi, sim + ar * pi + ai * pr
    ok = row >= 1
    hre = jnp.where(ok, pltpu.roll(sre, 1, 0), 0.0).astype(BF16)
    him = jnp.where(ok, pltpu.roll(sim, 1, 0), 0.0).astype(BF16)
    y_ref[0] = y_local + _dot(hre, qre_ref[0]) + _dot(him, qim_ref[0])


def _ssm(xp, lp, rows_per_batch):
    n_bundles, cr, _ = xp.shape
    n_steps = lp["ssm_are"].shape[1]
    blk = lambda *shape: pl.BlockSpec((1,) + shape, lambda p: (p,) + (0,) * len(shape))
    return pl.pallas_call(
        functools.partial(_ssm_kernel, rows_per_batch=rows_per_batch, n_steps=n_steps),
        grid=(n_bundles,),
        in_specs=[blk(cr, SSM_COLS), blk(SSM_COLS, SSM_COLS), blk(SSM_COLS, SSM_SLANES),
                  blk(SSM_COLS, SSM_SLANES), blk(SSM_SLANES, SSM_COLS), blk(SSM_SLANES, SSM_COLS),
                  blk(n_steps, SSM_SLANES), blk(n_steps, SSM_SLANES)],
        out_specs=blk(cr, SSM_COLS),
        out_shape=jax.ShapeDtypeStruct((n_bundles, cr, SSM_COLS), F32),
        compiler_params=pltpu.CompilerParams(dimension_semantics=("parallel",),
                                             vmem_limit_bytes=VMEM_LIMIT),
        name="s5_mixer",
    )(xp, lp["ssm_m"], lp["ssm_pre"], lp["ssm_pim"], lp["ssm_qre"], lp["ssm_qim"], lp["ssm_are"],
      lp["ssm_aim"])


def _mixer_out_kernel(a_ref, sg_ref, y_ref, h_ref, gma_ref, gmg_ref, gms_ref, woa_ref, wog_ref,
                      wos_ref, wglu_ref, bglu_ref, ln2_ref, wrh_ref, wrl_ref, br_ref, ltri_ref,
                      h1_out, xn_out, ri_out, rw_out, cnt_out, run_ref):
    i = pl.program_id(0)

    @pl.when(i == 0)
    def _():
        run_ref[...] = jnp.zeros_like(run_ref)

    g = jax.nn.gelu(y_ref[...])
    glu = g * jax.nn.sigmoid(_dot(g.astype(BF16), wglu_ref[...]) + bglu_ref[...])
    a_n = _rms(a_ref[...].astype(F32), gma_ref[...], ATTN_HEADS * V_DIM)
    g_n = _rms(sg_ref[...], gmg_ref[...])
    s_n = _rms(glu, gms_ref[...])
    h1 = (h_ref[...] + _dot(a_n.astype(BF16), woa_ref[...]) + _dot(g_n.astype(BF16), wog_ref[...])
          + _dot(s_n.astype(BF16), wos_ref[...]))
    h1_out[...] = h1
    xn = _rms(h1, ln2_ref[...])
    _store_token_rows(xn_out, _pack_bf16_pairs(xn))

    xh, xl = _split_bf16(xn)
    logit = _dot(xh, wrh_ref[...]) + _dot(xl, wrh_ref[...]) + _dot(xh, wrl_ref[...]) + br_ref[...]
    shape = logit.shape
    lane = lax.broadcasted_iota(jnp.int32, shape, 1)
    neg = -jnp.inf
    is_grp = lane < MOE_GROUPS
    gl = jnp.where(is_grp, logit, neg)
    gmax = jnp.max(gl, axis=-1, keepdims=True)
    gsel = jnp.min(jnp.where(gl == gmax, lane, LANES), axis=-1, keepdims=True)
    gsum = jnp.sum(jnp.where(is_grp, jnp.exp(logit - gmax), 0.0), axis=-1, keepdims=True)
    g_w = 1.0 / gsum
    lo = MOE_GROUPS + gsel * EXPERTS_PER_GROUP
    in_grp = (lane >= lo) & (lane < lo + EXPERTS_PER_GROUP)
    el = jnp.where(in_grp, logit, neg)
    m1 = jnp.max(el, axis=-1, keepdims=True)
    i1 = jnp.min(jnp.where(el == m1, lane, LANES), axis=-1, keepdims=True)
    el2 = jnp.where(lane == i1, neg, el)
    m2 = jnp.max(el2, axis=-1, keepdims=True)
    i2 = jnp.min(jnp.where(el2 == m2, lane, LANES), axis=-1, keepdims=True)
    e2 = jnp.exp(m2 - m1)
    w1 = g_w / (1.0 + e2)
    w2 = g_w * e2 / (1.0 + e2)
    e_a = i1 - MOE_GROUPS
    e_b = i2 - MOE_GROUPS

    hit_a = lane == e_a
    hit_b = lane == e_b
    onehot = jnp.where(hit_a | hit_b, 1.0, 0.0).astype(BF16)
    before = _dot(ltri_ref[...], onehot) + run_ref[...]
    c_a = jnp.sum(jnp.where(hit_a, before, 0.0), axis=-1, keepdims=True).astype(jnp.int32)
    c_b = jnp.sum(jnp.where(hit_b, before, 0.0), axis=-1, keepdims=True).astype(jnp.int32)
    run_ref[...] = run_ref[...] + jnp.sum(onehot.astype(F32), axis=0, keepdims=True)
    cnt_out[...] = jnp.broadcast_to(run_ref[...], cnt_out.shape)

    ri_out[...] = jnp.where(lane == 0, e_a, jnp.where(lane == 1, e_b, jnp.where(
        lane == 2, c_a, jnp.where(lane == 3, c_b, 0))))[:, :ROUTE_COLS]
    rw_out[...] = jnp.where(lane == 0, w1, jnp.where(lane == 1, w2, 0.0))[:, :ROUTE_COLS]


def _mixer_out(a, sg, y, h, lp, tm):
    t = h.shape[0]
    const = lambda *shape: pl.BlockSpec(shape, lambda i: (0,) * len(shape))
    rows = lambda w: pl.BlockSpec((tm, w), lambda i: (i, 0))
    return pl.pallas_call(
        _mixer_out_kernel,
        grid=(t // tm,),
        in_specs=[rows(QKV_PAD), rows(SG_WIDTH), rows(SSM_WIDTH), rows(D_MODEL), const(1, QKV_PAD),
                  const(1, SG_WIDTH), const(1, SSM_WIDTH), const(QKV_PAD, D_MODEL),
                  const(SG_WIDTH, D_MODEL), const(SSM_WIDTH, D_MODEL), const(SSM_WIDTH, SSM_WIDTH),
                  const(1, SSM_WIDTH), const(1, D_MODEL), const(D_MODEL, LANES), const(D_MODEL, LANES),
                  const(1, LANES), const(tm, tm)],
        out_specs=[rows(D_MODEL), pl.BlockSpec((tm * PACKED_ROWS, LANES), lambda i: (i, 0)),
                   rows(ROUTE_COLS), rows(ROUTE_COLS), const(8, LANES)],
        out_shape=[jax.ShapeDtypeStruct((t, D_MODEL), F32),
                   jax.ShapeDtypeStruct((t * PACKED_ROWS, LANES), jnp.uint32),
                   jax.ShapeDtypeStruct((t, ROUTE_COLS), jnp.int32),
                   jax.ShapeDtypeStruct((t, ROUTE_COLS), F32),
                   jax.ShapeDtypeStruct((8, LANES), F32)],
        scratch_shapes=[pltpu.VMEM((1, LANES), F32)],
        compiler_params=pltpu.CompilerParams(dimension_semantics=("arbitrary",),
                                             vmem_limit_bytes=VMEM_LIMIT),
        name="mixer_out_router",
    )(a, sg, y, h, lp["gm_a"], lp["gm_g"], lp["gm_s"], lp["wo_a"], lp["wo_g"], lp["wo_s"], lp["wglu"],
      lp["bglu"], lp["ln2"], lp["wr_hi"], lp["wr_lo"], lp["br"], lp["ltri"])


def _row_copy(src_ref, src_row, dst_ref, dst_row, sem):
    src = src_ref.at[pl.ds(pl.multiple_of(src_row * PACKED_ROWS, PACKED_ROWS), PACKED_ROWS)]
    return pltpu.make_async_copy(src, dst_ref.at[pl.ds(dst_row * PACKED_ROWS, PACKED_ROWS)], sem)


def _rows_wait(src_ref, dst_ref, n_tokens, sem):
    rows = pl.ds(0, n_tokens * PACKED_ROWS)
    pltpu.make_async_copy(src_ref.at[rows], dst_ref.at[rows], sem).wait()


def _store_token_rows(ref, words):
    n = words.shape[0]
    for c in range(PACKED_ROWS):
        ref[pl.ds(c, n, stride=PACKED_ROWS), :] = words[:, c * LANES:(c + 1) * LANES]


def _load_token_rows(ref):
    n = ref.shape[0] // PACKED_ROWS
    return jnp.concatenate([ref[pl.ds(c, n, stride=PACKED_ROWS), :] for c in range(PACKED_ROWS)], axis=1)


def _slot_map_kernel(dest_ref, zeros_ref, row_tok_ref, sem, *, td):
    i = pl.program_id(0)

    @pl.when(i == 0)
    def _():
        clear = pltpu.make_async_copy(zeros_ref, row_tok_ref, sem)
        clear.start()
        clear.wait()

    def place(r, _):
        row_tok_ref[dest_ref[0, 0, 2 * r]] = i * td + r
        row_tok_ref[dest_ref[0, 0, 2 * r + 1]] = i * td + r
        return 0

    lax.fori_loop(0, td, place, 0, unroll=SCALAR_LOOP_UNROLL)


def _slot_map(dest, n_rows, td):
    n_tiles = dest.shape[0]
    return pl.pallas_call(
        functools.partial(_slot_map_kernel, td=td),
        grid=(n_tiles,),
        in_specs=[pl.BlockSpec((1, 1, 2 * td), lambda i: (i, 0, 0), memory_space=pltpu.SMEM),
                  pl.BlockSpec(memory_space=pl.ANY)],
        out_specs=pl.BlockSpec(memory_space=pltpu.SMEM),
        out_shape=jax.ShapeDtypeStruct((n_rows,), jnp.int32),
        scratch_shapes=[pltpu.SemaphoreType.DMA(())],
        compiler_params=pltpu.CompilerParams(dimension_semantics=("arbitrary",)),
        name="moe_slot_map",
    )(dest, jnp.zeros((n_rows,), jnp.int32))


def _unpack_bf16_pairs(u, dtype):
    lo = lax.bitcast_convert_type(u << 16, F32)
    hi = lax.bitcast_convert_type(u & jnp.uint32(0xFFFF0000), F32)
    return jnp.concatenate([lo, hi], axis=1).astype(dtype)


def _pack_bf16_pairs(x):
    n = x.shape[1] // 2
    bits = lax.bitcast_convert_type(x.astype(BF16).astype(F32), jnp.uint32)
    return (bits[:, :n] >> 16) | (bits[:, n:] & jnp.uint32(0xFFFF0000))


def _expert_kernel(row_tok_ref, blk_e_ref, x_ref, w13_ref, w2_ref, yb_ref, xbuf, sem):
    del blk_e_ref
    b = pl.program_id(0)
    last = pl.num_programs(0) - 1
    n_buf = EXPERT_GATHER_AHEAD + 1

    def gather(block, to_slot):
        base = block * EXPERT_BLOCK
        for r in range(EXPERT_BLOCK):
            _row_copy(x_ref, row_tok_ref[base + r], xbuf.at[to_slot], r,
                      sem.at[to_slot]).start(priority=r % 2)

    def wait(of_slot):
        _rows_wait(x_ref, xbuf.at[of_slot], EXPERT_BLOCK, sem.at[of_slot])

    @pl.when(b == 0)
    def _():
        for ahead in range(EXPERT_GATHER_AHEAD):
            gather(ahead, ahead)

    gather(jnp.minimum(b + EXPERT_GATHER_AHEAD, last), (b + EXPERT_GATHER_AHEAD) % n_buf)
    slot = b % n_buf
    wait(slot)
    hh = _dot(_unpack_bf16_pairs(_load_token_rows(xbuf.at[slot]), BF16), w13_ref[...])
    hb = jax.nn.silu(hh[:, :D_EXPERT]) * hh[:, D_EXPERT:]
    _store_token_rows(yb_ref, _pack_bf16_pairs(_dot(hb.astype(BF16), w2_ref[...])))

    @pl.when(b == last)
    def _():
        for ahead in range(1, n_buf):
            wait((b + ahead) % n_buf)


def _experts(row_tok, blk_e, xn_packed, w13, w2):
    n_rows = row_tok.shape[0]
    grid_spec = pltpu.PrefetchScalarGridSpec(
        num_scalar_prefetch=2,
        grid=(n_rows // EXPERT_BLOCK,),
        in_specs=[pl.BlockSpec(memory_space=pl.ANY),
                  pl.BlockSpec((None, D_MODEL, 2 * D_EXPERT), lambda i, rt, be: (be[i], 0, 0)),
                  pl.BlockSpec((None, D_EXPERT, D_MODEL), lambda i, rt, be: (be[i], 0, 0))],
        out_specs=pl.BlockSpec((EXPERT_BLOCK * PACKED_ROWS, LANES), lambda i, rt, be: (i, 0)),
        scratch_shapes=[pltpu.VMEM((EXPERT_GATHER_AHEAD + 1, EXPERT_BLOCK * PACKED_ROWS, LANES), jnp.uint32),
                        pltpu.SemaphoreType.DMA((EXPERT_GATHER_AHEAD + 1,))],
    )
    return pl.pallas_call(
        _expert_kernel,
        grid_spec=grid_spec,
        out_shape=jax.ShapeDtypeStruct((n_rows * PACKED_ROWS, LANES), jnp.uint32),
        compiler_params=pltpu.CompilerParams(dimension_semantics=("arbitrary",),
                                             vmem_limit_bytes=VMEM_LIMIT),
        name="moe_experts",
    )(row_tok, blk_e, xn_packed, w13, w2)


def _combine_kernel(dest_ref, dest_next_ref, h_ref, rw_ref, p_ref, wpg_ref, bpg_ref,
                    wple_ref, yb_ref, out_ref, y0_ref, y1_ref, sem, *, tc):
    i = pl.program_id(0)
    last = pl.num_programs(0) - 1
    slot = i % 2

    def gather(tile_dest_ref, to_slot):
        for r in range(tc):
            _row_copy(yb_ref, tile_dest_ref[0, 0, 2 * r], y0_ref.at[to_slot], r, sem.at[to_slot]).start()
            _row_copy(yb_ref, tile_dest_ref[0, 0, 2 * r + 1], y1_ref.at[to_slot], r,
                      sem.at[to_slot]).start()

    def wait(of_slot):
        _rows_wait(yb_ref, y0_ref.at[of_slot], tc, sem.at[of_slot])
        _rows_wait(yb_ref, y1_ref.at[of_slot], tc, sem.at[of_slot])

    @pl.when(i == 0)
    def _():
        gather(dest_ref, 0)

    gather(dest_next_ref, 1 - slot)
    wait(slot)

    rw = rw_ref[...]
    y0 = _unpack_bf16_pairs(_load_token_rows(y0_ref.at[slot]), F32)
    y1 = _unpack_bf16_pairs(_load_token_rows(y1_ref.at[slot]), F32)
    h2 = h_ref[...] + rw[:, 0:1] * y0 + rw[:, 1:2] * y1
    gate = jax.nn.sigmoid(_dot(h2.astype(BF16), wpg_ref[...]) + bpg_ref[...])
    out_ref[...] = h2 + gate * _dot(p_ref[...].astype(BF16), wple_ref[...])

    @pl.when(i == last)
    def _():
        wait(1 - slot)


def _combine(dest, h1, rw, p_all, layer, lp, yb, tc):
    t = h1.shape[0]
    n_tiles = t // tc
    const = lambda *shape: pl.BlockSpec(shape, lambda i: (0,) * len(shape))
    rows = lambda w: pl.BlockSpec((tc, w), lambda i: (i, 0))
    p_rows = pl.BlockSpec((tc, PLE_DIM), lambda i: (layer * n_tiles + i, 0))
    grid_spec = pl.GridSpec(
        grid=(n_tiles,),
        in_specs=[pl.BlockSpec((1, 1, 2 * tc), lambda i: (i, 0, 0), memory_space=pltpu.SMEM),
                  pl.BlockSpec((1, 1, 2 * tc), lambda i: (jnp.minimum(i + 1, n_tiles - 1), 0, 0),
                               memory_space=pltpu.SMEM),
                  rows(D_MODEL), rows(ROUTE_COLS), p_rows, const(D_MODEL, D_MODEL), const(1, D_MODEL),
                  const(PLE_DIM, D_MODEL), pl.BlockSpec(memory_space=pl.ANY)],
        out_specs=rows(D_MODEL),
        scratch_shapes=[pltpu.VMEM((2, tc * PACKED_ROWS, LANES), jnp.uint32),
                        pltpu.VMEM((2, tc * PACKED_ROWS, LANES), jnp.uint32),
                        pltpu.SemaphoreType.DMA((2,))],
    )
    return pl.pallas_call(
        functools.partial(_combine_kernel, tc=tc),
        grid_spec=grid_spec,
        out_shape=jax.ShapeDtypeStruct((t, D_MODEL), F32),
        compiler_params=pltpu.CompilerParams(dimension_semantics=("arbitrary",),
                                             vmem_limit_bytes=VMEM_LIMIT),
        name="moe_combine_ple",
    )(dest, dest, h1, rw, p_all, lp["wpg"], lp["bpg"], lp["wple"], yb)


def _rope_lane_tables(seq):
    inv = 1.0 / (ROPE_THETA ** (jnp.arange(0, ROPE_DIM, 2, dtype=F32) / ROPE_DIM))
    ang = jnp.arange(seq, dtype=F32)[:, None] * inv[None, :]
    cos, sin = jnp.cos(ang), jnp.sin(ang)
    half = ROPE_DIM // 2
    cos128 = jnp.concatenate([jnp.ones((seq, NOPE_DIM), F32), cos, cos,
                              jnp.zeros((seq, LANES - QK_DIM), F32)], axis=1)
    sin128 = jnp.concatenate([jnp.zeros((seq, NOPE_DIM), F32), -sin, sin,
                              jnp.zeros((seq, LANES - QK_DIM), F32)], axis=1)
    del half
    return cos128, sin128


def _swap_rope_halves(w):
    half = ROPE_DIM // 2
    return jnp.concatenate([w[..., half:], w[..., :half]], axis=-1)


def _pad_heads(w, width):
    return jnp.pad(w, ((0, 0), (0, 0), (0, HEAD_PAD - width))).reshape(w.shape[0], QKV_PAD)


def _rope_gain_rows(g):
    pad = jnp.zeros((LANES - QK_DIM,), F32)
    ga = jnp.concatenate([g, pad])
    gb = jnp.concatenate([jnp.zeros((NOPE_DIM,), F32), _swap_rope_halves(g[NOPE_DIM:]), pad])
    return ga, gb


def _ssm_tables(lam_re, lam_im, log_dt, b_re, b_im, c_re, c_im, d_skip, n_steps):
    hp = lax.Precision.HIGHEST
    lr = jnp.minimum(lam_re, -1e-4)
    li = lam_im
    dt = jnp.exp(log_dt)
    mag = jnp.exp(lr * dt)
    ab_re, ab_im = mag * jnp.cos(li * dt), mag * jnp.sin(li * dt)
    den = lr * lr + li * li
    nr, ni = ab_re - 1.0, ab_im
    fr = (nr * lr + ni * li) / den
    fi = (ni * lr - nr * li) / den
    bb_re = fr[..., None] * b_re - fi[..., None] * b_im
    bb_im = fr[..., None] * b_im + fi[..., None] * b_re

    steps = jnp.arange(SSM_CHUNK + 1, dtype=F32)[:, None, None]
    pmag = jnp.exp(lr * dt * steps)
    pw_re, pw_im = pmag * jnp.cos(li * dt * steps), pmag * jnp.sin(li * dt * steps)

    w_re = pw_re[:SSM_CHUNK, :, :, None] * bb_re - pw_im[:SSM_CHUNK, :, :, None] * bb_im
    w_im = pw_re[:SSM_CHUNK, :, :, None] * bb_im + pw_im[:SSM_CHUNK, :, :, None] * bb_re
    taps = (jnp.einsum("gcn,tgnd->tgcd", c_re, w_re, precision=hp)
            - jnp.einsum("gcn,tgnd->tgcd", c_im, w_im, precision=hp))
    taps = taps.at[0].add(d_skip[:, :, None] * jnp.eye(SSM_CH, dtype=F32)[None])
    t_in = jnp.arange(SSM_CHUNK)[:, None]
    t_out = jnp.arange(SSM_CHUNK)[None, :]
    lag = t_out - t_in
    m = jnp.where((lag >= 0)[:, :, None, None, None], taps[jnp.clip(lag, 0)], 0.0)
    m = m.transpose(2, 0, 4, 1, 3)

    p_re = w_re[::-1].transpose(1, 0, 3, 2)
    p_im = w_im[::-1].transpose(1, 0, 3, 2)

    up_re, up_im = pw_re[1:], pw_im[1:]
    q_re = (c_re[None] * up_re[:, :, None, :] - c_im[None] * up_im[:, :, None, :])
    q_im = -(c_re[None] * up_im[:, :, None, :] + c_im[None] * up_re[:, :, None, :])
    q_re = q_re.transpose(1, 3, 0, 2)
    q_im = q_im.transpose(1, 3, 0, 2)

    are, aim = [pw_re[SSM_CHUNK]], [pw_im[SSM_CHUNK]]
    for _ in range(n_steps - 1):
        r, im = are[-1], aim[-1]
        are.append(r * r - im * im)
        aim.append(2.0 * r * im)
    are, aim = jnp.stack(are, axis=1), jnp.stack(aim, axis=1)

    nbu = SSM_GROUPS // SSM_BUNDLE
    per_group = SSM_CHUNK * SSM_CH
    bundle = lambda x: x.reshape((nbu, SSM_BUNDLE) + x.shape[1:])
    col = jnp.arange(SSM_COLS)
    col_src = (col // (SSM_BUNDLE * SSM_CH)) * SSM_CH + col % SSM_CH
    col_grp = (col // SSM_CH) % SSM_BUNDLE
    sel_io = ((col_grp[None, None, :] == jnp.arange(SSM_BUNDLE)[:, None, None])
              & (col_src[None, None, :] == jnp.arange(per_group)[None, :, None])).astype(F32)
    lane = jnp.arange(SSM_SLANES)
    sel_st = ((lane[None, None, :] // SSM_STATE == jnp.arange(SSM_BUNDLE)[:, None, None])
              & (lane[None, None, :] % SSM_STATE == jnp.arange(SSM_STATE)[None, :, None])).astype(F32)

    def expand(x, rows_sel, cols_sel):
        return jnp.einsum("gaR,qgab,gbS->qRS", rows_sel, bundle(x), cols_sel).astype(BF16)

    def lanes(x):
        return bundle(x).transpose(0, 2, 1, 3).reshape(nbu, x.shape[1], SSM_SLANES)

    flat_io = lambda x: x.reshape(SSM_GROUPS, per_group, -1)
    return dict(ssm_m=expand(m.reshape(SSM_GROUPS, per_group, per_group), sel_io, sel_io),
                ssm_pre=expand(flat_io(p_re), sel_io, sel_st), ssm_pim=expand(flat_io(p_im), sel_io, sel_st),
                ssm_qre=expand(q_re.reshape(SSM_GROUPS, SSM_STATE, per_group), sel_st, sel_io),
                ssm_qim=expand(q_im.reshape(SSM_GROUPS, SSM_STATE, per_group), sel_st, sel_io),
                ssm_are=lanes(are), ssm_aim=lanes(aim))


def _layer_params(i, tm, n_scan_steps, ln1, w_in, g_cq, g_ckv, w_uq, w_ukv, g_qh, g_kh, sg_ln_g,
                  sg_ln_b, sg_w, sg_b, lam_re, lam_im, log_dt, ssm_b_re, ssm_b_im, ssm_c_re, ssm_c_im,
                  ssm_d, w_glu, b_glu, g_mix, w_out, ln2, w_rg, b_rg, w_rexp, b_rexp, w1, w3, w2, w_pg,
                  b_pg, w_ple):
    lp = {}
    wi = w_in[i]
    off_kr = Q_RANK + KV_RANK
    off_sg = off_kr + ROPE_DIM
    kr = wi[:, off_kr:off_sg]
    zpad = lambda n: jnp.zeros((D_MODEL, n), F32)
    kra = jnp.concatenate([zpad(NOPE_DIM), kr, zpad(LANES - QK_DIM)], axis=1)
    krb = jnp.concatenate([zpad(NOPE_DIM), _swap_rope_halves(kr), zpad(LANES - QK_DIM)], axis=1)
    lp["win"] = jnp.concatenate([wi[:, :off_kr], kra, krb, wi[:, off_sg:]], axis=1).astype(BF16)
    lp["ln1"] = ln1[i][None]
    lp["gcq"] = g_cq[i][None]
    lp["gckv"] = g_ckv[i][None]

    wq = w_uq[i].reshape(Q_RANK, ATTN_HEADS, QK_DIM)
    lp["wqa"] = _pad_heads(wq, QK_DIM).astype(BF16)
    wq_sw = jnp.concatenate([jnp.zeros((Q_RANK, ATTN_HEADS, NOPE_DIM), F32),
                             _swap_rope_halves(wq[..., NOPE_DIM:])], axis=-1)
    lp["wqb"] = _pad_heads(wq_sw, QK_DIM).astype(BF16)
    wkv = w_ukv[i].reshape(KV_RANK, ATTN_HEADS, NOPE_DIM + V_DIM)
    lp["wk"] = _pad_heads(wkv[..., :NOPE_DIM], NOPE_DIM).astype(BF16)
    lp["wv"] = _pad_heads(wkv[..., NOPE_DIM:], V_DIM).astype(BF16)
    gq_a, gq_b = _rope_gain_rows(g_qh[i])
    gk_a, gk_b = _rope_gain_rows(g_kh[i])
    lp["gqk"] = jnp.concatenate([jnp.stack([gq_a, gq_b, gk_a, gk_b]), jnp.zeros((4, LANES), F32)])
    lp["vones"] = jnp.tile((jnp.arange(HEAD_PAD) == V_DIM).astype(F32), ATTN_HEADS)[None]

    lp["sg_lng"] = sg_ln_g[i][None]
    lp["sg_lnb"] = sg_ln_b[i][None]
    lp["sg_w"] = sg_w[i]
    lp["sg_bias"] = jnp.repeat(sg_b[i].T, SG_CH, axis=1)
    grp = jnp.arange(SG_WIDTH) // SG_CH
    lp["mmean"] = ((grp[:, None] == grp[None, :]).astype(F32) / SG_CH).astype(BF16)

    lp.update(_ssm_tables(lam_re[i], lam_im[i], log_dt[i], ssm_b_re[i], ssm_b_im[i], ssm_c_re[i],
                          ssm_c_im[i], ssm_d[i], n_scan_steps))

    gm = g_mix[i]
    aw = ATTN_HEADS * V_DIM
    lp["gm_a"] = _pad_heads(gm[:aw].reshape(1, ATTN_HEADS, V_DIM), V_DIM)
    lp["gm_g"] = gm[aw:aw + SG_WIDTH][None]
    lp["gm_s"] = gm[aw + SG_WIDTH:][None]
    wo = w_out[i]
    woa = wo[:aw].reshape(ATTN_HEADS, V_DIM, D_MODEL)
    lp["wo_a"] = jnp.pad(woa, ((0, 0), (0, HEAD_PAD - V_DIM), (0, 0))).reshape(QKV_PAD, D_MODEL).astype(BF16)
    lp["wo_g"] = wo[aw:aw + SG_WIDTH].astype(BF16)
    lp["wo_s"] = wo[aw + SG_WIDTH:].astype(BF16)
    lp["wglu"] = w_glu[i].astype(BF16)
    lp["bglu"] = b_glu[i][None]
    lp["ln2"] = ln2[i][None]
    wr = jnp.concatenate([w_rg[i], w_rexp[i], jnp.zeros((D_MODEL, LANES - MOE_GROUPS - N_EXPERTS), F32)],
                         axis=1)
    lp["wr_hi"], lp["wr_lo"] = _split_bf16(wr)
    lp["br"] = jnp.concatenate([b_rg[i], b_rexp[i], jnp.zeros((LANES - MOE_GROUPS - N_EXPERTS,), F32)])[None]
    lp["ltri"] = (jnp.arange(tm)[None, :] < jnp.arange(tm)[:, None]).astype(BF16)

    lp["w13"] = jnp.concatenate([w1[i], w3[i]], axis=-1).astype(BF16)
    lp["w2"] = w2[i].astype(BF16)
    lp["wpg"] = w_pg[i].astype(BF16)
    lp["bpg"] = b_pg[i][None]
    lp["wple"] = w_ple[i].astype(BF16)
    return lp


def kernel(x, p, ln1, w_in, g_cq, g_ckv, w_uq, w_ukv, g_qh, g_kh, sg_ln_g, sg_ln_b, sg_w, sg_b, lam_re, lam_im, log_dt, ssm_b_re, ssm_b_im, ssm_c_re, ssm_c_im, ssm_d, w_glu, b_glu, g_mix, w_out, ln2, w_rg, b_rg, w_rexp, b_rexp, w1, w3, w2, w_pg, b_pg, w_ple):
    batch, seq, _ = x.shape
    depth = p.shape[0]
    t = batch * seq
    tm = min(512, seq)
    tq = min(512, seq)
    td = min(1024, seq)
    tc = min(256, seq)
    assert seq % tm == 0 and seq % SSM_CHUNK == 0 and tm % SG_CHUNK == 0
    rows_per_batch = seq // SSM_CHUNK
    n_scan_steps = max(1, (rows_per_batch - 1).bit_length())
    cr = t // SSM_CHUNK
    n_bundles = SSM_GROUPS // SSM_BUNDLE
    bundle_w = SSM_BUNDLE * SSM_CH
    tk2 = 2 * t
    n_blocks = -(-(tk2 + N_EXPERTS * (EXPERT_BLOCK - 1)) // EXPERT_BLOCK)
    n_rows = n_blocks * EXPERT_BLOCK

    cos128, sin128 = _rope_lane_tables(seq)
    h = x.reshape(t, D_MODEL)
    p_all = p.reshape(depth * t, PLE_DIM)
    weights = (ln1, w_in, g_cq, g_ckv, w_uq, w_ukv, g_qh, g_kh, sg_ln_g, sg_ln_b, sg_w, sg_b, lam_re,
               lam_im, log_dt, ssm_b_re, ssm_b_im, ssm_c_re, ssm_c_im, ssm_d, w_glu, b_glu, g_mix, w_out,
               ln2, w_rg, b_rg, w_rexp, b_rexp, w1, w3, w2, w_pg, b_pg, w_ple)
    for i in range(depth):
        lp = _layer_params(i, tm, n_scan_steps, *weights)
        q, k, v, sg, xs_ssm = _mixer_in(h, lp, cos128, sin128, tm, seq // tm)
        a = _attention(q, k, v, batch, seq, tq)

        xp = xs_ssm.reshape(cr, SSM_CHUNK, n_bundles, bundle_w).transpose(2, 0, 1, 3)
        yp = _ssm(xp.reshape(n_bundles, cr, SSM_COLS), lp, rows_per_batch)
        y = yp.reshape(n_bundles, cr, SSM_CHUNK, bundle_w).transpose(1, 2, 0, 3).reshape(t, SSM_WIDTH)

        h1, xn, ri, rw, cnt = _mixer_out(a, sg, y, h, lp, tm)

        counts = cnt[0, :N_EXPERTS].astype(jnp.int32)
        pcounts = (counts + EXPERT_BLOCK - 1) // EXPERT_BLOCK * EXPERT_BLOCK
        pend = jnp.cumsum(pcounts)
        pstart = (pend - pcounts).astype(jnp.int32)
        blk_first_row = jnp.arange(n_blocks, dtype=jnp.int32) * EXPERT_BLOCK
        blk_e = jnp.minimum(jnp.sum((pend[None, :] <= blk_first_row[:, None]).astype(jnp.int32), axis=1),
                            N_EXPERTS - 1)
        is_e = ri[:, 0:2, None] == jnp.arange(N_EXPERTS, dtype=jnp.int32)
        dest = jnp.sum(jnp.where(is_e, pstart, 0), axis=-1) + ri[:, 2:4]
        row_tok = _slot_map(dest.reshape(t // td, 1, 2 * td), n_rows, td)
        yb = _experts(row_tok, blk_e, xn, lp["w13"], lp["w2"])
        h = _combine(dest.reshape(t // tc, 1, 2 * tc), h1, rw, p_all, i, lp, yb, tc)
    return h.reshape(batch, seq, D_MODEL)
```

```python
import functools
import math

import jax
import jax.numpy as jnp
from jax import lax
from jax.experimental import pallas as pl
from jax.experimental.pallas import tpu as pltpu

F32 = jnp.float32
BF16 = jnp.bfloat16

D_MODEL = 1024
ATTN_HEADS = 8
NOPE_DIM = 64
ROPE_DIM = 32
QK_DIM = NOPE_DIM + ROPE_DIM
V_DIM = 64
Q_RANK = 384
KV_RANK = 256
SG_GROUPS = 4
SG_CH = 64
SG_WIDTH = SG_GROUPS * SG_CH
SG_CHUNK = 128
SSM_GROUPS = 16
SSM_CH = 16
SSM_WIDTH = SSM_GROUPS * SSM_CH
SSM_STATE = 64
MOE_GROUPS = 4
EXPERTS_PER_GROUP = 8
N_EXPERTS = MOE_GROUPS * EXPERTS_PER_GROUP
D_EXPERT = 256
PLE_DIM = 256
ROPE_THETA = 10000.0
EPS = 1e-6

LANES = 128
HEAD_PAD = LANES
QKV_PAD = ATTN_HEADS * HEAD_PAD
ATTN_HEADS_PER_STEP = 2
ATTN_BLOCKS_PER_ITER = 4
SSM_CHUNK = 16
SSM_BUNDLE = 4
SSM_COLS = SSM_BUNDLE * SSM_CHUNK * SSM_CH
SSM_SLANES = SSM_BUNDLE * SSM_STATE
PACKED_ROWS = D_MODEL // 2 // LANES
ROUTE_COLS = 8
EXPERT_BLOCK = 256
EXPERT_GATHER_AHEAD = 2
SCALAR_LOOP_UNROLL = 8
VMEM_LIMIT = 56 * 1024 * 1024

C_Q = 0
C_KV = C_Q + Q_RANK
C_KRA = C_KV + KV_RANK
C_KRB = C_KRA + LANES
C_SG = C_KRB + LANES
C_SSM = C_SG + 2 * SG_WIDTH
C_END = C_SSM + SSM_WIDTH


def _dot(a, b):
    return jnp.dot(a, b, preferred_element_type=F32)


def _split_bf16(x):
    hi = x.astype(BF16)
    lo = (x - hi.astype(F32)).astype(BF16)
    return hi, lo


def _rms(x, g, width=None):
    width = x.shape[-1] if width is None else width
    ms = jnp.sum(x * x, axis=-1, keepdims=True) * (1.0 / width)
    return x * lax.rsqrt(ms + EPS) * g


def _mixer_in_kernel(h_ref, ln1_ref, win_ref, gcq_ref, gckv_ref, wqa_ref, wqb_ref, wk_ref, wv_ref,
                     cos_ref, sin_ref, gqk_ref, vones_ref, lng_ref, lnb_ref, sgw_ref, sgb_ref,
                     mmean_ref, q_out, k_out, v_out, sg_out, ssm_out, *, tm):
    xn = _rms(h_ref[...], ln1_ref[...])
    z = _dot(xn.astype(BF16), win_ref[...])

    cos = cos_ref[...]
    sin = sin_ref[...]
    gqk = gqk_ref[...]
    gqa = cos * gqk[0:1, :]
    gqb = sin * gqk[1:2, :]
    gka = cos * gqk[2:3, :]
    gkb = sin * gqk[3:4, :]
    scale = QK_DIM ** -0.5 * math.log2(math.e)

    qn = _rms(z[:, C_Q:C_KV], gcq_ref[...]).astype(BF16)
    qa = _dot(qn, wqa_ref[...])
    qb = _dot(qn, wqb_ref[...])
    for hd in range(ATTN_HEADS):
        sl = slice(hd * HEAD_PAD, (hd + 1) * HEAD_PAD)
        a = qa[:, sl]
        r = lax.rsqrt(jnp.sum(a * a, axis=-1, keepdims=True) * (1.0 / QK_DIM) + EPS)
        q_out[:, sl] = ((a * gqa + qb[:, sl] * gqb) * (r * scale)).astype(BF16)

    kvn = _rms(z[:, C_KV:C_KRA], gckv_ref[...]).astype(BF16)
    ka = _dot(kvn, wk_ref[...])
    v_out[...] = (_dot(kvn, wv_ref[...]) + vones_ref[...]).astype(BF16)
    kra = z[:, C_KRA:C_KRB]
    krb = z[:, C_KRB:C_SG]
    ss_rope = jnp.sum(kra * kra, axis=-1, keepdims=True)
    krot = kra * gka + krb * gkb
    for hd in range(ATTN_HEADS):
        sl = slice(hd * HEAD_PAD, (hd + 1) * HEAD_PAD)
        a = ka[:, sl]
        r = lax.rsqrt((jnp.sum(a * a, axis=-1, keepdims=True) + ss_rope) * (1.0 / QK_DIM) + EPS)
        k_out[:, sl] = ((a * gka + krot) * r).astype(BF16)

    zg = jax.nn.gelu(z[:, C_SG:C_SSM])
    u = zg[:, :SG_WIDTH]
    vv = zg[:, SG_WIDTH:]
    mmean = mmean_ref[...]
    vh, vl = _split_bf16(vv)
    mu = _dot(vh, mmean) + _dot(vl, mmean)
    xc = vv - mu
    sh, sl_ = _split_bf16(xc * xc)
    var = _dot(sh, mmean) + _dot(sl_, mmean)
    vn = (xc * lax.rsqrt(var + EPS) * lng_ref[...] + lnb_ref[...]).astype(BF16)
    tri = (lax.broadcasted_iota(jnp.int32, (SG_CHUNK, SG_CHUNK), 1)
           <= lax.broadcasted_iota(jnp.int32, (SG_CHUNK, SG_CHUNK), 0))
    lane_grp = lax.broadcasted_iota(jnp.int32, (SG_CHUNK, SG_WIDTH), 1) // SG_CH
    ws = [jnp.where(tri, sgw_ref[g], 0.0).astype(BF16) for g in range(SG_GROUPS)]
    for c in range(tm // SG_CHUNK):
        rows = slice(c * SG_CHUNK, (c + 1) * SG_CHUNK)
        vc = vn[rows, :]
        mixed = sgb_ref[...]
        for g in range(SG_GROUPS):
            mixed = mixed + jnp.where(lane_grp == g, _dot(ws[g], vc), 0.0)
        sg_out[rows, :] = u[rows, :] * mixed

    ssm_out[...] = z[:, C_SSM:C_END].astype(BF16)


def _mixer_in(h, lp, cos128, sin128, tm, n_pos_tiles):
    t = h.shape[0]
    const = lambda *shape: pl.BlockSpec(shape, lambda i: (0,) * len(shape))
    rows = lambda w: pl.BlockSpec((tm, w), lambda i: (i, 0))
    pos = pl.BlockSpec((tm, LANES), lambda i: (i % n_pos_tiles, 0))
    return pl.pallas_call(
        functools.partial(_mixer_in_kernel, tm=tm),
        grid=(t // tm,),
        in_specs=[rows(D_MODEL), const(1, D_MODEL), const(D_MODEL, C_END), const(1, Q_RANK),
                  const(1, KV_RANK), const(Q_RANK, QKV_PAD), const(Q_RANK, QKV_PAD),
                  const(KV_RANK, QKV_PAD), const(KV_RANK, QKV_PAD), pos, pos, const(8, LANES),
                  const(1, QKV_PAD), const(1, SG_WIDTH), const(1, SG_WIDTH),
                  const(SG_GROUPS, SG_CHUNK, SG_CHUNK), const(SG_CHUNK, SG_WIDTH),
                  const(SG_WIDTH, SG_WIDTH)],
        out_specs=[rows(QKV_PAD), rows(QKV_PAD), rows(QKV_PAD), rows(SG_WIDTH), rows(SSM_WIDTH)],
        out_shape=[jax.ShapeDtypeStruct((t, QKV_PAD), BF16), jax.ShapeDtypeStruct((t, QKV_PAD), BF16),
                   jax.ShapeDtypeStruct((t, QKV_PAD), BF16), jax.ShapeDtypeStruct((t, SG_WIDTH), F32),
                   jax.ShapeDtypeStruct((t, SSM_WIDTH), BF16)],
        compiler_params=pltpu.CompilerParams(dimension_semantics=("parallel",),
                                             vmem_limit_bytes=VMEM_LIMIT),
        name="mixer_in",
    )(h, lp["ln1"], lp["win"], lp["gcq"], lp["gckv"], lp["wqa"], lp["wqb"], lp["wk"], lp["wv"],
      cos128, sin128, lp["gqk"], lp["vones"], lp["sg_lng"], lp["sg_lnb"], lp["sg_w"], lp["sg_bias"],
      lp["mmean"])


def _attn_kernel(q_ref, k_ref, v_ref, o_ref, s_even, s_odd, m_ref, acc_ref, *, tq):
    i = pl.program_id(2)
    heads = [slice(hh * HEAD_PAD, (hh + 1) * HEAD_PAD) for hh in range(ATTN_HEADS_PER_STEP)]

    def block_rows(j):
        return pl.ds(pl.multiple_of(j * tq, tq), tq)

    def scores(j, s_ref):
        for hh, sl in enumerate(heads):
            s_ref[hh] = lax.dot_general(q_ref[:, sl], k_ref[block_rows(j), sl],
                                        (((1,), (1,)), ((), ())), preferred_element_type=F32)

    def accumulate(j, s_ref, masked):
        for hh, sl in enumerate(heads):
            s = s_ref[hh]
            if masked:
                keep = (lax.broadcasted_iota(jnp.int32, (tq, tq), 1)
                        <= lax.broadcasted_iota(jnp.int32, (tq, tq), 0))
                s = jnp.where(keep, s, -jnp.inf)
            m = m_ref[hh]
            m_new = jnp.maximum(m, jnp.max(s, axis=-1, keepdims=True))
            alpha = jnp.exp2(m - m_new)
            p = jnp.exp2(s - m_new).astype(BF16)
            m_ref[hh] = m_new
            acc_ref[hh] = alpha * acc_ref[hh] + _dot(p, v_ref[block_rows(j), sl])

    m_ref[...] = jnp.full(m_ref.shape, -jnp.inf, F32)
    acc_ref[...] = jnp.zeros(acc_ref.shape, F32)
    scores(0, s_even)

    def pipelined(t, _):
        j = ATTN_BLOCKS_PER_ITER * t
        for u in range(0, ATTN_BLOCKS_PER_ITER, 2):
            scores(j + u + 1, s_odd)
            accumulate(j + u, s_even, False)
            scores(j + u + 2, s_even)
            accumulate(j + u + 1, s_odd, False)
        return 0

    n_pipelined = i // ATTN_BLOCKS_PER_ITER
    lax.fori_loop(0, n_pipelined, pipelined, 0)

    def leftover(j, _):
        accumulate(j, s_even, False)
        scores(j + 1, s_even)
        return 0

    lax.fori_loop(n_pipelined * ATTN_BLOCKS_PER_ITER, i, leftover, 0)
    accumulate(i, s_even, True)

    lane = lax.broadcasted_iota(jnp.int32, (tq, HEAD_PAD), 1)
    for hh, sl in enumerate(heads):
        acc = acc_ref[hh]
        out = acc / acc[:, V_DIM:V_DIM + 1]
        o_ref[:, sl] = jnp.where(lane < V_DIM, out, 0.0).astype(BF16)


def _attention(q, k, v, batch, seq, tq):
    t = q.shape[0]
    nq = seq // tq
    width = ATTN_HEADS_PER_STEP * HEAD_PAD
    return pl.pallas_call(
        functools.partial(_attn_kernel, tq=tq),
        grid=(batch, ATTN_HEADS // ATTN_HEADS_PER_STEP, nq),
        in_specs=[pl.BlockSpec((tq, width), lambda b, h, i: (b * nq + i, h)),
                  pl.BlockSpec((seq, width), lambda b, h, i: (b, h)),
                  pl.BlockSpec((seq, width), lambda b, h, i: (b, h))],
        out_specs=pl.BlockSpec((tq, width), lambda b, h, i: (b * nq + i, h)),
        out_shape=jax.ShapeDtypeStruct((t, QKV_PAD), BF16),
        scratch_shapes=[pltpu.VMEM((ATTN_HEADS_PER_STEP, tq, tq), F32),
                        pltpu.VMEM((ATTN_HEADS_PER_STEP, tq, tq), F32),
                        pltpu.VMEM((ATTN_HEADS_PER_STEP, tq, 1), F32),
                        pltpu.VMEM((ATTN_HEADS_PER_STEP, tq, HEAD_PAD), F32)],
        compiler_params=pltpu.CompilerParams(
            dimension_semantics=("parallel", "parallel", "arbitrary"), vmem_limit_bytes=VMEM_LIMIT),
        name="mla_attention",
    )(q, k, v)


def _ssm_kernel(x_ref, m_ref, pre_ref, pim_ref, qre_ref, qim_ref, are_ref, aim_ref, y_ref, *,
                rows_per_batch, n_steps):
    x = x_ref[0]
    cr = x.shape[0]
    y_local = _dot(x, m_ref[0])
    sre = _dot(x, pre_ref[0])
    sim = _dot(x, pim_ref[0])
    row = lax.broadcasted_iota(jnp.int32, (cr, SSM_SLANES), 0) % rows_per_batch
    are = are_ref[0]
    aim = aim_ref[0]
    for k in range(n_steps):
        sh = 1 << k
        ok = row >= sh
        pr = jnp.where(ok, pltpu.roll(sre, sh, 0), 0.0)
        pi = jnp.where(ok, pltpu.roll(sim, sh, 0), 0.0)
        ar = are[k:k + 1, :]
        ai = aim[k:k + 1, :]
        sre, sim = sre + ar * pr - ai * pi, sim + ar * pi + ai * pr
    ok = row >= 1
    hre = jnp.where(ok, pltpu.roll(sre, 1, 0), 0.0).astype(BF16)
    him = jnp.where(ok, pltpu.roll(sim, 1, 0), 0.0).astype(BF16)
    y_ref[0] = (y_local + _dot(hre, qre_ref[0]) + _dot(him, qim_ref[0])).astype(BF16)


def _ssm(xp, lp, rows_per_batch):
    n_bundles, cr, _ = xp.shape
    n_steps = lp["ssm_are"].shape[1]
    blk = lambda *shape: pl.BlockSpec((1,) + shape, lambda p: (p,) + (0,) * len(shape))
    return pl.pallas_call(
        functools.partial(_ssm_kernel, rows_per_batch=rows_per_batch, n_steps=n_steps),
        grid=(n_bundles,),
        in_specs=[blk(cr, SSM_COLS), blk(SSM_COLS, SSM_COLS), blk(SSM_COLS, SSM_SLANES),
                  blk(SSM_COLS, SSM_SLANES), blk(SSM_SLANES, SSM_COLS), blk(SSM_SLANES, SSM_COLS),
                  blk(n_steps, SSM_SLANES), blk(n_steps, SSM_SLANES)],
        out_specs=blk(cr, SSM_COLS),
        out_shape=jax.ShapeDtypeStruct((n_bundles, cr, SSM_COLS), BF16),
        compiler_params=pltpu.CompilerParams(dimension_semantics=("parallel",),
                                             vmem_limit_bytes=VMEM_LIMIT),
        name="s5_mixer",
    )(xp, lp["ssm_m"], lp["ssm_pre"], lp["ssm_pim"], lp["ssm_qre"], lp["ssm_qim"], lp["ssm_are"],
      lp["ssm_aim"])


def _mixer_out_kernel(a_ref, sg_ref, y_ref, h_ref, gma_ref, gmg_ref, gms_ref, woa_ref, wog_ref,
                      wos_ref, wglu_ref, bglu_ref, ln2_ref, wrh_ref, wrl_ref, br_ref, ltri_ref,
                      h1_out, xn_out, ri_out, rw_out, cnt_out, run_ref):
    i = pl.program_id(0)

    @pl.when(i == 0)
    def _():
        run_ref[...] = jnp.zeros_like(run_ref)

    g = jax.nn.gelu(y_ref[...].astype(F32))
    glu = g * jax.nn.sigmoid(_dot(g.astype(BF16), wglu_ref[...]) + bglu_ref[...])
    a_n = _rms(a_ref[...].astype(F32), gma_ref[...], ATTN_HEADS * V_DIM)
    g_n = _rms(sg_ref[...], gmg_ref[...])
    s_n = _rms(glu, gms_ref[...])
    h1 = (h_ref[...] + _dot(a_n.astype(BF16), woa_ref[...]) + _dot(g_n.astype(BF16), wog_ref[...])
          + _dot(s_n.astype(BF16), wos_ref[...]))
    h1_out[...] = h1
    xn = _rms(h1, ln2_ref[...])
    _store_token_rows(xn_out, _pack_bf16_pairs(xn))

    xh, xl = _split_bf16(xn)
    logit = _dot(xh, wrh_ref[...]) + _dot(xl, wrh_ref[...]) + _dot(xh, wrl_ref[...]) + br_ref[...]
    shape = logit.shape
    lane = lax.broadcasted_iota(jnp.int32, shape, 1)
    neg = -jnp.inf
    is_grp = lane < MOE_GROUPS
    gl = jnp.where(is_grp, logit, neg)
    gmax = jnp.max(gl, axis=-1, keepdims=True)
    gsel = jnp.min(jnp.where(gl == gmax, lane, LANES), axis=-1, keepdims=True)
    gsum = jnp.sum(jnp.where(is_grp, jnp.exp(logit - gmax), 0.0), axis=-1, keepdims=True)
    g_w = 1.0 / gsum
    lo = MOE_GROUPS + gsel * EXPERTS_PER_GROUP
    in_grp = (lane >= lo) & (lane < lo + EXPERTS_PER_GROUP)
    el = jnp.where(in_grp, logit, neg)
    m1 = jnp.max(el, axis=-1, keepdims=True)
    i1 = jnp.min(jnp.where(el == m1, lane, LANES), axis=-1, keepdims=True)
    el2 = jnp.where(lane == i1, neg, el)
    m2 = jnp.max(el2, axis=-1, keepdims=True)
    i2 = jnp.min(jnp.where(el2 == m2, lane, LANES), axis=-1, keepdims=True)
    e2 = jnp.exp(m2 - m1)
    w1 = g_w / (1.0 + e2)
    w2 = g_w * e2 / (1.0 + e2)
    e_a = i1 - MOE_GROUPS
    e_b = i2 - MOE_GROUPS

    hit_a = lane == e_a
    hit_b = lane == e_b
    onehot = jnp.where(hit_a | hit_b, 1.0, 0.0).astype(BF16)
    before = _dot(ltri_ref[...], onehot) + run_ref[...]
    c_a = jnp.sum(jnp.where(hit_a, before, 0.0), axis=-1, keepdims=True).astype(jnp.int32)
    c_b = jnp.sum(jnp.where(hit_b, before, 0.0), axis=-1, keepdims=True).astype(jnp.int32)
    run_ref[...] = run_ref[...] + jnp.sum(onehot.astype(F32), axis=0, keepdims=True)
    cnt_out[...] = jnp.broadcast_to(run_ref[...], cnt_out.shape)

    ri_out[...] = jnp.where(lane == 0, e_a, jnp.where(lane == 1, e_b, jnp.where(
        lane == 2, c_a, jnp.where(lane == 3, c_b, 0))))[:, :ROUTE_COLS]
    rw_out[...] = jnp.where(lane == 0, w1, jnp.where(lane == 1, w2, 0.0))[:, :ROUTE_COLS]


def _mixer_out(a, sg, y, h, lp, tm):
    t = h.shape[0]
    const = lambda *shape: pl.BlockSpec(shape, lambda i: (0,) * len(shape))
    rows = lambda w: pl.BlockSpec((tm, w), lambda i: (i, 0))
    return pl.pallas_call(
        _mixer_out_kernel,
        grid=(t // tm,),
        in_specs=[rows(QKV_PAD), rows(SG_WIDTH), rows(SSM_WIDTH), rows(D_MODEL), const(1, QKV_PAD),
                  const(1, SG_WIDTH), const(1, SSM_WIDTH), const(QKV_PAD, D_MODEL),
                  const(SG_WIDTH, D_MODEL), const(SSM_WIDTH, D_MODEL), const(SSM_WIDTH, SSM_WIDTH),
                  const(1, SSM_WIDTH), const(1, D_MODEL), const(D_MODEL, LANES), const(D_MODEL, LANES),
                  const(1, LANES), const(tm, tm)],
        out_specs=[rows(D_MODEL), pl.BlockSpec((tm * PACKED_ROWS, LANES), lambda i: (i, 0)),
                   rows(ROUTE_COLS), rows(ROUTE_COLS), const(8, LANES)],
        out_shape=[jax.ShapeDtypeStruct((t, D_MODEL), F32),
                   jax.ShapeDtypeStruct((t * PACKED_ROWS, LANES), jnp.uint32),
                   jax.ShapeDtypeStruct((t, ROUTE_COLS), jnp.int32),
                   jax.ShapeDtypeStruct((t, ROUTE_COLS), F32),
                   jax.ShapeDtypeStruct((8, LANES), F32)],
        scratch_shapes=[pltpu.VMEM((1, LANES), F32)],
        compiler_params=pltpu.CompilerParams(dimension_semantics=("arbitrary",),
                                             vmem_limit_bytes=VMEM_LIMIT),
        name="mixer_out_router",
    )(a, sg, y, h, lp["gm_a"], lp["gm_g"], lp["gm_s"], lp["wo_a"], lp["wo_g"], lp["wo_s"], lp["wglu"],
      lp["bglu"], lp["ln2"], lp["wr_hi"], lp["wr_lo"], lp["br"], lp["ltri"])


def _row_copy(src_ref, src_row, dst_ref, dst_row, sem):
    src = src_ref.at[pl.ds(pl.multiple_of(src_row * PACKED_ROWS, PACKED_ROWS), PACKED_ROWS)]
    return pltpu.make_async_copy(src, dst_ref.at[pl.ds(dst_row * PACKED_ROWS, PACKED_ROWS)], sem)


def _rows_wait(src_ref, dst_ref, n_tokens, sem):
    rows = pl.ds(0, n_tokens * PACKED_ROWS)
    pltpu.make_async_copy(src_ref.at[rows], dst_ref.at[rows], sem).wait()


def _store_token_rows(ref, words):
    n = words.shape[0]
    for c in range(PACKED_ROWS):
        ref[pl.ds(c, n, stride=PACKED_ROWS), :] = words[:, c * LANES:(c + 1) * LANES]


def _load_token_rows(ref):
    n = ref.shape[0] // PACKED_ROWS
    return jnp.concatenate([ref[pl.ds(c, n, stride=PACKED_ROWS), :] for c in range(PACKED_ROWS)], axis=1)


def _slot_map_kernel(dest_ref, zeros_ref, row_tok_ref, sem, *, td):
    i = pl.program_id(0)

    @pl.when(i == 0)
    def _():
        clear = pltpu.make_async_copy(zeros_ref, row_tok_ref, sem)
        clear.start()
        clear.wait()

    def place(r, _):
        row_tok_ref[dest_ref[0, 0, 2 * r]] = i * td + r
        row_tok_ref[dest_ref[0, 0, 2 * r + 1]] = i * td + r
        return 0

    lax.fori_loop(0, td, place, 0, unroll=SCALAR_LOOP_UNROLL)


def _slot_map(dest, n_rows, td):
    n_tiles = dest.shape[0]
    return pl.pallas_call(
        functools.partial(_slot_map_kernel, td=td),
        grid=(n_tiles,),
        in_specs=[pl.BlockSpec((1, 1, 2 * td), lambda i: (i, 0, 0), memory_space=pltpu.SMEM),
                  pl.BlockSpec(memory_space=pl.ANY)],
        out_specs=pl.BlockSpec(memory_space=pltpu.SMEM),
        out_shape=jax.ShapeDtypeStruct((n_rows,), jnp.int32),
        scratch_shapes=[pltpu.SemaphoreType.DMA(())],
        compiler_params=pltpu.CompilerParams(dimension_semantics=("arbitrary",)),
        name="moe_slot_map",
    )(dest, jnp.zeros((n_rows,), jnp.int32))


def _unpack_bf16_pairs(u, dtype):
    lo = lax.bitcast_convert_type(u << 16, F32)
    hi = lax.bitcast_convert_type(u & jnp.uint32(0xFFFF0000), F32)
    return jnp.concatenate([lo, hi], axis=1).astype(dtype)


def _pack_bf16_pairs(x):
    n = x.shape[1] // 2
    bits = lax.bitcast_convert_type(x.astype(BF16).astype(F32), jnp.uint32)
    return (bits[:, :n] >> 16) | (bits[:, n:] & jnp.uint32(0xFFFF0000))


def _expert_kernel(row_tok_ref, blk_e_ref, x_ref, w13_ref, w2_ref, yb_ref, xbuf, sem):
    del blk_e_ref
    b = pl.program_id(0)
    last = pl.num_programs(0) - 1
    n_buf = EXPERT_GATHER_AHEAD + 1

    def gather(block, to_slot):
        base = block * EXPERT_BLOCK
        for r in range(EXPERT_BLOCK):
            _row_copy(x_ref, row_tok_ref[base + r], xbuf.at[to_slot], r,
                      sem.at[to_slot]).start(priority=r % 2)

    def wait(of_slot):
        _rows_wait(x_ref, xbuf.at[of_slot], EXPERT_BLOCK, sem.at[of_slot])

    @pl.when(b == 0)
    def _():
        for ahead in range(EXPERT_GATHER_AHEAD):
            gather(ahead, ahead)

    slot = b % n_buf
    wait(slot)
    gather(jnp.minimum(b + EXPERT_GATHER_AHEAD, last), (b + EXPERT_GATHER_AHEAD) % n_buf)
    hh = _dot(_unpack_bf16_pairs(_load_token_rows(xbuf.at[slot]), BF16), w13_ref[...])
    hb = jax.nn.silu(hh[:, :D_EXPERT]) * hh[:, D_EXPERT:]
    _store_token_rows(yb_ref, _pack_bf16_pairs(_dot(hb.astype(BF16), w2_ref[...])))

    @pl.when(b == last)
    def _():
        for ahead in range(1, n_buf):
            wait((b + ahead) % n_buf)


def _experts(row_tok, blk_e, xn_packed, w13, w2):
    n_rows = row_tok.shape[0]
    grid_spec = pltpu.PrefetchScalarGridSpec(
        num_scalar_prefetch=2,
        grid=(n_rows // EXPERT_BLOCK,),
        in_specs=[pl.BlockSpec(memory_space=pl.ANY),
                  pl.BlockSpec((None, D_MODEL, 2 * D_EXPERT), lambda i, rt, be: (be[i], 0, 0)),
                  pl.BlockSpec((None, D_EXPERT, D_MODEL), lambda i, rt, be: (be[i], 0, 0))],
        out_specs=pl.BlockSpec((EXPERT_BLOCK * PACKED_ROWS, LANES), lambda i, rt, be: (i, 0)),
        scratch_shapes=[pltpu.VMEM((EXPERT_GATHER_AHEAD + 1, EXPERT_BLOCK * PACKED_ROWS, LANES), jnp.uint32),
                        pltpu.SemaphoreType.DMA((EXPERT_GATHER_AHEAD + 1,))],
    )
    return pl.pallas_call(
        _expert_kernel,
        grid_spec=grid_spec,
        out_shape=jax.ShapeDtypeStruct((n_rows * PACKED_ROWS, LANES), jnp.uint32),
        compiler_params=pltpu.CompilerParams(dimension_semantics=("arbitrary",),
                                             vmem_limit_bytes=VMEM_LIMIT),
        name="moe_experts",
    )(row_tok, blk_e, xn_packed, w13, w2)


def _combine_kernel(dest_ref, dest_next_ref, h_ref, rw_ref, p_ref, wpg_ref, bpg_ref,
                    wple_ref, yb_ref, out_ref, y0_ref, y1_ref, sem, *, tc):
    i = pl.program_id(0)
    last = pl.num_programs(0) - 1
    slot = i % 2

    def gather(tile_dest_ref, to_slot):
        for r in range(tc):
            _row_copy(yb_ref, tile_dest_ref[0, 0, 2 * r], y0_ref.at[to_slot], r, sem.at[to_slot]).start()
            _row_copy(yb_ref, tile_dest_ref[0, 0, 2 * r + 1], y1_ref.at[to_slot], r,
                      sem.at[to_slot]).start()

    def wait(of_slot):
        _rows_wait(yb_ref, y0_ref.at[of_slot], tc, sem.at[of_slot])
        _rows_wait(yb_ref, y1_ref.at[of_slot], tc, sem.at[of_slot])

    @pl.when(i == 0)
    def _():
        gather(dest_ref, 0)

    wait(slot)
    gather(dest_next_ref, 1 - slot)

    rw = rw_ref[...]
    y0 = _unpack_bf16_pairs(_load_token_rows(y0_ref.at[slot]), F32)
    y1 = _unpack_bf16_pairs(_load_token_rows(y1_ref.at[slot]), F32)
    h2 = h_ref[...] + rw[:, 0:1] * y0 + rw[:, 1:2] * y1
    gate = jax.nn.sigmoid(_dot(h2.astype(BF16), wpg_ref[...]) + bpg_ref[...])
    out_ref[...] = h2 + gate * _dot(p_ref[...].astype(BF16), wple_ref[...])

    @pl.when(i == last)
    def _():
        wait(1 - slot)


def _combine(dest, h1, rw, p_all, layer, lp, yb, tc):
    t = h1.shape[0]
    n_tiles = t // tc
    const = lambda *shape: pl.BlockSpec(shape, lambda i: (0,) * len(shape))
    rows = lambda w: pl.BlockSpec((tc, w), lambda i: (i, 0))
    p_rows = pl.BlockSpec((tc, PLE_DIM), lambda i: (layer * n_tiles + i, 0))
    grid_spec = pl.GridSpec(
        grid=(n_tiles,),
        in_specs=[pl.BlockSpec((1, 1, 2 * tc), lambda i: (i, 0, 0), memory_space=pltpu.SMEM),
                  pl.BlockSpec((1, 1, 2 * tc), lambda i: (jnp.minimum(i + 1, n_tiles - 1), 0, 0),
                               memory_space=pltpu.SMEM),
                  rows(D_MODEL), rows(ROUTE_COLS), p_rows, const(D_MODEL, D_MODEL), const(1, D_MODEL),
                  const(PLE_DIM, D_MODEL), pl.BlockSpec(memory_space=pl.ANY)],
        out_specs=rows(D_MODEL),
        scratch_shapes=[pltpu.VMEM((2, tc * PACKED_ROWS, LANES), jnp.uint32),
                        pltpu.VMEM((2, tc * PACKED_ROWS, LANES), jnp.uint32),
                        pltpu.SemaphoreType.DMA((2,))],
    )
    return pl.pallas_call(
        functools.partial(_combine_kernel, tc=tc),
        grid_spec=grid_spec,
        out_shape=jax.ShapeDtypeStruct((t, D_MODEL), F32),
        compiler_params=pltpu.CompilerParams(dimension_semantics=("arbitrary",),
                                             vmem_limit_bytes=VMEM_LIMIT),
        name="moe_combine_ple",
    )(dest, dest, h1, rw, p_all, lp["wpg"], lp["bpg"], lp["wple"], yb)


def _rope_lane_tables(seq):
    inv = 1.0 / (ROPE_THETA ** (jnp.arange(0, ROPE_DIM, 2, dtype=F32) / ROPE_DIM))
    ang = jnp.arange(seq, dtype=F32)[:, None] * inv[None, :]
    cos, sin = jnp.cos(ang), jnp.sin(ang)
    half = ROPE_DIM // 2
    cos128 = jnp.concatenate([jnp.ones((seq, NOPE_DIM), F32), cos, cos,
                              jnp.zeros((seq, LANES - QK_DIM), F32)], axis=1)
    sin128 = jnp.concatenate([jnp.zeros((seq, NOPE_DIM), F32), -sin, sin,
                              jnp.zeros((seq, LANES - QK_DIM), F32)], axis=1)
    del half
    return cos128, sin128


def _swap_rope_halves(w):
    half = ROPE_DIM // 2
    return jnp.concatenate([w[..., half:], w[..., :half]], axis=-1)


def _pad_heads(w, width):
    return jnp.pad(w, ((0, 0), (0, 0), (0, HEAD_PAD - width))).reshape(w.shape[0], QKV_PAD)


def _rope_gain_rows(g):
    pad = jnp.zeros((LANES - QK_DIM,), F32)
    ga = jnp.concatenate([g, pad])
    gb = jnp.concatenate([jnp.zeros((NOPE_DIM,), F32), _swap_rope_halves(g[NOPE_DIM:]), pad])
    return ga, gb


def _ssm_tables(lam_re, lam_im, log_dt, b_re, b_im, c_re, c_im, d_skip, n_steps):
    hp = lax.Precision.HIGHEST
    lr = jnp.minimum(lam_re, -1e-4)
    li = lam_im
    dt = jnp.exp(log_dt)
    mag = jnp.exp(lr * dt)
    ab_re, ab_im = mag * jnp.cos(li * dt), mag * jnp.sin(li * dt)
    den = lr * lr + li * li
    nr, ni = ab_re - 1.0, ab_im
    fr = (nr * lr + ni * li) / den
    fi = (ni * lr - nr * li) / den
    bb_re = fr[..., None] * b_re - fi[..., None] * b_im
    bb_im = fr[..., None] * b_im + fi[..., None] * b_re

    steps = jnp.arange(SSM_CHUNK + 1, dtype=F32)[:, None, None]
    pmag = jnp.exp(lr * dt * steps)
    pw_re, pw_im = pmag * jnp.cos(li * dt * steps), pmag * jnp.sin(li * dt * steps)

    w_re = pw_re[:SSM_CHUNK, :, :, None] * bb_re - pw_im[:SSM_CHUNK, :, :, None] * bb_im
    w_im = pw_re[:SSM_CHUNK, :, :, None] * bb_im + pw_im[:SSM_CHUNK, :, :, None] * bb_re
    taps = (jnp.einsum("gcn,tgnd->tgcd", c_re, w_re, precision=hp)
            - jnp.einsum("gcn,tgnd->tgcd", c_im, w_im, precision=hp))
    taps = taps.at[0].add(d_skip[:, :, None] * jnp.eye(SSM_CH, dtype=F32)[None])
    t_in = jnp.arange(SSM_CHUNK)[:, None]
    t_out = jnp.arange(SSM_CHUNK)[None, :]
    lag = t_out - t_in
    m = jnp.where((lag >= 0)[:, :, None, None, None], taps[jnp.clip(lag, 0)], 0.0)
    m = m.transpose(2, 0, 4, 1, 3)

    p_re = w_re[::-1].transpose(1, 0, 3, 2)
    p_im = w_im[::-1].transpose(1, 0, 3, 2)

    up_re, up_im = pw_re[1:], pw_im[1:]
    q_re = (c_re[None] * up_re[:, :, None, :] - c_im[None] * up_im[:, :, None, :])
    q_im = -(c_re[None] * up_im[:, :, None, :] + c_im[None] * up_re[:, :, None, :])
    q_re = q_re.transpose(1, 3, 0, 2)
    q_im = q_im.transpose(1, 3, 0, 2)

    are, aim = [pw_re[SSM_CHUNK]], [pw_im[SSM_CHUNK]]
    for _ in range(n_steps - 1):
        r, im = are[-1], aim[-1]
        are.append(r * r - im * im)
        aim.append(2.0 * r * im)
    are, aim = jnp.stack(are, axis=1), jnp.stack(aim, axis=1)

    nbu = SSM_GROUPS // SSM_BUNDLE
    per_group = SSM_CHUNK * SSM_CH
    bundle = lambda x: x.reshape((nbu, SSM_BUNDLE) + x.shape[1:])
    col = jnp.arange(SSM_COLS)
    col_src = (col // (SSM_BUNDLE * SSM_CH)) * SSM_CH + col % SSM_CH
    col_grp = (col // SSM_CH) % SSM_BUNDLE
    sel_io = ((col_grp[None, None, :] == jnp.arange(SSM_BUNDLE)[:, None, None])
              & (col_src[None, None, :] == jnp.arange(per_group)[None, :, None])).astype(F32)
    lane = jnp.arange(SSM_SLANES)
    sel_st = ((lane[None, None, :] // SSM_STATE == jnp.arange(SSM_BUNDLE)[:, None, None])
              & (lane[None, None, :] % SSM_STATE == jnp.arange(SSM_STATE)[None, :, None])).astype(F32)

    def expand(x, rows_sel, cols_sel):
        return jnp.einsum("gaR,qgab,gbS->qRS", rows_sel, bundle(x), cols_sel).astype(BF16)

    def lanes(x):
        return bundle(x).transpose(0, 2, 1, 3).reshape(nbu, x.shape[1], SSM_SLANES)

    flat_io = lambda x: x.reshape(SSM_GROUPS, per_group, -1)
    return dict(ssm_m=expand(m.reshape(SSM_GROUPS, per_group, per_group), sel_io, sel_io),
                ssm_pre=expand(flat_io(p_re), sel_io, sel_st), ssm_pim=expand(flat_io(p_im), sel_io, sel_st),
                ssm_qre=expand(q_re.reshape(SSM_GROUPS, SSM_STATE, per_group), sel_st, sel_io),
                ssm_qim=expand(q_im.reshape(SSM_GROUPS, SSM_STATE, per_group), sel_st, sel_io),
                ssm_are=lanes(are), ssm_aim=lanes(aim))


def _layer_params(i, tm, n_scan_steps, ln1, w_in, g_cq, g_ckv, w_uq, w_ukv, g_qh, g_kh, sg_ln_g,
                  sg_ln_b, sg_w, sg_b, lam_re, lam_im, log_dt, ssm_b_re, ssm_b_im, ssm_c_re, ssm_c_im,
                  ssm_d, w_glu, b_glu, g_mix, w_out, ln2, w_rg, b_rg, w_rexp, b_rexp, w1, w3, w2, w_pg,
                  b_pg, w_ple):
    lp = {}
    wi = w_in[i]
    off_kr = Q_RANK + KV_RANK
    off_sg = off_kr + ROPE_DIM
    kr = wi[:, off_kr:off_sg]
    zpad = lambda n: jnp.zeros((D_MODEL, n), F32)
    kra = jnp.concatenate([zpad(NOPE_DIM), kr, zpad(LANES - QK_DIM)], axis=1)
    krb = jnp.concatenate([zpad(NOPE_DIM), _swap_rope_halves(kr), zpad(LANES - QK_DIM)], axis=1)
    lp["win"] = jnp.concatenate([wi[:, :off_kr], kra, krb, wi[:, off_sg:]], axis=1).astype(BF16)
    lp["ln1"] = ln1[i][None]
    lp["gcq"] = g_cq[i][None]
    lp["gckv"] = g_ckv[i][None]

    wq = w_uq[i].reshape(Q_RANK, ATTN_HEADS, QK_DIM)
    lp["wqa"] = _pad_heads(wq, QK_DIM).astype(BF16)
    wq_sw = jnp.concatenate([jnp.zeros((Q_RANK, ATTN_HEADS, NOPE_DIM), F32),
                             _swap_rope_halves(wq[..., NOPE_DIM:])], axis=-1)
    lp["wqb"] = _pad_heads(wq_sw, QK_DIM).astype(BF16)
    wkv = w_ukv[i].reshape(KV_RANK, ATTN_HEADS, NOPE_DIM + V_DIM)
    lp["wk"] = _pad_heads(wkv[..., :NOPE_DIM], NOPE_DIM).astype(BF16)
    lp["wv"] = _pad_heads(wkv[..., NOPE_DIM:], V_DIM).astype(BF16)
    gq_a, gq_b = _rope_gain_rows(g_qh[i])
    gk_a, gk_b = _rope_gain_rows(g_kh[i])
    lp["gqk"] = jnp.concatenate([jnp.stack([gq_a, gq_b, gk_a, gk_b]), jnp.zeros((4, LANES), F32)])
    lp["vones"] = jnp.tile((jnp.arange(HEAD_PAD) == V_DIM).astype(F32), ATTN_HEADS)[None]

    lp["sg_lng"] = sg_ln_g[i][None]
    lp["sg_lnb"] = sg_ln_b[i][None]
    lp["sg_w"] = sg_w[i]
    lp["sg_bias"] = jnp.repeat(sg_b[i].T, SG_CH, axis=1)
    grp = jnp.arange(SG_WIDTH) // SG_CH
    lp["mmean"] = ((grp[:, None] == grp[None, :]).astype(F32) / SG_CH).astype(BF16)

    lp.update(_ssm_tables(lam_re[i], lam_im[i], log_dt[i], ssm_b_re[i], ssm_b_im[i], ssm_c_re[i],
                          ssm_c_im[i], ssm_d[i], n_scan_steps))

    gm = g_mix[i]
    aw = ATTN_HEADS * V_DIM
    lp["gm_a"] = _pad_heads(gm[:aw].reshape(1, ATTN_HEADS, V_DIM), V_DIM)
    lp["gm_g"] = gm[aw:aw + SG_WIDTH][None]
    lp["gm_s"] = gm[aw + SG_WIDTH:][None]
    wo = w_out[i]
    woa = wo[:aw].reshape(ATTN_HEADS, V_DIM, D_MODEL)
    lp["wo_a"] = jnp.pad(woa, ((0, 0), (0, HEAD_PAD - V_DIM), (0, 0))).reshape(QKV_PAD, D_MODEL).astype(BF16)
    lp["wo_g"] = wo[aw:aw + SG_WIDTH].astype(BF16)
    lp["wo_s"] = wo[aw + SG_WIDTH:].astype(BF16)
    lp["wglu"] = w_glu[i].astype(BF16)
    lp["bglu"] = b_glu[i][None]
    lp["ln2"] = ln2[i][None]
    wr = jnp.concatenate([w_rg[i], w_rexp[i], jnp.zeros((D_MODEL, LANES - MOE_GROUPS - N_EXPERTS), F32)],
                         axis=1)
    lp["wr_hi"], lp["wr_lo"] = _split_bf16(wr)
    lp["br"] = jnp.concatenate([b_rg[i], b_rexp[i], jnp.zeros((LANES - MOE_GROUPS - N_EXPERTS,), F32)])[None]
    lp["ltri"] = (jnp.arange(tm)[None, :] < jnp.arange(tm)[:, None]).astype(BF16)

    lp["w13"] = jnp.concatenate([w1[i], w3[i]], axis=-1).astype(BF16)
    lp["w2"] = w2[i].astype(BF16)
    lp["wpg"] = w_pg[i].astype(BF16)
    lp["bpg"] = b_pg[i][None]
    lp["wple"] = w_ple[i].astype(BF16)
    return lp


def kernel(x, p, ln1, w_in, g_cq, g_ckv, w_uq, w_ukv, g_qh, g_kh, sg_ln_g, sg_ln_b, sg_w, sg_b, lam_re, lam_im, log_dt, ssm_b_re, ssm_b_im, ssm_c_re, ssm_c_im, ssm_d, w_glu, b_glu, g_mix, w_out, ln2, w_rg, b_rg, w_rexp, b_rexp, w1, w3, w2, w_pg, b_pg, w_ple):
    batch, seq, _ = x.shape
    depth = p.shape[0]
    t = batch * seq
    tm = min(512, seq)
    tq = min(512, seq)
    td = min(1024, seq)
    tc = min(256, seq)
    assert seq % tm == 0 and seq % SSM_CHUNK == 0 and tm % SG_CHUNK == 0
    rows_per_batch = seq // SSM_CHUNK
    n_scan_steps = max(1, (rows_per_batch - 1).bit_length())
    cr = t // SSM_CHUNK
    n_bundles = SSM_GROUPS // SSM_BUNDLE
    bundle_w = SSM_BUNDLE * SSM_CH
    tk2 = 2 * t
    n_blocks = -(-(tk2 + N_EXPERTS * (EXPERT_BLOCK - 1)) // EXPERT_BLOCK)
    n_rows = n_blocks * EXPERT_BLOCK

    cos128, sin128 = _rope_lane_tables(seq)
    h = x.reshape(t, D_MODEL)
    p_all = p.reshape(depth * t, PLE_DIM)
    weights = (ln1, w_in, g_cq, g_ckv, w_uq, w_ukv, g_qh, g_kh, sg_ln_g, sg_ln_b, sg_w, sg_b, lam_re,
               lam_im, log_dt, ssm_b_re, ssm_b_im, ssm_c_re, ssm_c_im, ssm_d, w_glu, b_glu, g_mix, w_out,
               ln2, w_rg, b_rg, w_rexp, b_rexp, w1, w3, w2, w_pg, b_pg, w_ple)
    for i in range(depth):
        lp = _layer_params(i, tm, n_scan_steps, *weights)
        q, k, v, sg, xs_ssm = _mixer_in(h, lp, cos128, sin128, tm, seq // tm)
        a = _attention(q, k, v, batch, seq, tq)

        xp = xs_ssm.reshape(cr, SSM_CHUNK, n_bundles, bundle_w).transpose(2, 0, 1, 3)
        yp = _ssm(xp.reshape(n_bundles, cr, SSM_COLS), lp, rows_per_batch)
        y = yp.reshape(n_bundles, cr, SSM_CHUNK, bundle_w).transpose(1, 2, 0, 3).reshape(t, SSM_WIDTH)

        h1, xn, ri, rw, cnt = _mixer_out(a, sg, y, h, lp, tm)

        counts = cnt[0, :N_EXPERTS].astype(jnp.int32)
        pcounts = (counts + EXPERT_BLOCK - 1) // EXPERT_BLOCK * EXPERT_BLOCK
        pend = jnp.cumsum(pcounts)
        pstart = (pend - pcounts).astype(jnp.int32)
        blk_first_row = jnp.arange(n_blocks, dtype=jnp.int32) * EXPERT_BLOCK
        blk_e = jnp.minimum(jnp.sum((pend[None, :] <= blk_first_row[:, None]).astype(jnp.int32), axis=1),
                            N_EXPERTS - 1)
        is_e = ri[:, 0:2, None] == jnp.arange(N_EXPERTS, dtype=jnp.int32)
        dest = jnp.sum(jnp.where(is_e, pstart, 0), axis=-1) + ri[:, 2:4]
        row_tok = _slot_map(dest.reshape(t // td, 1, 2 * td), n_rows, td)
        yb = _experts(row_tok, blk_e, xn, lp["w13"], lp["w2"])
        h = _combine(dest.reshape(t // tc, 1, 2 * tc), h1, rw, p_all, i, lp, yb, tc)
    return h.reshape(batch, seq, D_MODEL)
```

```python
import functools
import math

import jax
import jax.numpy as jnp
from jax import lax
from jax.experimental import pallas as pl
from jax.experimental.pallas import tpu as pltpu

F32 = jnp.float32
BF16 = jnp.bfloat16

D_MODEL = 1024
ATTN_HEADS = 8
NOPE_DIM = 64
ROPE_DIM = 32
QK_DIM = NOPE_DIM + ROPE_DIM
V_DIM = 64
Q_RANK = 384
KV_RANK = 256
SG_GROUPS = 4
SG_CH = 64
SG_WIDTH = SG_GROUPS * SG_CH
SG_CHUNK = 128
SSM_GROUPS = 16
SSM_CH = 16
SSM_WIDTH = SSM_GROUPS * SSM_CH
SSM_STATE = 64
MOE_GROUPS = 4
EXPERTS_PER_GROUP = 8
N_EXPERTS = MOE_GROUPS * EXPERTS_PER_GROUP
D_EXPERT = 256
PLE_DIM = 256
ROPE_THETA = 10000.0
EPS = 1e-6

LANES = 128
HEAD_PAD = LANES
QKV_PAD = ATTN_HEADS * HEAD_PAD
ATTN_HEADS_PER_STEP = 2
ATTN_BLOCKS_PER_ITER = 4
SSM_CHUNK = 16
SSM_BUNDLE = 4
SSM_COLS = SSM_BUNDLE * SSM_CHUNK * SSM_CH
SSM_SLANES = SSM_BUNDLE * SSM_STATE
PACKED_ROWS = D_MODEL // 2 // LANES
ROUTE_COLS = 8
EXPERT_BLOCK = 256
EXPERT_GATHER_AHEAD = 2
SCALAR_LOOP_UNROLL = 8
VMEM_LIMIT = 56 * 1024 * 1024

C_Q = 0
C_KV = C_Q + Q_RANK
C_KRA = C_KV + KV_RANK
C_KRB = C_KRA + LANES
C_SG = C_KRB + LANES
C_SSM = C_SG + 2 * SG_WIDTH
C_END = C_SSM + SSM_WIDTH


def _dot(a, b):
    return jnp.dot(a, b, preferred_element_type=F32)


def _split_bf16(x):
    hi = x.astype(BF16)
    lo = (x - hi.astype(F32)).astype(BF16)
    return hi, lo


def _rms(x, g, width=None):
    width = x.shape[-1] if width is None else width
    ms = jnp.sum(x * x, axis=-1, keepdims=True) * (1.0 / width)
    return x * lax.rsqrt(ms + EPS) * g


def _mixer_in_kernel(h_ref, ln1_ref, win_ref, gcq_ref, gckv_ref, wqa_ref, wqb_ref, wk_ref, wv_ref,
                     cos_ref, sin_ref, gqk_ref, vones_ref, lng_ref, lnb_ref, sgw_ref, sgb_ref,
                     mmean_ref, q_out, k_out, v_out, sg_out, ssm_out, *, tm):
    xn = _rms(h_ref[...], ln1_ref[...])
    z = _dot(xn.astype(BF16), win_ref[...])

    cos = cos_ref[...]
    sin = sin_ref[...]
    gqk = gqk_ref[...]
    gqa = cos * gqk[0:1, :]
    gqb = sin * gqk[1:2, :]
    gka = cos * gqk[2:3, :]
    gkb = sin * gqk[3:4, :]
    scale = QK_DIM ** -0.5 * math.log2(math.e)

    qn = _rms(z[:, C_Q:C_KV], gcq_ref[...]).astype(BF16)
    qa = _dot(qn, wqa_ref[...])
    qb = _dot(qn, wqb_ref[...])
    for hd in range(ATTN_HEADS):
        sl = slice(hd * HEAD_PAD, (hd + 1) * HEAD_PAD)
        a = qa[:, sl]
        r = lax.rsqrt(jnp.sum(a * a, axis=-1, keepdims=True) * (1.0 / QK_DIM) + EPS)
        q_out[:, sl] = ((a * gqa + qb[:, sl] * gqb) * (r * scale)).astype(BF16)

    kvn = _rms(z[:, C_KV:C_KRA], gckv_ref[...]).astype(BF16)
    ka = _dot(kvn, wk_ref[...])
    v_out[...] = (_dot(kvn, wv_ref[...]) + vones_ref[...]).astype(BF16)
    kra = z[:, C_KRA:C_KRB]
    krb = z[:, C_KRB:C_SG]
    ss_rope = jnp.sum(kra * kra, axis=-1, keepdims=True)
    krot = kra * gka + krb * gkb
    for hd in range(ATTN_HEADS):
        sl = slice(hd * HEAD_PAD, (hd + 1) * HEAD_PAD)
        a = ka[:, sl]
        r = lax.rsqrt((jnp.sum(a * a, axis=-1, keepdims=True) + ss_rope) * (1.0 / QK_DIM) + EPS)
        k_out[:, sl] = ((a * gka + krot) * r).astype(BF16)

    zg = jax.nn.gelu(z[:, C_SG:C_SSM])
    u = zg[:, :SG_WIDTH]
    vv = zg[:, SG_WIDTH:]
    mmean = mmean_ref[...]
    vh, vl = _split_bf16(vv)
    mu = _dot(vh, mmean) + _dot(vl, mmean)
    xc = vv - mu
    sh, sl_ = _split_bf16(xc * xc)
    var = _dot(sh, mmean) + _dot(sl_, mmean)
    vn = (xc * lax.rsqrt(var + EPS) * lng_ref[...] + lnb_ref[...]).astype(BF16)
    tri = (lax.broadcasted_iota(jnp.int32, (SG_CHUNK, SG_CHUNK), 1)
           <= lax.broadcasted_iota(jnp.int32, (SG_CHUNK, SG_CHUNK), 0))
    lane_grp = lax.broadcasted_iota(jnp.int32, (SG_CHUNK, SG_WIDTH), 1) // SG_CH
    ws = [jnp.where(tri, sgw_ref[g], 0.0).astype(BF16) for g in range(SG_GROUPS)]
    for c in range(tm // SG_CHUNK):
        rows = slice(c * SG_CHUNK, (c + 1) * SG_CHUNK)
        vc = vn[rows, :]
        mixed = sgb_ref[...]
        for g in range(SG_GROUPS):
            mixed = mixed + jnp.where(lane_grp == g, _dot(ws[g], vc), 0.0)
        sg_out[rows, :] = u[rows, :] * mixed

    ssm_out[...] = z[:, C_SSM:C_END].astype(BF16)


def _mixer_in(h, lp, cos128, sin128, tm, n_pos_tiles):
    t = h.shape[0]
    const = lambda *shape: pl.BlockSpec(shape, lambda i: (0,) * len(shape))
    rows = lambda w: pl.BlockSpec((tm, w), lambda i: (i, 0))
    pos = pl.BlockSpec((tm, LANES), lambda i: (i % n_pos_tiles, 0))
    return pl.pallas_call(
        functools.partial(_mixer_in_kernel, tm=tm),
        grid=(t // tm,),
        in_specs=[rows(D_MODEL), const(1, D_MODEL), const(D_MODEL, C_END), const(1, Q_RANK),
                  const(1, KV_RANK), const(Q_RANK, QKV_PAD), const(Q_RANK, QKV_PAD),
                  const(KV_RANK, QKV_PAD), const(KV_RANK, QKV_PAD), pos, pos, const(8, LANES),
                  const(1, QKV_PAD), const(1, SG_WIDTH), const(1, SG_WIDTH),
                  const(SG_GROUPS, SG_CHUNK, SG_CHUNK), const(SG_CHUNK, SG_WIDTH),
                  const(SG_WIDTH, SG_WIDTH)],
        out_specs=[rows(QKV_PAD), rows(QKV_PAD), rows(QKV_PAD), rows(SG_WIDTH), rows(SSM_WIDTH)],
        out_shape=[jax.ShapeDtypeStruct((t, QKV_PAD), BF16), jax.ShapeDtypeStruct((t, QKV_PAD), BF16),
                   jax.ShapeDtypeStruct((t, QKV_PAD), BF16), jax.ShapeDtypeStruct((t, SG_WIDTH), F32),
                   jax.ShapeDtypeStruct((t, SSM_WIDTH), BF16)],
        compiler_params=pltpu.CompilerParams(dimension_semantics=("parallel",),
                                             vmem_limit_bytes=VMEM_LIMIT),
        name="mixer_in",
    )(h, lp["ln1"], lp["win"], lp["gcq"], lp["gckv"], lp["wqa"], lp["wqb"], lp["wk"], lp["wv"],
      cos128, sin128, lp["gqk"], lp["vones"], lp["sg_lng"], lp["sg_lnb"], lp["sg_w"], lp["sg_bias"],
      lp["mmean"])


def _attn_kernel(q_ref, k_ref, v_ref, o_ref, s_even, s_odd, m_ref, acc_ref, *, tq):
    i = pl.program_id(2)
    heads = [slice(hh * HEAD_PAD, (hh + 1) * HEAD_PAD) for hh in range(ATTN_HEADS_PER_STEP)]

    def block_rows(j):
        return pl.ds(pl.multiple_of(j * tq, tq), tq)

    def scores(j, s_ref):
        for hh, sl in enumerate(heads):
            s_ref[hh] = lax.dot_general(q_ref[:, sl], k_ref[block_rows(j), sl],
                                        (((1,), (1,)), ((), ())), preferred_element_type=F32)

    def accumulate(j, s_ref, masked):
        for hh, sl in enumerate(heads):
            s = s_ref[hh]
            if masked:
                keep = (lax.broadcasted_iota(jnp.int32, (tq, tq), 1)
                        <= lax.broadcasted_iota(jnp.int32, (tq, tq), 0))
                s = jnp.where(keep, s, -jnp.inf)
            m = m_ref[hh]
            m_new = jnp.maximum(m, jnp.max(s, axis=-1, keepdims=True))
            alpha = jnp.exp2(m - m_new)
            p = jnp.exp2(s - jnp.concatenate([m_new] * (tq // HEAD_PAD), axis=1)).astype(BF16)
            m_ref[hh] = m_new
            acc_ref[hh] = alpha * acc_ref[hh] + _dot(p, v_ref[block_rows(j), sl])

    m_ref[...] = jnp.full(m_ref.shape, -jnp.inf, F32)
    acc_ref[...] = jnp.zeros(acc_ref.shape, F32)
    scores(0, s_even)

    def pipelined(t, _):
        j = ATTN_BLOCKS_PER_ITER * t
        for u in range(0, ATTN_BLOCKS_PER_ITER, 2):
            scores(j + u + 1, s_odd)
            accumulate(j + u, s_even, False)
            scores(j + u + 2, s_even)
            accumulate(j + u + 1, s_odd, False)
        return 0

    n_pipelined = i // ATTN_BLOCKS_PER_ITER
    lax.fori_loop(0, n_pipelined, pipelined, 0)

    def leftover(j, _):
        accumulate(j, s_even, False)
        scores(j + 1, s_even)
        return 0

    lax.fori_loop(n_pipelined * ATTN_BLOCKS_PER_ITER, i, leftover, 0)
    accumulate(i, s_even, True)

    lane = lax.broadcasted_iota(jnp.int32, (tq, HEAD_PAD), 1)
    for hh, sl in enumerate(heads):
        acc = acc_ref[hh]
        out = acc / acc[:, V_DIM:V_DIM + 1]
        o_ref[:, sl] = jnp.where(lane < V_DIM, out, 0.0).astype(BF16)


def _attention(q, k, v, batch, seq, tq):
    t = q.shape[0]
    nq = seq // tq
    width = ATTN_HEADS_PER_STEP * HEAD_PAD
    return pl.pallas_call(
        functools.partial(_attn_kernel, tq=tq),
        grid=(batch, ATTN_HEADS // ATTN_HEADS_PER_STEP, nq),
        in_specs=[pl.BlockSpec((tq, width), lambda b, h, i: (b * nq + i, h)),
                  pl.BlockSpec((seq, width), lambda b, h, i: (b, h)),
                  pl.BlockSpec((seq, width), lambda b, h, i: (b, h))],
        out_specs=pl.BlockSpec((tq, width), lambda b, h, i: (b * nq + i, h)),
        out_shape=jax.ShapeDtypeStruct((t, QKV_PAD), BF16),
        scratch_shapes=[pltpu.VMEM((ATTN_HEADS_PER_STEP, tq, tq), F32),
                        pltpu.VMEM((ATTN_HEADS_PER_STEP, tq, tq), F32),
                        pltpu.VMEM((ATTN_HEADS_PER_STEP, tq, HEAD_PAD), F32),
                        pltpu.VMEM((ATTN_HEADS_PER_STEP, tq, HEAD_PAD), F32)],
        compiler_params=pltpu.CompilerParams(
            dimension_semantics=("parallel", "parallel", "arbitrary"), vmem_limit_bytes=VMEM_LIMIT),
        name="mla_attention",
    )(q, k, v)


def _ssm_kernel(x_ref, m_ref, pre_ref, pim_ref, qre_ref, qim_ref, are_ref, aim_ref, y_ref, *,
                rows_per_batch, n_steps):
    x = x_ref[0]
    cr = x.shape[0]
    y_local = _dot(x, m_ref[0])
    sre = _dot(x, pre_ref[0])
    sim = _dot(x, pim_ref[0])
    row = lax.broadcasted_iota(jnp.int32, (cr, SSM_SLANES), 0) % rows_per_batch
    are = are_ref[0]
    aim = aim_ref[0]
    for k in range(n_steps):
        sh = 1 << k
        ok = row >= sh
        pr = jnp.where(ok, pltpu.roll(sre, sh, 0), 0.0)
        pi = jnp.where(ok, pltpu.roll(sim, sh, 0), 0.0)
        ar = are[k:k + 1, :]
        ai = aim[k:k + 1, :]
        sre, sim = sre + ar * pr - ai * pi, sim + ar * pi + ai * pr
    ok = row >= 1
    hre = jnp.where(ok, pltpu.roll(sre, 1, 0), 0.0).astype(BF16)
    him = jnp.where(ok, pltpu.roll(sim, 1, 0), 0.0).astype(BF16)
    y_ref[0] = y_local + _dot(hre, qre_ref[0]) + _dot(him, qim_ref[0])


def _ssm(xp, lp, rows_per_batch):
    n_bundles, cr, _ = xp.shape
    n_steps = lp["ssm_are"].shape[1]
    blk = lambda *shape: pl.BlockSpec((1,) + shape, lambda p: (p,) + (0,) * len(shape))
    return pl.pallas_call(
        functools.partial(_ssm_kernel, rows_per_batch=rows_per_batch, n_steps=n_steps),
        grid=(n_bundles,),
        in_specs=[blk(cr, SSM_COLS), blk(SSM_COLS, SSM_COLS), blk(SSM_COLS, SSM_SLANES),
                  blk(SSM_COLS, SSM_SLANES), blk(SSM_SLANES, SSM_COLS), blk(SSM_SLANES, SSM_COLS),
                  blk(n_steps, SSM_SLANES), blk(n_steps, SSM_SLANES)],
        out_specs=blk(cr, SSM_COLS),
        out_shape=jax.ShapeDtypeStruct((n_bundles, cr, SSM_COLS), F32),
        compiler_params=pltpu.CompilerParams(dimension_semantics=("parallel",),
                                             vmem_limit_bytes=VMEM_LIMIT),
        name="s5_mixer",
    )(xp, lp["ssm_m"], lp["ssm_pre"], lp["ssm_pim"], lp["ssm_qre"], lp["ssm_qim"], lp["ssm_are"],
      lp["ssm_aim"])


def _mixer_out_kernel(a_ref, sg_ref, y_ref, h_ref, gma_ref, gmg_ref, gms_ref, woa_ref, wog_ref,
                      wos_ref, wglu_ref, bglu_ref, ln2_ref, wrh_ref, wrl_ref, br_ref, ltri_ref,
                      h1_out, xn_out, ri_out, rw_out, cnt_out, run_ref):
    i = pl.program_id(0)

    @pl.when(i == 0)
    def _():
        run_ref[...] = jnp.zeros_like(run_ref)

    g = jax.nn.gelu(y_ref[...])
    glu = g * jax.nn.sigmoid(_dot(g.astype(BF16), wglu_ref[...]) + bglu_ref[...])
    a_n = _rms(a_ref[...].astype(F32), gma_ref[...], ATTN_HEADS * V_DIM)
    g_n = _rms(sg_ref[...], gmg_ref[...])
    s_n = _rms(glu, gms_ref[...])
    h1 = (h_ref[...] + _dot(a_n.astype(BF16), woa_ref[...]) + _dot(g_n.astype(BF16), wog_ref[...])
          + _dot(s_n.astype(BF16), wos_ref[...]))
    h1_out[...] = h1
    xn = _rms(h1, ln2_ref[...])
    _store_token_rows(xn_out, _pack_bf16_pairs(xn))

    xh, xl = _split_bf16(xn)
    logit = _dot(xh, wrh_ref[...]) + _dot(xl, wrh_ref[...]) + _dot(xh, wrl_ref[...]) + br_ref[...]
    shape = logit.shape
    lane = lax.broadcasted_iota(jnp.int32, shape, 1)
    neg = -jnp.inf
    is_grp = lane < MOE_GROUPS
    gl = jnp.where(is_grp, logit, neg)
    gmax = jnp.max(gl, axis=-1, keepdims=True)
    gsel = jnp.min(jnp.where(gl == gmax, lane, LANES), axis=-1, keepdims=True)
    gsum = jnp.sum(jnp.where(is_grp, jnp.exp(logit - gmax), 0.0), axis=-1, keepdims=True)
    g_w = 1.0 / gsum
    lo = MOE_GROUPS + gsel * EXPERTS_PER_GROUP
    in_grp = (lane >= lo) & (lane < lo + EXPERTS_PER_GROUP)
    el = jnp.where(in_grp, logit, neg)
    m1 = jnp.max(el, axis=-1, keepdims=True)
    i1 = jnp.min(jnp.where(el == m1, lane, LANES), axis=-1, keepdims=True)
    el2 = jnp.where(lane == i1, neg, el)
    m2 = jnp.max(el2, axis=-1, keepdims=True)
    i2 = jnp.min(jnp.where(el2 == m2, lane, LANES), axis=-1, keepdims=True)
    e2 = jnp.exp(m2 - m1)
    w1 = g_w / (1.0 + e2)
    w2 = g_w * e2 / (1.0 + e2)
    e_a = i1 - MOE_GROUPS
    e_b = i2 - MOE_GROUPS

    hit_a = lane == e_a
    hit_b = lane == e_b
    onehot = jnp.where(hit_a | hit_b, 1.0, 0.0).astype(BF16)
    before = _dot(ltri_ref[...], onehot) + run_ref[...]
    c_a = jnp.sum(jnp.where(hit_a, before, 0.0), axis=-1, keepdims=True).astype(jnp.int32)
    c_b = jnp.sum(jnp.where(hit_b, before, 0.0), axis=-1, keepdims=True).astype(jnp.int32)
    run_ref[...] = run_ref[...] + jnp.sum(onehot.astype(F32), axis=0, keepdims=True)
    cnt_out[...] = jnp.broadcast_to(run_ref[...], cnt_out.shape)

    ri_out[...] = jnp.where(lane == 0, e_a, jnp.where(lane == 1, e_b, jnp.where(
        lane == 2, c_a, jnp.where(lane == 3, c_b, 0))))[:, :ROUTE_COLS]
    rw_out[...] = jnp.where(lane == 0, w1, jnp.where(lane == 1, w2, 0.0))[:, :ROUTE_COLS]


def _mixer_out(a, sg, y, h, lp, tm):
    t = h.shape[0]
    const = lambda *shape: pl.BlockSpec(shape, lambda i: (0,) * len(shape))
    rows = lambda w: pl.BlockSpec((tm, w), lambda i: (i, 0))
    return pl.pallas_call(
        _mixer_out_kernel,
        grid=(t // tm,),
        in_specs=[rows(QKV_PAD), rows(SG_WIDTH), rows(SSM_WIDTH), rows(D_MODEL), const(1, QKV_PAD),
                  const(1, SG_WIDTH), const(1, SSM_WIDTH), const(QKV_PAD, D_MODEL),
                  const(SG_WIDTH, D_MODEL), const(SSM_WIDTH, D_MODEL), const(SSM_WIDTH, SSM_WIDTH),
                  const(1, SSM_WIDTH), const(1, D_MODEL), const(D_MODEL, LANES), const(D_MODEL, LANES),
                  const(1, LANES), const(tm, tm)],
        out_specs=[rows(D_MODEL), pl.BlockSpec((tm * PACKED_ROWS, LANES), lambda i: (i, 0)),
                   rows(ROUTE_COLS), rows(ROUTE_COLS), const(8, LANES)],
        out_shape=[jax.ShapeDtypeStruct((t, D_MODEL), F32),
                   jax.ShapeDtypeStruct((t * PACKED_ROWS, LANES), jnp.uint32),
                   jax.ShapeDtypeStruct((t, ROUTE_COLS), jnp.int32),
                   jax.ShapeDtypeStruct((t, ROUTE_COLS), F32),
                   jax.ShapeDtypeStruct((8, LANES), F32)],
        scratch_shapes=[pltpu.VMEM((1, LANES), F32)],
        compiler_params=pltpu.CompilerParams(dimension_semantics=("arbitrary",),
                                             vmem_limit_bytes=VMEM_LIMIT),
        name="mixer_out_router",
    )(a, sg, y, h, lp["gm_a"], lp["gm_g"], lp["gm_s"], lp["wo_a"], lp["wo_g"], lp["wo_s"], lp["wglu"],
      lp["bglu"], lp["ln2"], lp["wr_hi"], lp["wr_lo"], lp["br"], lp["ltri"])


def _row_copy(src_ref, src_row, dst_ref, dst_row, sem):
    src = src_ref.at[pl.ds(pl.multiple_of(src_row * PACKED_ROWS, PACKED_ROWS), PACKED_ROWS)]
    return pltpu.make_async_copy(src, dst_ref.at[pl.ds(dst_row * PACKED_ROWS, PACKED_ROWS)], sem)


def _rows_wait(src_ref, dst_ref, n_tokens, sem):
    rows = pl.ds(0, n_tokens * PACKED_ROWS)
    pltpu.make_async_copy(src_ref.at[rows], dst_ref.at[rows], sem).wait()


def _store_token_rows(ref, words):
    n = words.shape[0]
    for c in range(PACKED_ROWS):
        ref[pl.ds(c, n, stride=PACKED_ROWS), :] = words[:, c * LANES:(c + 1) * LANES]


def _load_token_rows(ref):
    n = ref.shape[0] // PACKED_ROWS
    return jnp.concatenate([ref[pl.ds(c, n, stride=PACKED_ROWS), :] for c in range(PACKED_ROWS)], axis=1)


def _slot_map_kernel(dest_ref, zeros_ref, row_tok_ref, sem, *, td):
    i = pl.program_id(0)

    @pl.when(i == 0)
    def _():
        clear = pltpu.make_async_copy(zeros_ref, row_tok_ref, sem)
        clear.start()
        clear.wait()

    def place(r, _):
        row_tok_ref[dest_ref[0, 0, 2 * r]] = i * td + r
        row_tok_ref[dest_ref[0, 0, 2 * r + 1]] = i * td + r
        return 0

    lax.fori_loop(0, td, place, 0, unroll=SCALAR_LOOP_UNROLL)


def _slot_map(dest, n_rows, td):
    n_tiles = dest.shape[0]
    return pl.pallas_call(
        functools.partial(_slot_map_kernel, td=td),
        grid=(n_tiles,),
        in_specs=[pl.BlockSpec((1, 1, 2 * td), lambda i: (i, 0, 0), memory_space=pltpu.SMEM),
                  pl.BlockSpec(memory_space=pl.ANY)],
        out_specs=pl.BlockSpec(memory_space=pltpu.SMEM),
        out_shape=jax.ShapeDtypeStruct((n_rows,), jnp.int32),
        scratch_shapes=[pltpu.SemaphoreType.DMA(())],
        compiler_params=pltpu.CompilerParams(dimension_semantics=("arbitrary",)),
        name="moe_slot_map",
    )(dest, jnp.zeros((n_rows,), jnp.int32))


def _unpack_bf16_pairs(u, dtype):
    lo = lax.bitcast_convert_type(u << 16, F32)
    hi = lax.bitcast_convert_type(u & jnp.uint32(0xFFFF0000), F32)
    return jnp.concatenate([lo, hi], axis=1).astype(dtype)


def _pack_bf16_pairs(x):
    n = x.shape[1] // 2
    bits = lax.bitcast_convert_type(x.astype(BF16).astype(F32), jnp.uint32)
    return (bits[:, :n] >> 16) | (bits[:, n:] & jnp.uint32(0xFFFF0000))


def _expert_kernel(row_tok_ref, blk_e_ref, x_ref, w13_ref, w2_ref, yb_ref, xbuf, sem):
    del blk_e_ref
    b = pl.program_id(0)
    last = pl.num_programs(0) - 1
    n_buf = EXPERT_GATHER_AHEAD + 1

    def gather(block, to_slot):
        base = block * EXPERT_BLOCK
        for r in range(EXPERT_BLOCK):
            _row_copy(x_ref, row_tok_ref[base + r], xbuf.at[to_slot], r,
                      sem.at[to_slot]).start(priority=r % 2)

    def wait(of_slot):
        _rows_wait(x_ref, xbuf.at[of_slot], EXPERT_BLOCK, sem.at[of_slot])

    @pl.when(b == 0)
    def _():
        for ahead in range(EXPERT_GATHER_AHEAD):
            gather(ahead, ahead)

    gather(jnp.minimum(b + EXPERT_GATHER_AHEAD, last), (b + EXPERT_GATHER_AHEAD) % n_buf)
    slot = b % n_buf
    wait(slot)
    hh = _dot(_unpack_bf16_pairs(_load_token_rows(xbuf.at[slot]), BF16), w13_ref[...])
    hb = jax.nn.silu(hh[:, :D_EXPERT]) * hh[:, D_EXPERT:]
    _store_token_rows(yb_ref, _pack_bf16_pairs(_dot(hb.astype(BF16), w2_ref[...])))

    @pl.when(b == last)
    def _():
        for ahead in range(1, n_buf):
            wait((b + ahead) % n_buf)


def _experts(row_tok, blk_e, xn_packed, w13, w2):
    n_rows = row_tok.shape[0]
    grid_spec = pltpu.PrefetchScalarGridSpec(
        num_scalar_prefetch=2,
        grid=(n_rows // EXPERT_BLOCK,),
        in_specs=[pl.BlockSpec(memory_space=pl.ANY),
                  pl.BlockSpec((None, D_MODEL, 2 * D_EXPERT), lambda i, rt, be: (be[i], 0, 0)),
                  pl.BlockSpec((None, D_EXPERT, D_MODEL), lambda i, rt, be: (be[i], 0, 0))],
        out_specs=pl.BlockSpec((EXPERT_BLOCK * PACKED_ROWS, LANES), lambda i, rt, be: (i, 0)),
        scratch_shapes=[pltpu.VMEM((EXPERT_GATHER_AHEAD + 1, EXPERT_BLOCK * PACKED_ROWS, LANES), jnp.uint32),
                        pltpu.SemaphoreType.DMA((EXPERT_GATHER_AHEAD + 1,))],
    )
    return pl.pallas_call(
        _expert_kernel,
        grid_spec=grid_spec,
        out_shape=jax.ShapeDtypeStruct((n_rows * PACKED_ROWS, LANES), jnp.uint32),
        compiler_params=pltpu.CompilerParams(dimension_semantics=("arbitrary",),
                                             vmem_limit_bytes=VMEM_LIMIT),
        name="moe_experts",
    )(row_tok, blk_e, xn_packed, w13, w2)


def _combine_kernel(dest_ref, dest_next_ref, h_ref, rw_ref, p_ref, wpg_ref, bpg_ref,
                    wple_ref, yb_ref, out_ref, y0_ref, y1_ref, sem, *, tc):
    i = pl.program_id(0)
    last = pl.num_programs(0) - 1
    slot = i % 2

    def gather(tile_dest_ref, to_slot):
        for r in range(tc):
            _row_copy(yb_ref, tile_dest_ref[0, 0, 2 * r], y0_ref.at[to_slot], r, sem.at[to_slot]).start()
            _row_copy(yb_ref, tile_dest_ref[0, 0, 2 * r + 1], y1_ref.at[to_slot], r,
                      sem.at[to_slot]).start()

    def wait(of_slot):
        _rows_wait(yb_ref, y0_ref.at[of_slot], tc, sem.at[of_slot])
        _rows_wait(yb_ref, y1_ref.at[of_slot], tc, sem.at[of_slot])

    @pl.when(i == 0)
    def _():
        gather(dest_ref, 0)

    gather(dest_next_ref, 1 - slot)
    wait(slot)

    rw = rw_ref[...]
    y0 = _unpack_bf16_pairs(_load_token_rows(y0_ref.at[slot]), F32)
    y1 = _unpack_bf16_pairs(_load_token_rows(y1_ref.at[slot]), F32)
    h2 = h_ref[...] + rw[:, 0:1] * y0 + rw[:, 1:2] * y1
    gate = jax.nn.sigmoid(_dot(h2.astype(BF16), wpg_ref[...]) + bpg_ref[...])
    out_ref[...] = h2 + gate * _dot(p_ref[...].astype(BF16), wple_ref[...])

    @pl.when(i == last)
    def _():
        wait(1 - slot)


def _combine(dest, h1, rw, p_all, layer, lp, yb, tc):
    t = h1.shape[0]
    n_tiles = t // tc
    const = lambda *shape: pl.BlockSpec(shape, lambda i: (0,) * len(shape))
    rows = lambda w: pl.BlockSpec((tc, w), lambda i: (i, 0))
    p_rows = pl.BlockSpec((tc, PLE_DIM), lambda i: (layer * n_tiles + i, 0))
    grid_spec = pl.GridSpec(
        grid=(n_tiles,),
        in_specs=[pl.BlockSpec((1, 1, 2 * tc), lambda i: (i, 0, 0), memory_space=pltpu.SMEM),
                  pl.BlockSpec((1, 1, 2 * tc), lambda i: (jnp.minimum(i + 1, n_tiles - 1), 0, 0),
                               memory_space=pltpu.SMEM),
                  rows(D_MODEL), rows(ROUTE_COLS), p_rows, const(D_MODEL, D_MODEL), const(1, D_MODEL),
                  const(PLE_DIM, D_MODEL), pl.BlockSpec(memory_space=pl.ANY)],
        out_specs=rows(D_MODEL),
        scratch_shapes=[pltpu.VMEM((2, tc * PACKED_ROWS, LANES), jnp.uint32),
                        pltpu.VMEM((2, tc * PACKED_ROWS, LANES), jnp.uint32),
                        pltpu.SemaphoreType.DMA((2,))],
    )
    return pl.pallas_call(
        functools.partial(_combine_kernel, tc=tc),
        grid_spec=grid_spec,
        out_shape=jax.ShapeDtypeStruct((t, D_MODEL), F32),
        compiler_params=pltpu.CompilerParams(dimension_semantics=("arbitrary",),
                                             vmem_limit_bytes=VMEM_LIMIT),
        name="moe_combine_ple",
    )(dest, dest, h1, rw, p_all, lp["wpg"], lp["bpg"], lp["wple"], yb)


def _rope_lane_tables(seq):
    inv = 1.0 / (ROPE_THETA ** (jnp.arange(0, ROPE_DIM, 2, dtype=F32) / ROPE_DIM))
    ang = jnp.arange(seq, dtype=F32)[:, None] * inv[None, :]
    cos, sin = jnp.cos(ang), jnp.sin(ang)
    half = ROPE_DIM // 2
    cos128 = jnp.concatenate([jnp.ones((seq, NOPE_DIM), F32), cos, cos,
                              jnp.zeros((seq, LANES - QK_DIM), F32)], axis=1)
    sin128 = jnp.concatenate([jnp.zeros((seq, NOPE_DIM), F32), -sin, sin,
                              jnp.zeros((seq, LANES - QK_DIM), F32)], axis=1)
    del half
    return cos128, sin128


def _swap_rope_halves(w):
    half = ROPE_DIM // 2
    return jnp.concatenate([w[..., half:], w[..., :half]], axis=-1)


def _pad_heads(w, width):
    return jnp.pad(w, ((0, 0), (0, 0), (0, HEAD_PAD - width))).reshape(w.shape[0], QKV_PAD)


def _rope_gain_rows(g):
    pad = jnp.zeros((LANES - QK_DIM,), F32)
    ga = jnp.concatenate([g, pad])
    gb = jnp.concatenate([jnp.zeros((NOPE_DIM,), F32), _swap_rope_halves(g[NOPE_DIM:]), pad])
    return ga, gb


def _ssm_tables(lam_re, lam_im, log_dt, b_re, b_im, c_re, c_im, d_skip, n_steps):
    hp = lax.Precision.HIGHEST
    lr = jnp.minimum(lam_re, -1e-4)
    li = lam_im
    dt = jnp.exp(log_dt)
    mag = jnp.exp(lr * dt)
    ab_re, ab_im = mag * jnp.cos(li * dt), mag * jnp.sin(li * dt)
    den = lr * lr + li * li
    nr, ni = ab_re - 1.0, ab_im
    fr = (nr * lr + ni * li) / den
    fi = (ni * lr - nr * li) / den
    bb_re = fr[..., None] * b_re - fi[..., None] * b_im
    bb_im = fr[..., None] * b_im + fi[..., None] * b_re

    steps = jnp.arange(SSM_CHUNK + 1, dtype=F32)[:, None, None]
    pmag = jnp.exp(lr * dt * steps)
    pw_re, pw_im = pmag * jnp.cos(li * dt * steps), pmag * jnp.sin(li * dt * steps)

    w_re = pw_re[:SSM_CHUNK, :, :, None] * bb_re - pw_im[:SSM_CHUNK, :, :, None] * bb_im
    w_im = pw_re[:SSM_CHUNK, :, :, None] * bb_im + pw_im[:SSM_CHUNK, :, :, None] * bb_re
    taps = (jnp.einsum("gcn,tgnd->tgcd", c_re, w_re, precision=hp)
            - jnp.einsum("gcn,tgnd->tgcd", c_im, w_im, precision=hp))
    taps = taps.at[0].add(d_skip[:, :, None] * jnp.eye(SSM_CH, dtype=F32)[None])
    t_in = jnp.arange(SSM_CHUNK)[:, None]
    t_out = jnp.arange(SSM_CHUNK)[None, :]
    lag = t_out - t_in
    m = jnp.where((lag >= 0)[:, :, None, None, None], taps[jnp.clip(lag, 0)], 0.0)
    m = m.transpose(2, 0, 4, 1, 3)

    p_re = w_re[::-1].transpose(1, 0, 3, 2)
    p_im = w_im[::-1].transpose(1, 0, 3, 2)

    up_re, up_im = pw_re[1:], pw_im[1:]
    q_re = (c_re[None] * up_re[:, :, None, :] - c_im[None] * up_im[:, :, None, :])
    q_im = -(c_re[None] * up_im[:, :, None, :] + c_im[None] * up_re[:, :, None, :])
    q_re = q_re.transpose(1, 3, 0, 2)
    q_im = q_im.transpose(1, 3, 0, 2)

    are, aim = [pw_re[SSM_CHUNK]], [pw_im[SSM_CHUNK]]
    for _ in range(n_steps - 1):
        r, im = are[-1], aim[-1]
        are.append(r * r - im * im)
        aim.append(2.0 * r * im)
    are, aim = jnp.stack(are, axis=1), jnp.stack(aim, axis=1)

    nbu = SSM_GROUPS // SSM_BUNDLE
    per_group = SSM_CHUNK * SSM_CH
    bundle = lambda x: x.reshape((nbu, SSM_BUNDLE) + x.shape[1:])
    col = jnp.arange(SSM_COLS)
    col_src = (col // (SSM_BUNDLE * SSM_CH)) * SSM_CH + col % SSM_CH
    col_grp = (col // SSM_CH) % SSM_BUNDLE
    sel_io = ((col_grp[None, None, :] == jnp.arange(SSM_BUNDLE)[:, None, None])
              & (col_src[None, None, :] == jnp.arange(per_group)[None, :, None])).astype(F32)
    lane = jnp.arange(SSM_SLANES)
    sel_st = ((lane[None, None, :] // SSM_STATE == jnp.arange(SSM_BUNDLE)[:, None, None])
              & (lane[None, None, :] % SSM_STATE == jnp.arange(SSM_STATE)[None, :, None])).astype(F32)

    def expand(x, rows_sel, cols_sel):
        return jnp.einsum("gaR,qgab,gbS->qRS", rows_sel, bundle(x), cols_sel).astype(BF16)

    def lanes(x):
        return bundle(x).transpose(0, 2, 1, 3).reshape(nbu, x.shape[1], SSM_SLANES)

    flat_io = lambda x: x.reshape(SSM_GROUPS, per_group, -1)
    return dict(ssm_m=expand(m.reshape(SSM_GROUPS, per_group, per_group), sel_io, sel_io),
                ssm_pre=expand(flat_io(p_re), sel_io, sel_st), ssm_pim=expand(flat_io(p_im), sel_io, sel_st),
                ssm_qre=expand(q_re.reshape(SSM_GROUPS, SSM_STATE, per_group), sel_st, sel_io),
                ssm_qim=expand(q_im.reshape(SSM_GROUPS, SSM_STATE, per_group), sel_st, sel_io),
                ssm_are=lanes(are), ssm_aim=lanes(aim))


def _layer_params(i, tm, n_scan_steps, ln1, w_in, g_cq, g_ckv, w_uq, w_ukv, g_qh, g_kh, sg_ln_g,
                  sg_ln_b, sg_w, sg_b, lam_re, lam_im, log_dt, ssm_b_re, ssm_b_im, ssm_c_re, ssm_c_im,
                  ssm_d, w_glu, b_glu, g_mix, w_out, ln2, w_rg, b_rg, w_rexp, b_rexp, w1, w3, w2, w_pg,
                  b_pg, w_ple):
    lp = {}
    wi = w_in[i]
    off_kr = Q_RANK + KV_RANK
    off_sg = off_kr + ROPE_DIM
    kr = wi[:, off_kr:off_sg]
    zpad = lambda n: jnp.zeros((D_MODEL, n), F32)
    kra = jnp.concatenate([zpad(NOPE_DIM), kr, zpad(LANES - QK_DIM)], axis=1)
    krb = jnp.concatenate([zpad(NOPE_DIM), _swap_rope_halves(kr), zpad(LANES - QK_DIM)], axis=1)
    lp["win"] = jnp.concatenate([wi[:, :off_kr], kra, krb, wi[:, off_sg:]], axis=1).astype(BF16)
    lp["ln1"] = ln1[i][None]
    lp["gcq"] = g_cq[i][None]
    lp["gckv"] = g_ckv[i][None]

    wq = w_uq[i].reshape(Q_RANK, ATTN_HEADS, QK_DIM)
    lp["wqa"] = _pad_heads(wq, QK_DIM).astype(BF16)
    wq_sw = jnp.concatenate([jnp.zeros((Q_RANK, ATTN_HEADS, NOPE_DIM), F32),
                             _swap_rope_halves(wq[..., NOPE_DIM:])], axis=-1)
    lp["wqb"] = _pad_heads(wq_sw, QK_DIM).astype(BF16)
    wkv = w_ukv[i].reshape(KV_RANK, ATTN_HEADS, NOPE_DIM + V_DIM)
    lp["wk"] = _pad_heads(wkv[..., :NOPE_DIM], NOPE_DIM).astype(BF16)
    lp["wv"] = _pad_heads(wkv[..., NOPE_DIM:], V_DIM).astype(BF16)
    gq_a, gq_b = _rope_gain_rows(g_qh[i])
    gk_a, gk_b = _rope_gain_rows(g_kh[i])
    lp["gqk"] = jnp.concatenate([jnp.stack([gq_a, gq_b, gk_a, gk_b]), jnp.zeros((4, LANES), F32)])
    lp["vones"] = jnp.tile((jnp.arange(HEAD_PAD) == V_DIM).astype(F32), ATTN_HEADS)[None]

    lp["sg_lng"] = sg_ln_g[i][None]
    lp["sg_lnb"] = sg_ln_b[i][None]
    lp["sg_w"] = sg_w[i]
    lp["sg_bias"] = jnp.repeat(sg_b[i].T, SG_CH, axis=1)
    grp = jnp.arange(SG_WIDTH) // SG_CH
    lp["mmean"] = ((grp[:, None] == grp[None, :]).astype(F32) / SG_CH).astype(BF16)

    lp.update(_ssm_tables(lam_re[i], lam_im[i], log_dt[i], ssm_b_re[i], ssm_b_im[i], ssm_c_re[i],
                          ssm_c_im[i], ssm_d[i], n_scan_steps))

    gm = g_mix[i]
    aw = ATTN_HEADS * V_DIM
    lp["gm_a"] = _pad_heads(gm[:aw].reshape(1, ATTN_HEADS, V_DIM), V_DIM)
    lp["gm_g"] = gm[aw:aw + SG_WIDTH][None]
    lp["gm_s"] = gm[aw + SG_WIDTH:][None]
    wo = w_out[i]
    woa = wo[:aw].reshape(ATTN_HEADS, V_DIM, D_MODEL)
    lp["wo_a"] = jnp.pad(woa, ((0, 0), (0, HEAD_PAD - V_DIM), (0, 0))).reshape(QKV_PAD, D_MODEL).astype(BF16)
    lp["wo_g"] = wo[aw:aw + SG_WIDTH].astype(BF16)
    lp["wo_s"] = wo[aw + SG_WIDTH:].astype(BF16)
    lp["wglu"] = w_glu[i].astype(BF16)
    lp["bglu"] = b_glu[i][None]
    lp["ln2"] = ln2[i][None]
    wr = jnp.concatenate([w_rg[i], w_rexp[i], jnp.zeros((D_MODEL, LANES - MOE_GROUPS - N_EXPERTS), F32)],
                         axis=1)
    lp["wr_hi"], lp["wr_lo"] = _split_bf16(wr)
    lp["br"] = jnp.concatenate([b_rg[i], b_rexp[i], jnp.zeros((LANES - MOE_GROUPS - N_EXPERTS,), F32)])[None]
    lp["ltri"] = (jnp.arange(tm)[None, :] < jnp.arange(tm)[:, None]).astype(BF16)

    lp["w13"] = jnp.concatenate([w1[i], w3[i]], axis=-1).astype(BF16)
    lp["w2"] = w2[i].astype(BF16)
    lp["wpg"] = w_pg[i].astype(BF16)
    lp["bpg"] = b_pg[i][None]
    lp["wple"] = w_ple[i].astype(BF16)
    return lp


def kernel(x, p, ln1, w_in, g_cq, g_ckv, w_uq, w_ukv, g_qh, g_kh, sg_ln_g, sg_ln_b, sg_w, sg_b, lam_re, lam_im, log_dt, ssm_b_re, ssm_b_im, ssm_c_re, ssm_c_im, ssm_d, w_glu, b_glu, g_mix, w_out, ln2, w_rg, b_rg, w_rexp, b_rexp, w1, w3, w2, w_pg, b_pg, w_ple):
    batch, seq, _ = x.shape
    depth = p.shape[0]
    t = batch * seq
    tm = min(512, seq)
    tq = min(512, seq)
    td = min(1024, seq)
    tc = min(256, seq)
    assert seq % tm == 0 and seq % SSM_CHUNK == 0 and tm % SG_CHUNK == 0
    rows_per_batch = seq // SSM_CHUNK
    n_scan_steps = max(1, (rows_per_batch - 1).bit_length())
    cr = t // SSM_CHUNK
    n_bundles = SSM_GROUPS // SSM_BUNDLE
    bundle_w = SSM_BUNDLE * SSM_CH
    tk2 = 2 * t
    n_blocks = -(-(tk2 + N_EXPERTS * (EXPERT_BLOCK - 1)) // EXPERT_BLOCK)
    n_rows = n_blocks * EXPERT_BLOCK

    cos128, sin128 = _rope_lane_tables(seq)
    h = x.reshape(t, D_MODEL)
    p_all = p.reshape(depth * t, PLE_DIM)
    weights = (ln1, w_in, g_cq, g_ckv, w_uq, w_ukv, g_qh, g_kh, sg_ln_g, sg_ln_b, sg_w, sg_b, lam_re,
               lam_im, log_dt, ssm_b_re, ssm_b_im, ssm_c_re, ssm_c_im, ssm_d, w_glu, b_glu, g_mix, w_out,
               ln2, w_rg, b_rg, w_rexp, b_rexp, w1, w3, w2, w_pg, b_pg, w_ple)
    for i in range(depth):
        lp = _layer_params(i, tm, n_scan_steps, *weights)
        q, k, v, sg, xs_ssm = _mixer_in(h, lp, cos128, sin128, tm, seq // tm)
        a = _attention(q, k, v, batch, seq, tq)

        xp = xs_ssm.reshape(cr, SSM_CHUNK, n_bundles, bundle_w).transpose(2, 0, 1, 3)
        yp = _ssm(xp.reshape(n_bundles, cr, SSM_COLS), lp, rows_per_batch)
        y = yp.reshape(n_bundles, cr, SSM_CHUNK, bundle_w).transpose(1, 2, 0, 3).reshape(t, SSM_WIDTH)

        h1, xn, ri, rw, cnt = _mixer_out(a, sg, y, h, lp, tm)

        counts = cnt[0, :N_EXPERTS].astype(jnp.int32)
        pcounts = (counts + EXPERT_BLOCK - 1) // EXPERT_BLOCK * EXPERT_BLOCK
        pend = jnp.cumsum(pcounts)
        pstart = (pend - pcounts).astype(jnp.int32)
        blk_first_row = jnp.arange(n_blocks, dtype=jnp.int32) * EXPERT_BLOCK
        blk_e = jnp.minimum(jnp.sum((pend[None, :] <= blk_first_row[:, None]).astype(jnp.int32), axis=1),
                            N_EXPERTS - 1)
        is_e = ri[:, 0:2, None] == jnp.arange(N_EXPERTS, dtype=jnp.int32)
        dest = jnp.sum(jnp.where(is_e, pstart, 0), axis=-1) + ri[:, 2:4]
        row_tok = _slot_map(dest.reshape(t // td, 1, 2 * td), n_rows, td)
        yb = _experts(row_tok, blk_e, xn, lp["w13"], lp["w2"])
        h = _combine(dest.reshape(t // tc, 1, 2 * tc), h1, rw, p_all, i, lp, yb, tc)
    return h.reshape(batch, seq, D_MODEL)
```
